```python
import jax, jax.numpy as jnp
from jax import lax
import numpy as np

D_MODEL = 2048
BATCH = 2
SEQ = 16384
DEPTH = 2

GRID_W = 64
CTX_LEN = 256
N_EVEN = (DEPTH + 1) // 2
N_ODD = DEPTH // 2
EPS = 1e-6
ROPE_BASE = 10000.0

GLA_HEADS = 4
GLA_DK = 128
GLA_DV = 256
GLA_GATE_RANK = 16
GLA_GATE_NORM = 16.0
GLA_CHUNK = 64

MLA_HEADS = 8
MLA_Q_RANK = 768
MLA_KV_RANK = 512
MLA_NOPE = 128
MLA_ROPE = 64
MLA_DV = 128
MLA_QBLOCK = 128
MLA_SCALE = (MLA_NOPE + MLA_ROPE) ** -0.5

RET_HEADS = 8
RET_DK = 256
RET_DV = 256
RET_CHUNK = 128

D_FF = 5632
N_EXPERTS = 8
TOP_K = 2

EVEN_SIZES = (GLA_HEADS * GLA_DK, GLA_HEADS * GLA_DK, GLA_HEADS * GLA_DV, GLA_HEADS * GLA_DV,
              GLA_GATE_RANK, GLA_GATE_RANK, MLA_Q_RANK, MLA_KV_RANK, MLA_ROPE)
EVEN_IN = sum(EVEN_SIZES)
EVEN_MIX = GLA_HEADS * GLA_DV + MLA_HEADS * MLA_DV
ODD_SIZES = (RET_HEADS * RET_DK, RET_HEADS * RET_DK, RET_HEADS * RET_DV, RET_HEADS * RET_DV)
ODD_IN = sum(ODD_SIZES)
ODD_MIX = RET_HEADS * RET_DV

kernel_name = "hybrid_gla_mla_retention_moe_dit_block"

F32 = jnp.float32


def _split(a, sizes):
    out, start = [], 0
    for s in sizes:
        out.append(a[..., start:start + s])
        start += s
    return out


def heads(a, n_heads):
    B, T, _ = a.shape
    return a.reshape(B, T, n_heads, -1).transpose(0, 2, 1, 3)


def merge_heads(a):
    B, H, T, d = a.shape
    return a.transpose(0, 2, 1, 3).reshape(B, T, H * d)


def rev(a):
    return jnp.flip(a, axis=2)


def rmsnorm(x, g):
    x32 = x.astype(F32)
    y = x32 * lax.rsqrt(jnp.mean(x32 * x32, axis=-1, keepdims=True) + EPS)
    return (y * g.astype(F32)).astype(x.dtype)


def head_rmsnorm(o, g):
    return o * lax.rsqrt(jnp.mean(o * o, axis=-1, keepdims=True) + EPS) * g.astype(F32)


def head_groupnorm(o, g):
    mu = jnp.mean(o, axis=-1, keepdims=True)
    d = o - mu
    return d * lax.rsqrt(jnp.mean(d * d, axis=-1, keepdims=True) + EPS) * g.astype(F32)


def axial_rope(rows, dim):
    row = jnp.repeat(jnp.arange(rows, dtype=F32), GRID_W)
    col = jnp.tile(jnp.arange(GRID_W, dtype=F32), rows)
    half = dim // 2
    inv = 1.0 / (ROPE_BASE ** (jnp.arange(0, half, 2, dtype=F32) / half))
    ang = jnp.concatenate([row[:, None] * inv, col[:, None] * inv], axis=-1)
    return jnp.cos(ang), jnp.sin(ang)


def apply_rope(x, cos, sin):
    xp = x.reshape(x.shape[:-1] + (x.shape[-1] // 2, 2)).astype(F32)
    x0, x1 = xp[..., 0], xp[..., 1]
    out = jnp.stack([x0 * cos - x1 * sin, x0 * sin + x1 * cos], axis=-1)
    return out.reshape(x.shape).astype(x.dtype)


def gla_chunk_scan(q, k, v, log_a, s0):
    B, H, T, dk = q.shape
    dv = v.shape[-1]
    n = T // GLA_CHUNK

    def to_chunks(a):
        return jnp.moveaxis(a.reshape(B, H, n, GLA_CHUNK, a.shape[-1]), 2, 0)

    lower = jnp.tril(jnp.ones((GLA_CHUNK, GLA_CHUNK), dtype=bool))

    def step(S, inp):
        qc, kc, vc, ac = inp
        b = jnp.cumsum(ac, axis=-2)
        b_last = b[..., -1:, :]
        qe = qc * jnp.exp(b)
        att = jnp.einsum('bhid,bhjd->bhij', qe, kc * jnp.exp(-b))
        att = jnp.where(lower, att, 0.0)
        o = jnp.einsum('bhij,bhjv->bhiv', att, vc) + jnp.einsum('bhid,bhdv->bhiv', qe, S)
        S = (jnp.swapaxes(jnp.exp(b_last), -1, -2) * S
             + jnp.einsum('bhjd,bhjv->bhdv', kc * jnp.exp(b_last - b), vc))
        return S, o

    S, o = lax.scan(step, s0, (to_chunks(q), to_chunks(k), to_chunks(v), to_chunks(log_a)))
    return jnp.moveaxis(o, 0, 2).reshape(B, H, T, dv), S


def retention_chunk_scan(q, k, v, log_gamma, s0):
    B, H, T, dk = q.shape
    dv = v.shape[-1]
    n = T // RET_CHUNK
    pos = jnp.arange(RET_CHUNK, dtype=F32)
    diff = pos[:, None] - pos[None, :]
    decay = jnp.where(diff >= 0, jnp.exp(log_gamma[:, None, None] * jnp.maximum(diff, 0.0)), 0.0)
    q_dec = jnp.exp(log_gamma[:, None] * (pos + 1.0))[..., None]
    k_dec = jnp.exp(log_gamma[:, None] * (RET_CHUNK - 1.0 - pos))[..., None]
    c_dec = jnp.exp(log_gamma * RET_CHUNK)[:, None, None]

    def to_chunks(a):
        return jnp.moveaxis(a.reshape(B, H, n, RET_CHUNK, a.shape[-1]), 2, 0)

    def step(S, inp):
        qc, kc, vc = inp
        att = jnp.einsum('bhid,bhjd->bhij', qc, kc) * decay
        o = jnp.einsum('bhij,bhjv->bhiv', att, vc) + jnp.einsum('bhid,bhdv->bhiv', qc * q_dec, S)
        S = c_dec * S + jnp.einsum('bhjd,bhjv->bhdv', kc * k_dec, vc)
        return S, o

    S, o = lax.scan(step, s0, (to_chunks(q), to_chunks(k), to_chunks(v)))
    return jnp.moveaxis(o, 0, 2).reshape(B, H, T, dv), S


def mla_attend(qn, qr, kn, kr, v):
    s = jnp.einsum('bhqd,bhkd->bhqk', qn, kn) + jnp.einsum('bhqr,bkr->bhqk', qr, kr)
    p = jax.nn.softmax(s.astype(F32) * MLA_SCALE, axis=-1)
    return jnp.einsum('bhqk,bhkv->bhqv', p.astype(v.dtype), v)


def even_mixer(h, hc, cos, sin, w_in, w_g2, b_g2, gla_g, q_norm, w_uq, kv_norm, w_ukv, w_out, need_ctx):
    B, T, _ = h.shape
    gq, gk, gv, gr, gaf, gab, mq, mkv, mkr = _split(h @ w_in, EVEN_SIZES)
    cgq, cgk, cgv, cgr, cgaf, cgab, cmq, cmkv, cmkr = _split(hc @ w_in, EVEN_SIZES)

    def gla_prep(q, k, v, af, ab):
        lg = [heads(jax.nn.log_sigmoid((a @ w_g2[d] + b_g2[d]).astype(F32)) / GLA_GATE_NORM, GLA_HEADS)
              for d, a in enumerate((af, ab))]
        return (heads(q, GLA_HEADS).astype(F32) * GLA_DK ** -0.5, heads(k, GLA_HEADS).astype(F32),
                heads(v, GLA_HEADS).astype(F32), lg[0], lg[1])

    lq, lk, lv, laf, lab = gla_prep(gq, gk, gv, gaf, gab)
    cq, ck, cv, caf, cab = gla_prep(cgq, cgk, cgv, cgaf, cgab)
    s0 = jnp.zeros((B, GLA_HEADS, GLA_DK, GLA_DV), F32)
    o_cf, s_f = gla_chunk_scan(cq, ck, cv, caf, s0)
    o_cb, s_b = gla_chunk_scan(rev(cq), rev(ck), rev(cv), rev(cab), s0)
    o_f, _ = gla_chunk_scan(lq, lk, lv, laf, s_f)
    o_b, _ = gla_chunk_scan(rev(lq), rev(lk), rev(lv), rev(lab), s_b)
    gla_lat = merge_heads(head_rmsnorm(o_f + rev(o_b), gla_g)).astype(h.dtype) * jax.nn.silu(gr)

    def mla_prep(q_lat, kv_lat):
        q = heads(rmsnorm(q_lat, q_norm) @ w_uq, MLA_HEADS)
        kv = heads(rmsnorm(kv_lat, kv_norm) @ w_ukv, MLA_HEADS)
        return q[..., :MLA_NOPE], q[..., MLA_NOPE:], kv[..., :MLA_NOPE], kv[..., MLA_NOPE:]

    qn, qr, kn, v = mla_prep(mq, mkv)
    cqn, cqr, ckn, cvv = mla_prep(cmq, cmkv)
    qr = apply_rope(qr, cos, sin)
    kr = apply_rope(mkr, cos, sin)
    kn_all = jnp.concatenate([kn, ckn], axis=2)
    kr_all = jnp.concatenate([kr, cmkr], axis=1)
    v_all = jnp.concatenate([v, cvv], axis=2)
    nblk = T // MLA_QBLOCK
    qn_b = jnp.moveaxis(qn.reshape(B, MLA_HEADS, nblk, MLA_QBLOCK, MLA_NOPE), 2, 0)
    qr_b = jnp.moveaxis(qr.reshape(B, MLA_HEADS, nblk, MLA_QBLOCK, MLA_ROPE), 2, 0)
    o = lax.map(lambda qs: mla_attend(qs[0], qs[1], kn_all, kr_all, v_all), (qn_b, qr_b))
    mla_lat = merge_heads(jnp.moveaxis(o, 0, 2).reshape(B, MLA_HEADS, T, MLA_DV))

    y = jnp.concatenate([gla_lat, mla_lat], axis=-1) @ w_out
    if not need_ctx:
        return y, None
    gla_ctx = merge_heads(head_rmsnorm(o_cf + rev(o_cb), gla_g)).astype(hc.dtype) * jax.nn.silu(cgr)
    mla_ctx = merge_heads(mla_attend(cqn, cqr, ckn, cmkr, cvv))
    yc = jnp.concatenate([gla_ctx, mla_ctx], axis=-1) @ w_out
    return y, yc


def odd_mixer(h, hc, cos, sin, w_in, log_decay, ret_g, w_out, need_ctx):
    B = h.shape[0]
    lq, lk, lv, lg_gate = _split(h @ w_in, ODD_SIZES)
    cq, ck, cv, cg_gate = _split(hc @ w_in, ODD_SIZES)

    def prep(q, k, v):
        return (heads(q, RET_HEADS).astype(F32), heads(k, RET_HEADS).astype(F32) * RET_DK ** -0.5,
                heads(v, RET_HEADS).astype(F32))

    lq, lk, lv = prep(lq, lk, lv)
    cq, ck, cv = prep(cq, ck, cv)
    lq = apply_rope(lq, cos, sin)
    lk = apply_rope(lk, cos, sin)
    log_gamma = -jnp.exp(log_decay.astype(F32))
    s0 = jnp.zeros((B, RET_HEADS, RET_DK, RET_DV), F32)
    o_cf, s_f = retention_chunk_scan(cq, ck, cv, log_gamma[0], s0)
    o_cb, s_b = retention_chunk_scan(rev(cq), rev(ck), rev(cv), log_gamma[1], s0)
    o_f, _ = retention_chunk_scan(lq, lk, lv, log_gamma[0], s_f)
    o_b, _ = retention_chunk_scan(rev(lq), rev(lk), rev(lv), log_gamma[1], s_b)
    y = (merge_heads(head_groupnorm(o_f + rev(o_b), ret_g)).astype(h.dtype) * jax.nn.silu(lg_gate)) @ w_out
    if not need_ctx:
        return y, None
    yc = (merge_heads(head_groupnorm(o_cf + rev(o_cb), ret_g)).astype(hc.dtype) * jax.nn.silu(cg_gate)) @ w_out
    return y, yc


def swiglu(h, wg, wu, wd):
    return (jax.nn.silu(h @ wg) * (h @ wu)) @ wd


def moe_ffn(h, w_router, w_gate, w_up, w_down):
    logits = (h @ w_router).astype(F32)
    top_val, top_idx = lax.top_k(logits, TOP_K)
    weights = jax.nn.softmax(top_val, axis=-1)
    gates = jnp.sum(jax.nn.one_hot(top_idx, N_EXPERTS, dtype=F32) * weights[..., None], axis=-2)
    y = jnp.zeros_like(h)
    for e in range(N_EXPERTS):
        y = y + gates[..., e:e + 1].astype(h.dtype) * swiglu(h, w_gate[e], w_up[e], w_down[e])
    return y


def setup_inputs(seed: int = 0) -> dict:
    key = jax.random.key(seed)
    ks = iter(jax.random.split(key, 40))

    def normal(shape, scale=1.0):
        return jax.random.normal(next(ks), shape, F32) * scale

    def nrm(shape, fan_in, scale=1.0):
        return normal(shape, scale * fan_in ** -0.5)

    def gain(shape):
        return 1.0 + normal(shape, 0.02)

    ret_init = jnp.log(-jnp.log1p(-(2.0 ** (-5.0 - jnp.arange(RET_HEADS, dtype=F32)))))
    return {
        "x": normal((BATCH, SEQ, D_MODEL)),
        "c": normal((BATCH, D_MODEL)),
        "ctx": normal((BATCH, CTX_LEN, D_MODEL)),
        "c_ctx": normal((D_MODEL,)),
        "ada_w": nrm((DEPTH, D_MODEL, 6 * D_MODEL), D_MODEL, 0.5),
        "ada_b": normal((DEPTH, 6 * D_MODEL), 0.02),
        "norm_mix": gain((DEPTH, D_MODEL)),
        "norm_ffn": gain((DEPTH, D_MODEL)),
        "norm_final": gain((D_MODEL,)),
        "ev_w_in": nrm((N_EVEN, D_MODEL, EVEN_IN), D_MODEL),
        "gla_w_gate2": nrm((N_EVEN, 2, GLA_GATE_RANK, GLA_HEADS * GLA_DK), GLA_GATE_RANK),
        "gla_b_gate2": normal((N_EVEN, 2, GLA_HEADS * GLA_DK), 0.1),
        "gla_norm": gain((N_EVEN, GLA_DV)),
        "mla_q_norm": gain((N_EVEN, MLA_Q_RANK)),
        "mla_w_uq": nrm((N_EVEN, MLA_Q_RANK, MLA_HEADS * (MLA_NOPE + MLA_ROPE)), MLA_Q_RANK),
        "mla_kv_norm": gain((N_EVEN, MLA_KV_RANK)),
        "mla_w_ukv": nrm((N_EVEN, MLA_KV_RANK, MLA_HEADS * (MLA_NOPE + MLA_DV)), MLA_KV_RANK),
        "ev_w_out": nrm((N_EVEN, EVEN_MIX, D_MODEL), EVEN_MIX),
        "ffn_w_gate": nrm((N_EVEN, D_MODEL, D_FF), D_MODEL),
        "ffn_w_up": nrm((N_EVEN, D_MODEL, D_FF), D_MODEL),
        "ffn_w_down": nrm((N_EVEN, D_FF, D_MODEL), D_FF),
        "od_w_in": nrm((N_ODD, D_MODEL, ODD_IN), D_MODEL),
        "ret_log_decay": ret_init[None, None, :] + normal((N_ODD, 2, RET_HEADS), 0.02),
        "ret_norm": gain((N_ODD, RET_DV)),
        "od_w_out": nrm((N_ODD, ODD_MIX, D_MODEL), ODD_MIX),
        "moe_router": nrm((N_ODD, D_MODEL, N_EXPERTS), D_MODEL),
        "moe_w_gate": nrm((N_ODD, N_EXPERTS, D_MODEL, D_FF), D_MODEL),
        "moe_w_up": nrm((N_ODD, N_EXPERTS, D_MODEL, D_FF), D_MODEL),
        "moe_w_down": nrm((N_ODD, N_EXPERTS, D_FF, D_MODEL), D_FF),
    }


def reference(x, c, ctx, c_ctx, ada_w, ada_b, norm_mix, norm_ffn, norm_final,
              ev_w_in, gla_w_gate2, gla_b_gate2, gla_norm, mla_q_norm, mla_w_uq, mla_kv_norm, mla_w_ukv,
              ev_w_out, ffn_w_gate, ffn_w_up, ffn_w_down,
              od_w_in, ret_log_decay, ret_norm, od_w_out, moe_router, moe_w_gate, moe_w_up, moe_w_down):
    T = x.shape[1]
    ROWS = T // GRID_W
    cos_m, sin_m = axial_rope(ROWS, MLA_ROPE)
    cos_r, sin_r = axial_rope(ROWS, RET_DK)
    xc = ctx
    for i in range(DEPTH):
        last = i == DEPTH - 1
        m_lat = (jax.nn.silu(c) @ ada_w[i] + ada_b[i])[:, None, :]
        m_ctx = (jax.nn.silu(c_ctx) @ ada_w[i] + ada_b[i])[None, None, :]
        sh_a, sc_a, g_a, sh_f, sc_f, g_f = jnp.split(m_lat, 6, axis=-1)
        csh_a, csc_a, cg_a, csh_f, csc_f, cg_f = jnp.split(m_ctx, 6, axis=-1)

        h = rmsnorm(x, norm_mix[i]) * (1.0 + sc_a) + sh_a
        hc = rmsnorm(xc, norm_mix[i]) * (1.0 + csc_a) + csh_a
        j = i // 2
        if i % 2 == 0:
            y, yc = even_mixer(h, hc, cos_m, sin_m, ev_w_in[j], gla_w_gate2[j], gla_b_gate2[j], gla_norm[j],
                               mla_q_norm[j], mla_w_uq[j], mla_kv_norm[j], mla_w_ukv[j], ev_w_out[j],
                               not last)
        else:
            y, yc = odd_mixer(h, hc, cos_r, sin_r, od_w_in[j], ret_log_decay[j], ret_norm[j], od_w_out[j],
                              not last)
        x = x + g_a * y
        h = rmsnorm(x, norm_ffn[i]) * (1.0 + sc_f) + sh_f
        if i % 2 == 0:
            x = x + g_f * swiglu(h, ffn_w_gate[j], ffn_w_up[j], ffn_w_down[j])
        else:
            x = x + g_f * moe_ffn(h, moe_router[j], moe_w_gate[j], moe_w_up[j], moe_w_down[j])

        if not last:
            xc = xc + cg_a * yc
            hc = rmsnorm(xc, norm_ffn[i]) * (1.0 + csc_f) + csh_f
            if i % 2 == 0:
                xc = xc + cg_f * swiglu(hc, ffn_w_gate[j], ffn_w_up[j], ffn_w_down[j])
            else:
                xc = xc + cg_f * moe_ffn(hc, moe_router[j], moe_w_gate[j], moe_w_up[j], moe_w_down[j])
    return rmsnorm(x, norm_final)
```

```python
import functools

import jax
import jax.numpy as jnp
import numpy as np
from jax import lax
from jax.experimental import pallas as pl
from jax.experimental.pallas import tpu as pltpu

F32 = jnp.float32
BF16 = jnp.bfloat16
EPS = 1e-6
ROPE_BASE = 10000.0
GRID_W = 64

GLA_HEADS, GLA_DK, GLA_DV, GLA_RANK, GLA_GATE_NORM, GLA_CHUNK = 4, 128, 256, 16, 16.0, 64
MLA_HEADS, MLA_Q_RANK, MLA_KV_RANK, MLA_NOPE, MLA_ROPE, MLA_DV = 8, 768, 512, 128, 64, 128
MLA_DQK = MLA_NOPE + MLA_ROPE
MLA_SCALE = MLA_DQK ** -0.5
RET_HEADS, RET_DK, RET_DV, RET_CHUNK = 8, 256, 256, 128
N_EXPERTS = 8

EV_GV, EV_GR, EV_GQ, EV_GK, EV_MQ, EV_KR, EV_MKV, EV_MAIN = 0, 1024, 2048, 2560, 3072, 3840, 4096, 4608
EV_SMALL = 128

V7X_VMEM_BYTES = 64 << 20
VMEM_HEADROOM_BYTES = 6 << 20
FFN_ROWS = 512
ROUTER_ROWS = 512


def _vmem(nbytes):
    return int(min(V7X_VMEM_BYTES - VMEM_HEADROOM_BYTES, max(32 << 20, nbytes + (8 << 20))))


def _tile(n, pref, align=128):
    if n <= pref:
        return n
    t = (pref // align) * align
    while t >= align:
        if n % t == 0:
            return t
        t -= align
    return n


def _split_bf16(a):
    hi = a.astype(BF16)
    lo = (a - hi.astype(F32)).astype(BF16)
    return hi, lo


def _dot(a, b):
    return jnp.dot(a, b, preferred_element_type=F32)


def _dot3(a, b):
    ah, al = _split_bf16(a)
    bh, bl = _split_bf16(b)
    return _dot(ah, bh) + _dot(al, bh) + _dot(ah, bl)


def _dot_nt(a, b):
    return lax.dot_general(a, b, (((1,), (1,)), ((), ())), preferred_element_type=F32)


def _dot_tn(a, b):
    return lax.dot_general(a, b, (((0,), (0,)), ((), ())), preferred_element_type=F32)


def _silu(x):
    return x * (1.0 / (1.0 + jnp.exp(-x)))


def _log_sigmoid(z):
    return -(jnp.maximum(-z, 0.0) + jnp.log(1.0 + jnp.exp(-jnp.abs(z))))


def _rms(x, g):
    return x * lax.rsqrt(jnp.mean(x * x, axis=-1, keepdims=True) + EPS) * g


def _ada_kernel(c_ref, w_ref, b_ref, o_ref):
    o_ref[...] = _dot3(_silu(c_ref[...]), w_ref[...]) + b_ref[...]


def _ada(c_rows, w, b):
    d, n = w.shape
    tn = _tile(n, 1536)
    return pl.pallas_call(
        _ada_kernel,
        grid=(n // tn,),
        in_specs=[pl.BlockSpec((8, d), lambda j: (0, 0)),
                  pl.BlockSpec((d, tn), lambda j: (0, j)),
                  pl.BlockSpec((1, tn), lambda j: (0, j))],
        out_specs=pl.BlockSpec((8, tn), lambda j: (0, j)),
        out_shape=jax.ShapeDtypeStruct((8, n), F32),
        compiler_params=pltpu.CompilerParams(dimension_semantics=("parallel",),
                                             vmem_limit_bytes=_vmem(2 * d * tn * 4 * 3)),
        name="ada_mod",
    )(c_rows, w, b.reshape(1, n))


def _nmm_kernel(*refs, has_small, rope_tiles, heads_per_tile):
    x_ref, g_ref, sc_ref, sh_ref, w_ref = refs[:5]
    k = 5
    if has_small:
        w2_ref = refs[k]; k += 1
    if rope_tiles:
        cs_ref, sn_ref = refs[k], refs[k + 1]; k += 2
    o_ref = refs[k]; k += 1
    if has_small:
        o2_ref = refs[k]; k += 1
    hn_ref = refs[k]
    j = pl.program_id(1)

    @pl.when(j == 0)
    def _():
        y = _rms(x_ref[...], g_ref[...]) * sc_ref[0] + sh_ref[0]
        hn_ref[...] = y.astype(BF16)
        if has_small:
            o2_ref[...] = _dot3(y, w2_ref[...])

    acc = _dot(hn_ref[...], w_ref[...])
    if rope_tiles:
        @pl.when(j < rope_tiles)
        def _():
            cs = cs_ref[...]
            sn = sn_ref[...]
            half = cs.shape[1] // 2
            outs = []
            for h in range(heads_per_tile):
                r = acc[:, h * 2 * half:(h + 1) * 2 * half]
                rs = jnp.concatenate([r[:, half:], r[:, :half]], axis=1)
                outs.append(r * cs + rs * sn)
            o_ref[...] = jnp.concatenate(outs, axis=1).astype(o_ref.dtype)

        @pl.when(j >= rope_tiles)
        def _():
            o_ref[...] = acc.astype(o_ref.dtype)
    else:
        o_ref[...] = acc.astype(o_ref.dtype)


def _nmm(x, g, sc, sh, w, *, rows_per_group, tn, w_small=None, rope=None, rope_cols=0):
    m, d = x.shape
    n = w.shape[1]
    tm = _tile(rows_per_group, 1024)
    gpt = rows_per_group // tm
    grp = lambda i, j: (i // gpt, 0, 0)
    in_specs = [pl.BlockSpec((tm, d), lambda i, j: (i, 0)),
                pl.BlockSpec((1, d), lambda i, j: (0, 0)),
                pl.BlockSpec((1, 1, d), grp),
                pl.BlockSpec((1, 1, d), grp),
                pl.BlockSpec((d, tn), lambda i, j: (0, j))]
    args = [x, g.reshape(1, d), sc, sh, w]
    out_specs = [pl.BlockSpec((tm, tn), lambda i, j: (i, j))]
    out_shape = [jax.ShapeDtypeStruct((m, n), BF16)]
    if w_small is not None:
        in_specs.append(pl.BlockSpec((d, EV_SMALL), lambda i, j: (0, 0)))
        args.append(w_small)
        out_specs.append(pl.BlockSpec((tm, EV_SMALL), lambda i, j: (i, 0)))
        out_shape.append(jax.ShapeDtypeStruct((m, EV_SMALL), F32))
    rope_tiles = heads_per_tile = 0
    if rope is not None:
        cs, sn = rope
        t_rows, hd = cs.shape
        rope_tiles, heads_per_tile = rope_cols // tn, tn // hd
        tpb = t_rows // tm
        in_specs += [pl.BlockSpec((tm, hd), lambda i, j: (i % tpb, 0)),
                     pl.BlockSpec((tm, hd), lambda i, j: (i % tpb, 0))]
        args += [cs, sn]
    kern = functools.partial(_nmm_kernel, has_small=w_small is not None, rope_tiles=rope_tiles,
                             heads_per_tile=heads_per_tile)
    res = pl.pallas_call(
        kern,
        grid=(m // tm, n // tn),
        in_specs=in_specs,
        out_specs=out_specs,
        out_shape=out_shape,
        scratch_shapes=[pltpu.VMEM((tm, d), BF16)],
        compiler_params=pltpu.CompilerParams(
            dimension_semantics=("parallel", "arbitrary"),
            vmem_limit_bytes=_vmem(2 * tm * d * 4 + tm * d * 2 + 2 * d * tn * 2 + 5 * tm * tn * 4)),
        name="norm_proj",
    )(*args)
    return res if w_small is not None else res[0]


def _gla_kernel(mf_q, mf_k, mf_v, sm_f, mb_q, mb_k, mb_v, sm_b, wg_ref, bg_ref, s0f_ref, s0b_ref,
                of_ref, ob_ref, sf_ref, sb_ref, st_ref, *, n_sub):
    s = pl.program_id(1)
    c = GLA_CHUNK

    @pl.when(s == 0)
    def _():
        st_ref[0] = s0f_ref[0]
        st_ref[1] = s0b_ref[0]

    row = lax.broadcasted_iota(jnp.int32, (c, c), 0)
    col = lax.broadcasted_iota(jnp.int32, (c, c), 1)
    ones_col = jnp.ones((c, GLA_DK), BF16)
    for d, (q_ref, k_ref, v_ref, sm_ref, o_ref) in enumerate(
            ((mf_q, mf_k, mf_v, sm_f, of_ref), (mb_q, mb_k, mb_v, sm_b, ob_ref))):
        keep = (col <= row) if d == 0 else (col >= row)
        tri = jnp.where(keep, 1.0, 0.0).astype(BF16)
        z = _dot3(sm_ref[...], wg_ref[d]) + bg_ref[d]
        la_all = _log_sigmoid(z) * (1.0 / GLA_GATE_NORM)
        order = range(n_sub) if d == 0 else range(n_sub - 1, -1, -1)
        for ci in order:
            r0 = ci * c
            for h in range(GLA_HEADS):
                la = la_all[r0:r0 + c, h * GLA_DK:(h + 1) * GLA_DK]
                lh, ll = _split_bf16(la)
                b = _dot(tri, lh) + _dot(tri, ll)
                tot = _dot_tn(lh, ones_col) + _dot_tn(ll, ones_col)
                b_last = b[c - 1:c, :] if d == 0 else b[0:1, :]
                q = q_ref[r0:r0 + c, h * GLA_DK:(h + 1) * GLA_DK].astype(F32) * (GLA_DK ** -0.5)
                k = k_ref[r0:r0 + c, h * GLA_DK:(h + 1) * GLA_DK].astype(F32)
                v = v_ref[r0:r0 + c, h * GLA_DV:(h + 1) * GLA_DV]
                qe = (q * jnp.exp(b)).astype(BF16)
                kd = (k * jnp.exp(-b)).astype(BF16)
                kl = (k * jnp.exp(b_last - b)).astype(BF16)
                att = jnp.where(keep, _dot_nt(qe, kd), 0.0).astype(BF16)
                st = st_ref[d, h]
                o = _dot(att, v) + _dot(qe, st.astype(BF16))
                o_ref[r0:r0 + c, h * GLA_DV:(h + 1) * GLA_DV] = o.astype(o_ref.dtype)
                dec = jnp.exp(tot)
                st_ref[d, h] = jnp.concatenate([dec, dec], axis=1) * st + _dot_tn(kl, v)

    @pl.when(s == pl.num_programs(1) - 1)
    def _():
        sf_ref[0] = st_ref[0]
        sb_ref[0] = st_ref[1]


def _gla(main, small, wg, bg, s0f, s0b, *, batch, t):
    rows = _tile(t, 256, GLA_CHUNK)
    ns = t // rows
    hv, hk = GLA_HEADS * GLA_DV, GLA_HEADS * GLA_DK
    fwd = lambda blk: (lambda b, s: (b * ns + s, blk))
    bwd = lambda blk: (lambda b, s: (b * ns + ns - 1 - s, blk))
    st_spec = pl.BlockSpec((1, GLA_HEADS, GLA_DK, GLA_DV), lambda b, s: (b, 0, 0, 0))
    st_shape = jax.ShapeDtypeStruct((batch, GLA_HEADS, GLA_DK, GLA_DV), F32)

    def io(m):
        return [pl.BlockSpec((rows, hk), m(EV_GQ // hk)), pl.BlockSpec((rows, hk), m(EV_GK // hk)),
                pl.BlockSpec((rows, hv), m(EV_GV // hv)), pl.BlockSpec((rows, EV_SMALL), m(0))]

    return pl.pallas_call(
        functools.partial(_gla_kernel, n_sub=rows // GLA_CHUNK),
        grid=(batch, ns),
        in_specs=io(fwd) + io(bwd) + [
            pl.BlockSpec((2, EV_SMALL, hk), lambda b, s: (0, 0, 0)),
            pl.BlockSpec((2, 1, hk), lambda b, s: (0, 0, 0)),
            st_spec, st_spec],
        out_specs=[pl.BlockSpec((rows, hv), fwd(0)), pl.BlockSpec((rows, hv), bwd(0)), st_spec, st_spec],
        out_shape=[jax.ShapeDtypeStruct((batch * t, hv), BF16), jax.ShapeDtypeStruct((batch * t, hv), BF16),
                   st_shape, st_shape],
        scratch_shapes=[pltpu.VMEM((2, GLA_HEADS, GLA_DK, GLA_DV), F32)],
        compiler_params=pltpu.CompilerParams(dimension_semantics=("parallel", "arbitrary"),
                                             vmem_limit_bytes=_vmem(16 << 20)),
        name="gla_scan",
    )(main, main, main, small, main, main, main, small, wg, bg, s0f, s0b)


def _mla_q_kernel(x_ref, g_ref, w_ref, cs_ref, sn_ref, o_ref, hn_ref):
    @pl.when(pl.program_id(1) == 0)
    def _():
        hn_ref[...] = _rms(x_ref[...].astype(F32), g_ref[...]).astype(BF16)

    r = _dot(hn_ref[...], w_ref[0])
    qr = r[:, MLA_NOPE:MLA_NOPE + MLA_ROPE] * cs_ref[...] + r[:, MLA_NOPE + MLA_ROPE:] * sn_ref[...]
    o_ref[0, 0] = (jnp.concatenate([r[:, :MLA_NOPE], qr], axis=1) * MLA_SCALE).astype(o_ref.dtype)


def _mla_q(main, g, w, cs, sn, *, t):
    m = main.shape[0]
    tm = _tile(t, 1024)
    tpb = t // tm
    return pl.pallas_call(
        _mla_q_kernel,
        grid=(m // tm, MLA_HEADS),
        in_specs=[pl.BlockSpec((tm, MLA_Q_RANK), lambda i, h: (i, EV_MQ // MLA_Q_RANK)),
                  pl.BlockSpec((1, MLA_Q_RANK), lambda i, h: (0, 0)),
                  pl.BlockSpec((1, MLA_Q_RANK, 2 * MLA_NOPE), lambda i, h: (h, 0, 0)),
                  pl.BlockSpec((tm, MLA_ROPE), lambda i, h: (i % tpb, 0)),
                  pl.BlockSpec((tm, MLA_ROPE), lambda i, h: (i % tpb, 0))],
        out_specs=pl.BlockSpec((1, 1, tm, MLA_DQK), lambda i, h: (i // tpb, h, i % tpb, 0)),
        out_shape=jax.ShapeDtypeStruct((m // t, MLA_HEADS, t, MLA_DQK), BF16),
        scratch_shapes=[pltpu.VMEM((tm, MLA_Q_RANK), BF16)],
        compiler_params=pltpu.CompilerParams(dimension_semantics=("parallel", "arbitrary")),
        name="mla_q_proj",
    )(main, g.reshape(1, -1), w, cs, sn)


def _mla_kv_kernel(x_ref, kr_ref, g_ref, w_ref, cs_ref, sn_ref, k_ref, v_ref, hn_ref):
    @pl.when(pl.program_id(1) == 0)
    def _():
        hn_ref[...] = _rms(x_ref[...].astype(F32), g_ref[...]).astype(BF16)

    r = _dot(hn_ref[...], w_ref[0])
    kr = kr_ref[...].astype(F32)
    krr = kr[:, :MLA_ROPE] * cs_ref[...] + kr[:, MLA_ROPE:2 * MLA_ROPE] * sn_ref[...]
    k_ref[0, 0] = jnp.concatenate([r[:, :MLA_NOPE], krr], axis=1).astype(k_ref.dtype)
    v_ref[0, 0] = r[:, MLA_NOPE:].astype(v_ref.dtype)


def _mla_kv(main, g, w, cs, sn, *, t):
    m = main.shape[0]
    tm = _tile(t, 1024)
    tpb = t // tm
    return pl.pallas_call(
        _mla_kv_kernel,
        grid=(m // tm, MLA_HEADS),
        in_specs=[pl.BlockSpec((tm, MLA_KV_RANK), lambda i, h: (i, EV_MKV // MLA_KV_RANK)),
                  pl.BlockSpec((tm, 256), lambda i, h: (i, EV_KR // 256)),
                  pl.BlockSpec((1, MLA_KV_RANK), lambda i, h: (0, 0)),
                  pl.BlockSpec((1, MLA_KV_RANK, MLA_NOPE + MLA_DV), lambda i, h: (h, 0, 0)),
                  pl.BlockSpec((tm, MLA_ROPE), lambda i, h: (i % tpb, 0)),
                  pl.BlockSpec((tm, MLA_ROPE), lambda i, h: (i % tpb, 0))],
        out_specs=[pl.BlockSpec((1, 1, tm, MLA_DQK), lambda i, h: (i // tpb, h, i % tpb, 0)),
                   pl.BlockSpec((1, 1, tm, MLA_DV), lambda i, h: (i // tpb, h, i % tpb, 0))],
        out_shape=[jax.ShapeDtypeStruct((m // t, MLA_HEADS, t, MLA_DQK), BF16),
                   jax.ShapeDtypeStruct((m // t, MLA_HEADS, t, MLA_DV), BF16)],
        scratch_shapes=[pltpu.VMEM((tm, MLA_KV_RANK), BF16)],
        compiler_params=pltpu.CompilerParams(dimension_semantics=("parallel", "arbitrary")),
        name="mla_kv_proj",
    )(main, main, g.reshape(1, -1), w, cs, sn)


def _attn_kernel(q_ref, k_ref, v_ref, o_ref, m_ref, l_ref, acc_ref):
    kv = pl.program_id(3)

    @pl.when(kv == 0)
    def _():
        m_ref[...] = jnp.full(m_ref.shape, -jnp.inf, F32)
        l_ref[...] = jnp.zeros(l_ref.shape, F32)
        acc_ref[...] = jnp.zeros(acc_ref.shape, F32)

    s = _dot_nt(q_ref[0, 0], k_ref[0, 0])
    m_old = m_ref[...]
    m_new = jnp.maximum(m_old, jnp.max(s, axis=-1, keepdims=True))
    alpha = jnp.exp(m_old - m_new)
    p = jnp.exp(s - m_new)
    l_ref[...] = alpha * l_ref[...] + jnp.sum(p, axis=-1, keepdims=True)
    acc_ref[...] = alpha * acc_ref[...] + _dot(p.astype(BF16), v_ref[0, 0])
    m_ref[...] = m_new

    @pl.when(kv == pl.num_programs(3) - 1)
    def _():
        o_ref[...] = (acc_ref[...] / l_ref[...]).astype(o_ref.dtype)


def _attn(q, k, v, *, tq_pref=1024, tk_pref=1280):
    b, h, tq_all, dqk = q.shape
    tk_all = k.shape[2]
    tq, tk = _tile(tq_all, tq_pref), _tile(tk_all, tk_pref)
    nq = tq_all // tq
    return pl.pallas_call(
        _attn_kernel,
        grid=(b, h, nq, tk_all // tk),
        in_specs=[pl.BlockSpec((1, 1, tq, dqk), lambda bi, hi, i, j: (bi, hi, i, 0)),
                  pl.BlockSpec((1, 1, tk, dqk), lambda bi, hi, i, j: (bi, hi, j, 0)),
                  pl.BlockSpec((1, 1, tk, MLA_DV), lambda bi, hi, i, j: (bi, hi, j, 0))],
        out_specs=pl.BlockSpec((tq, MLA_DV), lambda bi, hi, i, j: (bi * nq + i, hi)),
        out_shape=jax.ShapeDtypeStruct((b * tq_all, h * MLA_DV), BF16),
        scratch_shapes=[pltpu.VMEM((tq, 1), F32), pltpu.VMEM((tq, 1), F32), pltpu.VMEM((tq, MLA_DV), F32)],
        compiler_params=pltpu.CompilerParams(
            dimension_semantics=("parallel", "parallel", "parallel", "arbitrary"),
            vmem_limit_bytes=_vmem(4 * tq * tk * 4)),
        name="mla_attention",
    )(q, k, v)


def _mixout_kernel(*refs, n_heads, dv, center, has_mla):
    of_ref, ob_ref, gate_ref, g_ref = refs[:4]
    k = 4
    if has_mla:
        mla_ref = refs[k]; k += 1
    w_ref, x_ref, ga_ref, o_ref = refs[k:k + 4]
    o = of_ref[...].astype(F32) + ob_ref[...].astype(F32)
    g = g_ref[...]
    parts = []
    for h in range(n_heads):
        oh = o[:, h * dv:(h + 1) * dv]
        if center:
            oh = oh - jnp.mean(oh, axis=-1, keepdims=True)
        parts.append(oh * lax.rsqrt(jnp.mean(oh * oh, axis=-1, keepdims=True) + EPS) * g)
    lat = jnp.concatenate(parts, axis=1) * _silu(gate_ref[...].astype(F32))
    lhs = lat.astype(BF16)
    if has_mla:
        lhs = jnp.concatenate([lhs, mla_ref[...]], axis=1)
    o_ref[...] = x_ref[...] + ga_ref[0] * _dot(lhs, w_ref[...])


def _mixout(o_f, o_b, main, gate_blk, g, mla, w, x, ga, *, rows_per_group, n_heads, dv, center):
    m, d = x.shape
    hw = n_heads * dv
    tm = _tile(rows_per_group, 512)
    gpt = rows_per_group // tm
    in_specs = [pl.BlockSpec((tm, hw), lambda i: (i, 0)), pl.BlockSpec((tm, hw), lambda i: (i, 0)),
                pl.BlockSpec((tm, hw), lambda i: (i, gate_blk)), pl.BlockSpec((1, dv), lambda i: (0, 0))]
    args = [o_f, o_b, main, g.reshape(1, dv)]
    if mla is not None:
        in_specs.append(pl.BlockSpec((tm, mla.shape[1]), lambda i: (i, 0)))
        args.append(mla)
    in_specs += [pl.BlockSpec(w.shape, lambda i: (0, 0)), pl.BlockSpec((tm, d), lambda i: (i, 0)),
                 pl.BlockSpec((1, 1, d), lambda i: (i // gpt, 0, 0))]
    args += [w, x, ga]
    return pl.pallas_call(
        functools.partial(_mixout_kernel, n_heads=n_heads, dv=dv, center=center, has_mla=mla is not None),
        grid=(m // tm,),
        in_specs=in_specs,
        out_specs=pl.BlockSpec((tm, d), lambda i: (i, 0)),
        out_shape=jax.ShapeDtypeStruct((m, d), F32),
        compiler_params=pltpu.CompilerParams(
            dimension_semantics=("parallel",),
            vmem_limit_bytes=_vmem(2 * w.size * 2 + 4 * tm * d * 4 + 8 * tm * hw * 2 + 6 * tm * hw * 4)),
        name="mixer_out",
    )(*args)


def _ffn_kernel(te_ref, nv_ref, *refs, prenorm):
    if prenorm:
        x_ref, g_ref, sc_ref, sh_ref, wg_ref, wu_ref, wd_ref, gf_ref, o_ref, hn_ref, acc_ref = refs
    else:
        x_ref, wg_ref, wu_ref, wd_ref, o_ref, hn_ref, acc_ref = refs
    i, j = pl.program_id(0), pl.program_id(1)
    valid = i < nv_ref[0]

    @pl.when(jnp.logical_and(valid, j == 0))
    def _():
        if prenorm:
            y = _rms(x_ref[...], g_ref[...]) * sc_ref[0] + sh_ref[0]
        else:
            y = x_ref[...]
        hn_ref[...] = y.astype(BF16)
        acc_ref[...] = jnp.zeros(acc_ref.shape, F32)

    @pl.when(valid)
    def _():
        hn = hn_ref[...]
        a = (_silu(_dot(hn, wg_ref[0])) * _dot(hn, wu_ref[0])).astype(BF16)
        acc_ref[...] += _dot(a, wd_ref[0])

    last = j == pl.num_programs(1) - 1

    @pl.when(jnp.logical_and(valid, last))
    def _():
        if prenorm:
            o_ref[...] = x_ref[...] + gf_ref[0] * acc_ref[...]
        else:
            o_ref[...] = acc_ref[...]

    @pl.when(jnp.logical_and(jnp.logical_not(valid), last))
    def _():
        o_ref[...] = jnp.zeros(o_ref.shape, o_ref.dtype)


def _ffn(x, wg, wu, wd, tile_expert, n_valid, *, tm, norm=None):
    p, d = x.shape
    f = wg.shape[2]
    tf = _tile(f, 512)
    nt = p // tm
    wmap_in = lambda i, j, te, nv: (te[i], 0, j)
    wmap_out = lambda i, j, te, nv: (te[i], j, 0)
    row = lambda i, j, te, nv: (i, 0)
    in_specs = [pl.BlockSpec((tm, d), row)]
    args = [x]
    if norm is not None:
        g, sc, sh, gf, rpg = norm
        gpt = rpg // tm
        grp = lambda i, j, te, nv: (i // gpt, 0, 0)
        in_specs += [pl.BlockSpec((1, d), lambda i, j, te, nv: (0, 0)), pl.BlockSpec((1, 1, d), grp),
                     pl.BlockSpec((1, 1, d), grp)]
        args += [g.reshape(1, d), sc, sh]
    in_specs += [pl.BlockSpec((1, d, tf), wmap_in), pl.BlockSpec((1, d, tf), wmap_in),
                 pl.BlockSpec((1, tf, d), wmap_out)]
    args += [wg, wu, wd]
    if norm is not None:
        in_specs.append(pl.BlockSpec((1, 1, d), grp))
        args.append(gf)
    return pl.pallas_call(
        functools.partial(_ffn_kernel, prenorm=norm is not None),
        grid_spec=pltpu.PrefetchScalarGridSpec(
            num_scalar_prefetch=2,
            grid=(nt, f // tf),
            in_specs=in_specs,
            out_specs=pl.BlockSpec((tm, d), row),
            scratch_shapes=[pltpu.VMEM((tm, d), BF16), pltpu.VMEM((tm, d), F32)]),
        out_shape=jax.ShapeDtypeStruct((p, d), F32),
        compiler_params=pltpu.CompilerParams(
            dimension_semantics=("parallel", "arbitrary"),
            vmem_limit_bytes=_vmem(4 * tm * d * 4 + tm * d * 6 + 6 * d * tf * 2 + 5 * tm * tf * 4)),
        name="swiglu_ffn",
    )(tile_expert, n_valid, *args)


def _ret_kernel(qf_ref, kf_ref, vf_ref, qb_ref, kb_ref, vb_ref, dm_ref, qd_ref, kd_ref, cd_ref,
                s0f_ref, s0b_ref, of_ref, ob_ref, sf_ref, sb_ref, st_ref, *, n_sub):
    s = pl.program_id(1)
    c = RET_CHUNK

    @pl.when(s == 0)
    def _():
        st_ref[0] = s0f_ref[0]
        st_ref[1] = s0b_ref[0]

    for d, (q_ref, k_ref, v_ref, o_ref) in enumerate(((qf_ref, kf_ref, vf_ref, of_ref),
                                                       (qb_ref, kb_ref, vb_ref, ob_ref))):
        order = range(n_sub) if d == 0 else range(n_sub - 1, -1, -1)
        for ci in order:
            r0 = ci * c
            for h in range(RET_HEADS):
                q = q_ref[r0:r0 + c, h * RET_DK:(h + 1) * RET_DK]
                k = (k_ref[r0:r0 + c, h * RET_DK:(h + 1) * RET_DK].astype(F32) * (RET_DK ** -0.5)).astype(BF16)
                v = v_ref[r0:r0 + c, h * RET_DV:(h + 1) * RET_DV]
                att = (_dot_nt(q, k) * dm_ref[d, h]).astype(BF16)
                st = st_ref[d, h]
                o = _dot(att, v) + qd_ref[d, h] * _dot(q, st.astype(BF16))
                o_ref[r0:r0 + c, h * RET_DV:(h + 1) * RET_DV] = o.astype(o_ref.dtype)
                vk = (v.astype(F32) * kd_ref[d, h]).astype(BF16)
                st_ref[d, h] = cd_ref[d, h] * st + _dot_tn(k, vk)

    @pl.when(s == pl.num_programs(1) - 1)
    def _():
        sf_ref[0] = st_ref[0]
        sb_ref[0] = st_ref[1]


def _ret(main, tables, s0f, s0b, *, batch, t):
    rows = _tile(t, 256, RET_CHUNK)
    ns = t // rows
    hk, hv = RET_HEADS * RET_DK, RET_HEADS * RET_DV
    fwd = lambda blk: (lambda b, s: (b * ns + s, blk))
    bwd = lambda blk: (lambda b, s: (b * ns + ns - 1 - s, blk))
    st_spec = pl.BlockSpec((1, RET_HEADS, RET_DK, RET_DV), lambda b, s: (b, 0, 0, 0))
    st_shape = jax.ShapeDtypeStruct((batch, RET_HEADS, RET_DK, RET_DV), F32)
    io = lambda m: [pl.BlockSpec((rows, hk), m(0)), pl.BlockSpec((rows, hk), m(1)), pl.BlockSpec((rows, hv), m(2))]
    full = lambda a: pl.BlockSpec(a.shape, lambda b, s: (0,) * a.ndim)
    return pl.pallas_call(
        functools.partial(_ret_kernel, n_sub=rows // RET_CHUNK),
        grid=(batch, ns),
        in_specs=io(fwd) + io(bwd) + [full(a) for a in tables] + [st_spec, st_spec],
        out_specs=[pl.BlockSpec((rows, hv), fwd(0)), pl.BlockSpec((rows, hv), bwd(0)), st_spec, st_spec],
        out_shape=[jax.ShapeDtypeStruct((batch * t, hv), BF16), jax.ShapeDtypeStruct((batch * t, hv), BF16),
                   st_shape, st_shape],
        scratch_shapes=[pltpu.VMEM((2, RET_HEADS, RET_DK, RET_DV), F32)],
        compiler_params=pltpu.CompilerParams(dimension_semantics=("parallel", "arbitrary"),
                                             vmem_limit_bytes=_vmem(40 << 20)),
        name="retention_scan",
    )(main, main, main, main, main, main, *tables, s0f, s0b)


def _router_kernel(x_ref, g_ref, sc_ref, sh_ref, wr_ref, h_ref, meta_ref, gate_ref, cnt_ref, run_ref):
    i = pl.program_id(0)

    @pl.when(i == 0)
    def _():
        run_ref[...] = jnp.zeros(run_ref.shape, F32)

    y = _rms(x_ref[...], g_ref[...]) * sc_ref[0] + sh_ref[0]
    h_ref[...] = y
    tm = y.shape[0]
    lt = jnp.transpose(_dot3(y, wr_ref[...]))[:N_EXPERTS, :]
    eid = lax.broadcasted_iota(jnp.int32, lt.shape, 0).astype(F32)
    v1 = jnp.max(lt, axis=0, keepdims=True)
    e1 = jnp.min(jnp.where(lt == v1, eid, float(N_EXPERTS)), axis=0, keepdims=True)
    lt2 = jnp.where(eid == e1, -jnp.inf, lt)
    v2 = jnp.max(lt2, axis=0, keepdims=True)
    e2 = jnp.min(jnp.where(lt2 == v2, eid, float(N_EXPERTS)), axis=0, keepdims=True)
    ex = jnp.exp(v2 - v1)
    w1 = 1.0 / (1.0 + ex)
    w2 = ex / (1.0 + ex)
    oh1 = jnp.where(eid == e1, 1.0, 0.0)
    oh2 = jnp.where(eid == e2, 1.0, 0.0)
    mem = oh1 + oh2
    r = lax.broadcasted_iota(jnp.int32, (tm, tm), 0)
    cidx = lax.broadcasted_iota(jnp.int32, (tm, tm), 1)
    before = jnp.where(r < cidx, 1.0, 0.0).astype(BF16)
    excl = _dot(mem.astype(BF16), before) + run_ref[...][:, :1]
    rk1 = jnp.sum(oh1 * excl, axis=0, keepdims=True)
    rk2 = jnp.sum(oh2 * excl, axis=0, keepdims=True)
    run_ref[...] = run_ref[...] + jnp.sum(mem, axis=1, keepdims=True)
    zero = jnp.zeros_like(e1)
    meta_ref[...] = jnp.concatenate([e1, e2, rk1, rk2, w1, w2, zero, zero], axis=0)
    wpad = jnp.concatenate([w1, w2, jnp.zeros((6, tm), F32)], axis=0)
    gate_ref[...] = jnp.transpose(jnp.concatenate([wpad] * 16, axis=0))
    cnt_ref[...] = run_ref[...]


def _router(x, g, sc, sh, wr, *, rows_per_group):
    m, d = x.shape
    tm = _tile(rows_per_group, ROUTER_ROWS)
    gpt = rows_per_group // tm
    grp = lambda i: (i // gpt, 0, 0)
    return pl.pallas_call(
        _router_kernel,
        grid=(m // tm,),
        in_specs=[pl.BlockSpec((tm, d), lambda i: (i, 0)), pl.BlockSpec((1, d), lambda i: (0, 0)),
                  pl.BlockSpec((1, 1, d), grp), pl.BlockSpec((1, 1, d), grp),
                  pl.BlockSpec((d, 128), lambda i: (0, 0))],
        out_specs=[pl.BlockSpec((tm, d), lambda i: (i, 0)), pl.BlockSpec((8, tm), lambda i: (0, i)),
                   pl.BlockSpec((tm, 128), lambda i: (i, 0)), pl.BlockSpec((N_EXPERTS, 128), lambda i: (0, 0))],
        out_shape=[jax.ShapeDtypeStruct((m, d), F32), jax.ShapeDtypeStruct((8, m), F32),
                   jax.ShapeDtypeStruct((m, 128), F32), jax.ShapeDtypeStruct((N_EXPERTS, 128), F32)],
        scratch_shapes=[pltpu.VMEM((N_EXPERTS, 128), F32)],
        compiler_params=pltpu.CompilerParams(dimension_semantics=("arbitrary",),
                                             vmem_limit_bytes=_vmem(6 * tm * d * 4)),
        name="moe_router",
    )(x, g.reshape(1, d), sc, sh, wr)


ROW_DMA_WINDOW = 16


def _scatter_kernel(pos_ref, h_ref, init_ref, o_ref, sem):
    del init_ref
    tm = pos_ref.shape[2]
    base = pl.program_id(0) * tm

    def copies(t):
        src = h_ref.at[base + t]
        return (pltpu.make_async_copy(src, o_ref.at[pos_ref[0, 0, t]], sem.at[0]),
                pltpu.make_async_copy(src, o_ref.at[pos_ref[0, 1, t]], sem.at[1]))

    def body(t, carry):
        @pl.when(t >= ROW_DMA_WINDOW)
        def _():
            for cp in copies(t - ROW_DMA_WINDOW):
                cp.wait()
        for cp in copies(t):
            cp.start()
        return carry

    lax.fori_loop(0, tm, body, 0)

    def drain(t, carry):
        for cp in copies(t):
            cp.wait()
        return carry

    lax.fori_loop(tm - ROW_DMA_WINDOW, tm, drain, 0)


def _scatter_rows(h, pos, init):
    nt, _, tm = pos.shape
    return pl.pallas_call(
        _scatter_kernel,
        grid=(nt,),
        in_specs=[pl.BlockSpec((1, 2, tm), lambda i: (i, 0, 0), memory_space=pltpu.SMEM),
                  pl.BlockSpec(memory_space=pl.ANY), pl.BlockSpec(memory_space=pl.ANY)],
        out_specs=pl.BlockSpec(memory_space=pl.ANY),
        out_shape=jax.ShapeDtypeStruct(init.shape, init.dtype),
        scratch_shapes=[pltpu.SemaphoreType.DMA((2,))],
        input_output_aliases={2: 0},
        compiler_params=pltpu.CompilerParams(dimension_semantics=("arbitrary",)),
        name="moe_scatter",
    )(pos, h, init)


def _combine_kernel(pos_ref, y_ref, x_ref, gate_ref, gf_ref, gn_ref, o_ref, buf_ref, sem):
    tm = pos_ref.shape[2]

    def copies(t):
        return (pltpu.make_async_copy(y_ref.at[pos_ref[0, 0, t]], buf_ref.at[0, t], sem.at[0]),
                pltpu.make_async_copy(y_ref.at[pos_ref[0, 1, t]], buf_ref.at[1, t], sem.at[1]))

    def issue(t, carry):
        for cp in copies(t):
            cp.start()
        return carry

    lax.fori_loop(0, tm, issue, 0)

    def drain(t, carry):
        for cp in copies(t):
            cp.wait()
        return carry

    lax.fori_loop(0, tm, drain, 0)
    w = gate_ref[...]
    moe = w[:, 0:1] * buf_ref[0] + w[:, 1:2] * buf_ref[1]
    o_ref[...] = _rms(x_ref[...] + gf_ref[0] * moe, gn_ref[...])


def _combine(y, pos, x, gate, gf, gn, *, rows_per_group):
    m, d = x.shape
    nt, _, tm = pos.shape
    gpt = rows_per_group // tm
    return pl.pallas_call(
        _combine_kernel,
        grid=(nt,),
        in_specs=[pl.BlockSpec((1, 2, tm), lambda i: (i, 0, 0), memory_space=pltpu.SMEM),
                  pl.BlockSpec(memory_space=pl.ANY),
                  pl.BlockSpec((tm, d), lambda i: (i, 0)), pl.BlockSpec((tm, 128), lambda i: (i, 0)),
                  pl.BlockSpec((1, 1, d), lambda i: (i // gpt, 0, 0)), pl.BlockSpec((1, d), lambda i: (0, 0))],
        out_specs=pl.BlockSpec((tm, d), lambda i: (i, 0)),
        out_shape=jax.ShapeDtypeStruct((m, d), F32),
        scratch_shapes=[pltpu.VMEM((2, tm, d), F32), pltpu.SemaphoreType.DMA((2,))],
        compiler_params=pltpu.CompilerParams(dimension_semantics=("arbitrary",),
                                             vmem_limit_bytes=_vmem(8 * tm * d * 4)),
        name="moe_combine",
    )(pos, y, x, gate, gf, gn.reshape(1, d))


def _axial_angles(rows, dim):
    row = jnp.repeat(jnp.arange(rows, dtype=F32), GRID_W)
    col = jnp.tile(jnp.arange(GRID_W, dtype=F32), rows)
    half = dim // 2
    inv = 1.0 / (ROPE_BASE ** (jnp.arange(0, half, 2, dtype=F32) / half))
    return jnp.concatenate([row[:, None] * inv, col[:, None] * inv], axis=-1)


def _rope_tables(rows, dim, t_ctx):
    ang = _axial_angles(rows, dim)
    cs = jnp.concatenate([jnp.cos(ang), jnp.cos(ang)], axis=1)
    sn = jnp.concatenate([-jnp.sin(ang), jnp.sin(ang)], axis=1)
    return (cs, sn), (jnp.ones((t_ctx, dim), F32), jnp.zeros((t_ctx, dim), F32))


def _deinterleave(n):
    return np.concatenate([np.arange(0, n, 2), np.arange(1, n, 2)])


def _even_weights(w_in, w_g2, b_g2, w_uq, w_ukv):
    d = w_in.shape[0]
    sizes = (512, 512, 1024, 1024, 16, 16, MLA_Q_RANK, MLA_KV_RANK, MLA_ROPE)
    offs = np.cumsum((0,) + sizes)
    gq, gk, gv, gr, gaf, gab, mq, mkv, mkr = [w_in[:, offs[i]:offs[i + 1]] for i in range(9)]
    perm = _deinterleave(MLA_ROPE)
    swap = np.concatenate([perm[MLA_ROPE // 2:], perm[:MLA_ROPE // 2]])
    w_main = jnp.concatenate([gv, gr, gq, gk, mq, mkr[:, perm], mkr[:, swap], jnp.zeros((d, 128), F32), mkv],
                             axis=1).astype(BF16)
    w_small = jnp.concatenate([gaf, gab, jnp.zeros((d, EV_SMALL - 2 * GLA_RANK), F32)], axis=1)
    wg = jnp.zeros((2, EV_SMALL, GLA_HEADS * GLA_DK), F32)
    wg = wg.at[0, :GLA_RANK].set(w_g2[0]).at[1, GLA_RANK:2 * GLA_RANK].set(w_g2[1])
    bg = b_g2.reshape(2, 1, -1)
    wq = w_uq.reshape(MLA_Q_RANK, MLA_HEADS, MLA_DQK)
    wq = jnp.concatenate([wq[:, :, :MLA_NOPE], wq[:, :, MLA_NOPE + perm], wq[:, :, MLA_NOPE + swap]], axis=2)
    wq = jnp.transpose(wq, (1, 0, 2)).astype(BF16)
    wkv = jnp.transpose(w_ukv.reshape(MLA_KV_RANK, MLA_HEADS, MLA_NOPE + MLA_DV), (1, 0, 2)).astype(BF16)
    return w_main, w_small, wg, bg, wq, wkv


def _odd_weights(w_in):
    d = w_in.shape[0]
    hk = RET_HEADS * RET_DK
    perm = _deinterleave(RET_DK)
    cols = (np.arange(RET_HEADS)[:, None] * RET_DK + perm[None, :]).reshape(-1)
    return jnp.concatenate([w_in[:, cols], w_in[:, hk + cols], w_in[:, 2 * hk:]], axis=1).astype(BF16)


def _ret_tables(log_decay):
    lg = -jnp.exp(log_decay.astype(F32))
    c = RET_CHUNK
    pos = jnp.arange(c, dtype=F32)
    diff = pos[:, None] - pos[None, :]
    lgd = lg[:, :, None, None]
    fmask = jnp.where(diff >= 0, jnp.exp(lgd * jnp.maximum(diff, 0.0)), 0.0)
    bmask = jnp.where(diff <= 0, jnp.exp(lgd * jnp.maximum(-diff, 0.0)), 0.0)
    dm = jnp.stack([fmask[0], bmask[1]])
    lgc = lg[:, :, None, None]
    qd = jnp.stack([jnp.exp(lgc[0] * (pos + 1.0)[None, :, None]), jnp.exp(lgc[1] * (c - pos)[None, :, None])])
    kd = jnp.stack([jnp.exp(lgc[0] * (c - 1.0 - pos)[None, :, None]), jnp.exp(lgc[1] * pos[None, :, None])])
    cd = jnp.exp(lgc * c)
    return dm, qd, kd, cd


def _dispatch_plan(meta, counts, tm, n_tiles):
    cnt = counts[:, 0].astype(jnp.int32)
    tiles = (cnt + tm - 1) // tm
    tile_end = jnp.cumsum(tiles)
    start = (tile_end - tiles) * tm
    e = meta[0:2].astype(jnp.int32)
    pos = start[e] + meta[2:4].astype(jnp.int32)
    tile_ids = jnp.arange(n_tiles, dtype=jnp.int32)
    te = jnp.minimum(jnp.searchsorted(tile_end, tile_ids, side="right"), N_EXPERTS - 1).astype(jnp.int32)
    n_valid = tile_end[-1:].astype(jnp.int32)
    te = jnp.where(tile_ids < n_valid[0], te, te[jnp.maximum(n_valid[0] - 1, 0)])
    return pos, te, n_valid


def kernel(x, c, ctx, c_ctx, ada_w, ada_b, norm_mix, norm_ffn, norm_final, ev_w_in, gla_w_gate2, gla_b_gate2, gla_norm, mla_q_norm, mla_w_uq, mla_kv_norm, mla_w_ukv, ev_w_out, ffn_w_gate, ffn_w_up, ffn_w_down, od_w_in, ret_log_decay, ret_norm, od_w_out, moe_router, moe_w_gate, moe_w_up, moe_w_down):
    B, T, D = x.shape
    TC = ctx.shape[1]
    depth = ada_w.shape[0]
    assert depth == 2 and T % GRID_W == 0 and B <= 7
    rows = T // GRID_W
    xl = x.reshape(B * T, D)
    xc = ctx.reshape(B * TC, D)

    c_rows = jnp.concatenate([c, c_ctx[None, :], jnp.zeros((8 - B - 1, D), F32)], axis=0)
    mods = []
    for i in range(depth):
        m = _ada(c_rows, ada_w[i], ada_b[i]).reshape(8, 6, 1, D)
        mods.append([(m[:B, k], m[B:B + 1, k]) for k in range(6)])

    (sh_a, sc_a, g_a, sh_f, sc_f, g_f) = mods[0]
    w_main, w_small, wg2, bg2, wq, wkv = _even_weights(ev_w_in[0], gla_w_gate2[0], gla_b_gate2[0],
                                                       mla_w_uq[0], mla_w_ukv[0])
    tn_ev = _tile(EV_MAIN, 768)
    main_l, small_l = _nmm(xl, norm_mix[0], 1.0 + sc_a[0], sh_a[0], w_main, rows_per_group=T, tn=tn_ev,
                           w_small=w_small)
    main_c, small_c = _nmm(xc, norm_mix[0], 1.0 + sc_a[1], sh_a[1], w_main, rows_per_group=B * TC, tn=tn_ev,
                           w_small=w_small)

    s0 = jnp.zeros((B, GLA_HEADS, GLA_DK, GLA_DV), F32)
    ocf, ocb, s_f, s_b = _gla(main_c, small_c, wg2, bg2, s0, s0, batch=B, t=TC)
    olf, olb, _, _ = _gla(main_l, small_l, wg2, bg2, s_f, s_b, batch=B, t=T)

    (cs_m, sn_m), (cs_1, sn_0) = _rope_tables(rows, MLA_ROPE, TC)
    q_l = _mla_q(main_l, mla_q_norm[0], wq, cs_m, sn_m, t=T)
    q_c = _mla_q(main_c, mla_q_norm[0], wq, cs_1, sn_0, t=TC)
    k_l, v_l = _mla_kv(main_l, mla_kv_norm[0], wkv, cs_m, sn_m, t=T)
    k_c, v_c = _mla_kv(main_c, mla_kv_norm[0], wkv, cs_1, sn_0, t=TC)
    mla_l = _attn(q_l, jnp.concatenate([k_l, k_c], axis=2), jnp.concatenate([v_l, v_c], axis=2))
    mla_c = _attn(q_c, k_c, v_c)

    w_out = ev_w_out[0].astype(BF16)
    gr_blk = EV_GR // (GLA_HEADS * GLA_DV)
    x1l = _mixout(olf, olb, main_l, gr_blk, gla_norm[0], mla_l, w_out, xl, g_a[0], rows_per_group=T,
                  n_heads=GLA_HEADS, dv=GLA_DV, center=False)
    x1c = _mixout(ocf, ocb, main_c, gr_blk, gla_norm[0], mla_c, w_out, xc, g_a[1], rows_per_group=B * TC,
                  n_heads=GLA_HEADS, dv=GLA_DV, center=False)

    fwg, fwu, fwd_ = ffn_w_gate.astype(BF16), ffn_w_up.astype(BF16), ffn_w_down.astype(BF16)
    tm_l = _tile(T, FFN_ROWS)
    x2l = _ffn(x1l, fwg, fwu, fwd_, jnp.zeros((B * T // tm_l,), jnp.int32), jnp.full((1,), B * T // tm_l, jnp.int32),
               tm=tm_l, norm=(norm_ffn[0], 1.0 + sc_f[0], sh_f[0], g_f[0], T))
    tm_c = _tile(B * TC, FFN_ROWS)
    x2c = _ffn(x1c, fwg, fwu, fwd_, jnp.zeros((B * TC // tm_c,), jnp.int32),
               jnp.full((1,), B * TC // tm_c, jnp.int32), tm=tm_c,
               norm=(norm_ffn[0], 1.0 + sc_f[1], sh_f[1], g_f[1], B * TC))

    (sh_a, sc_a, g_a, sh_f, sc_f, g_f) = mods[1]
    w_odd = _odd_weights(od_w_in[0])
    hk = RET_HEADS * RET_DK
    tn_od = _tile(4 * hk, 1024)
    rope_l, rope_c = _rope_tables(rows, RET_DK, B * TC)
    m2l = _nmm(x2l, norm_mix[1], 1.0 + sc_a[0], sh_a[0], w_odd, rows_per_group=T, tn=tn_od, rope=rope_l,
               rope_cols=2 * hk)
    m2c = _nmm(x2c, norm_mix[1], 1.0 + sc_a[1], sh_a[1], w_odd, rows_per_group=B * TC, tn=tn_od, rope=rope_c,
               rope_cols=2 * hk)
    tables = _ret_tables(ret_log_decay[0])
    s0 = jnp.zeros((B, RET_HEADS, RET_DK, RET_DV), F32)
    _, _, s_f, s_b = _ret(m2c, tables, s0, s0, batch=B, t=TC)
    orf, orb, _, _ = _ret(m2l, tables, s_f, s_b, batch=B, t=T)
    x3 = _mixout(orf, orb, m2l, 3, ret_norm[0], None, od_w_out[0].astype(BF16), x2l, g_a[0], rows_per_group=T,
                 n_heads=RET_HEADS, dv=RET_DV, center=True)

    wr = jnp.concatenate([moe_router[0], jnp.zeros((D, 128 - N_EXPERTS), F32)], axis=1)
    h, meta, gate, counts = _router(x3, norm_ffn[1], 1.0 + sc_f[0], sh_f[0], wr, rows_per_group=T)
    tm_e = _tile(T, FFN_ROWS)
    n_tiles = (2 * B * T) // tm_e + N_EXPERTS
    pos, te, n_valid = _dispatch_plan(meta, counts, tm_e, n_tiles)
    tm_r = _tile(T, ROUTER_ROWS)
    pos_t = jnp.transpose(pos.reshape(2, -1, tm_r), (1, 0, 2))
    hs = _scatter_rows(h, pos_t, jnp.zeros((n_tiles * tm_e, D), F32))
    ys = _ffn(hs, moe_w_gate[0].astype(BF16), moe_w_up[0].astype(BF16), moe_w_down[0].astype(BF16), te, n_valid,
              tm=tm_e)
    out = _combine(ys, pos_t, x3, gate, g_f[0], norm_final, rows_per_group=T)
    return out.reshape(B, T, D)
```

```python
import functools

import jax
import jax.numpy as jnp
import numpy as np
from jax import lax
from jax.experimental import pallas as pl
from jax.experimental.pallas import tpu as pltpu

F32 = jnp.float32
BF16 = jnp.bfloat16
EPS = 1e-6
ROPE_BASE = 10000.0
GRID_W = 64

GLA_HEADS, GLA_DK, GLA_DV, GLA_RANK, GLA_GATE_NORM, GLA_CHUNK = 4, 128, 256, 16, 16.0, 64
MLA_HEADS, MLA_Q_RANK, MLA_KV_RANK, MLA_NOPE, MLA_ROPE, MLA_DV = 8, 768, 512, 128, 64, 128
MLA_DQK = MLA_NOPE + MLA_ROPE
MLA_SCALE = MLA_DQK ** -0.5
MLA_Q_SCALE = MLA_SCALE * 1.4426950408889634
RET_HEADS, RET_DK, RET_DV, RET_CHUNK = 8, 256, 256, 128
N_EXPERTS = 8

EV_GV, EV_GR, EV_GQ, EV_GK, EV_MQ, EV_KR, EV_MKV, EV_MAIN = 0, 1024, 2048, 2560, 3072, 3840, 4096, 4608
EV_SMALL = 128

V7X_VMEM_BYTES = 64 << 20
VMEM_HEADROOM_BYTES = 6 << 20
FFN_ROWS = 512
ROUTER_ROWS = 512


def _vmem(nbytes):
    return int(min(V7X_VMEM_BYTES - VMEM_HEADROOM_BYTES, max(32 << 20, nbytes + (8 << 20))))


def _tile(n, pref, align=128):
    if n <= pref:
        return n
    t = (pref // align) * align
    while t >= align:
        if n % t == 0:
            return t
        t -= align
    return n


def _split_bf16(a):
    hi = a.astype(BF16)
    lo = (a - hi.astype(F32)).astype(BF16)
    return hi, lo


def _dot(a, b):
    return jnp.dot(a, b, preferred_element_type=F32)


def _dot3(a, b):
    ah, al = _split_bf16(a)
    bh, bl = _split_bf16(b)
    return _dot(ah, bh) + _dot(al, bh) + _dot(ah, bl)


def _dot_nt(a, b):
    return lax.dot_general(a, b, (((1,), (1,)), ((), ())), preferred_element_type=F32)


def _dot_tn(a, b):
    return lax.dot_general(a, b, (((0,), (0,)), ((), ())), preferred_element_type=F32)


def _silu(x):
    return x * (1.0 / (1.0 + jnp.exp(-x)))


def _log_sigmoid(z):
    return -(jnp.maximum(-z, 0.0) + jnp.log(1.0 + jnp.exp(-jnp.abs(z))))


def _rms(x, g):
    return x * lax.rsqrt(jnp.mean(x * x, axis=-1, keepdims=True) + EPS) * g


def _ada_kernel(c_ref, w_ref, b_ref, o_ref):
    o_ref[...] = _dot3(_silu(c_ref[...]), w_ref[...]) + b_ref[...]


def _ada(c_rows, w, b):
    d, n = w.shape
    tn = _tile(n, 1536)
    return pl.pallas_call(
        _ada_kernel,
        grid=(n // tn,),
        in_specs=[pl.BlockSpec((8, d), lambda j: (0, 0)),
                  pl.BlockSpec((d, tn), lambda j: (0, j)),
                  pl.BlockSpec((1, tn), lambda j: (0, j))],
        out_specs=pl.BlockSpec((8, tn), lambda j: (0, j)),
        out_shape=jax.ShapeDtypeStruct((8, n), F32),
        compiler_params=pltpu.CompilerParams(dimension_semantics=("parallel",),
                                             vmem_limit_bytes=_vmem(2 * d * tn * 4 * 3)),
        name="ada_mod",
    )(c_rows, w, b.reshape(1, n))


def _nmm_kernel(*refs, has_small, rope_tiles, heads_per_tile):
    x_ref, g_ref, sc_ref, sh_ref, w_ref = refs[:5]
    k = 5
    if has_small:
        w2_ref = refs[k]; k += 1
    if rope_tiles:
        cs_ref, sn_ref = refs[k], refs[k + 1]; k += 2
    o_ref = refs[k]; k += 1
    if has_small:
        o2_ref = refs[k]; k += 1
    hn_ref = refs[k]
    j = pl.program_id(1)

    @pl.when(j == 0)
    def _():
        y = _rms(x_ref[...], g_ref[...]) * sc_ref[0] + sh_ref[0]
        hn_ref[...] = y.astype(BF16)
        if has_small:
            o2_ref[...] = _dot3(y, w2_ref[...])

    acc = _dot(hn_ref[...], w_ref[...])
    if rope_tiles:
        @pl.when(j < rope_tiles)
        def _():
            cs = cs_ref[...]
            sn = sn_ref[...]
            half = cs.shape[1] // 2
            outs = []
            for h in range(heads_per_tile):
                r = acc[:, h * 2 * half:(h + 1) * 2 * half]
                rs = jnp.concatenate([r[:, half:], r[:, :half]], axis=1)
                outs.append(r * cs + rs * sn)
            o_ref[...] = jnp.concatenate(outs, axis=1).astype(o_ref.dtype)

        @pl.when(j >= rope_tiles)
        def _():
            o_ref[...] = acc.astype(o_ref.dtype)
    else:
        o_ref[...] = acc.astype(o_ref.dtype)


def _nmm(x, g, sc, sh, w, *, rows_per_group, tn, w_small=None, rope=None, rope_cols=0):
    m, d = x.shape
    n = w.shape[1]
    tm = _tile(rows_per_group, 1024)
    gpt = rows_per_group // tm
    grp = lambda i, j: (i // gpt, 0, 0)
    in_specs = [pl.BlockSpec((tm, d), lambda i, j: (i, 0)),
                pl.BlockSpec((1, d), lambda i, j: (0, 0)),
                pl.BlockSpec((1, 1, d), grp),
                pl.BlockSpec((1, 1, d), grp),
                pl.BlockSpec((d, tn), lambda i, j: (0, j))]
    args = [x, g.reshape(1, d), sc, sh, w]
    out_specs = [pl.BlockSpec((tm, tn), lambda i, j: (i, j))]
    out_shape = [jax.ShapeDtypeStruct((m, n), BF16)]
    if w_small is not None:
        in_specs.append(pl.BlockSpec((d, EV_SMALL), lambda i, j: (0, 0)))
        args.append(w_small)
        out_specs.append(pl.BlockSpec((tm, EV_SMALL), lambda i, j: (i, 0)))
        out_shape.append(jax.ShapeDtypeStruct((m, EV_SMALL), F32))
    rope_tiles = heads_per_tile = 0
    if rope is not None:
        cs, sn = rope
        t_rows, hd = cs.shape
        rope_tiles, heads_per_tile = rope_cols // tn, tn // hd
        tpb = t_rows // tm
        in_specs += [pl.BlockSpec((tm, hd), lambda i, j: (i % tpb, 0)),
                     pl.BlockSpec((tm, hd), lambda i, j: (i % tpb, 0))]
        args += [cs, sn]
    kern = functools.partial(_nmm_kernel, has_small=w_small is not None, rope_tiles=rope_tiles,
                             heads_per_tile=heads_per_tile)
    res = pl.pallas_call(
        kern,
        grid=(m // tm, n // tn),
        in_specs=in_specs,
        out_specs=out_specs,
        out_shape=out_shape,
        scratch_shapes=[pltpu.VMEM((tm, d), BF16)],
        compiler_params=pltpu.CompilerParams(
            dimension_semantics=("parallel", "arbitrary"),
            vmem_limit_bytes=_vmem(2 * tm * d * 4 + tm * d * 2 + 2 * d * tn * 2 + 5 * tm * tn * 4)),
        name="norm_proj",
    )(*args)
    return res if w_small is not None else res[0]


def _gla_kernel(mf_q, mf_k, mf_v, sm_f, mb_q, mb_k, mb_v, sm_b, wg_ref, bg_ref, s0f_ref, s0b_ref,
                of_ref, ob_ref, sf_ref, sb_ref, st_ref, *, n_sub):
    s = pl.program_id(1)
    c = GLA_CHUNK

    @pl.when(s == 0)
    def _():
        st_ref[0] = s0f_ref[0]
        st_ref[1] = s0b_ref[0]

    row = lax.broadcasted_iota(jnp.int32, (c, c), 0)
    col = lax.broadcasted_iota(jnp.int32, (c, c), 1)
    ones_col = jnp.ones((c, GLA_DK), BF16)
    for d, (q_ref, k_ref, v_ref, sm_ref, o_ref) in enumerate(
            ((mf_q, mf_k, mf_v, sm_f, of_ref), (mb_q, mb_k, mb_v, sm_b, ob_ref))):
        keep = (col <= row) if d == 0 else (col >= row)
        tri = jnp.where(keep, 1.0, 0.0).astype(BF16)
        z = _dot3(sm_ref[...], wg_ref[d]) + bg_ref[d]
        la_all = _log_sigmoid(z) * (1.0 / GLA_GATE_NORM)
        order = range(n_sub) if d == 0 else range(n_sub - 1, -1, -1)
        for ci in order:
            r0 = ci * c
            for h in range(GLA_HEADS):
                la = la_all[r0:r0 + c, h * GLA_DK:(h + 1) * GLA_DK]
                lh, ll = _split_bf16(la)
                b = _dot(tri, lh) + _dot(tri, ll)
                tot = _dot_tn(lh, ones_col) + _dot_tn(ll, ones_col)
                b_last = b[c - 1:c, :] if d == 0 else b[0:1, :]
                q = q_ref[r0:r0 + c, h * GLA_DK:(h + 1) * GLA_DK].astype(F32) * (GLA_DK ** -0.5)
                k = k_ref[r0:r0 + c, h * GLA_DK:(h + 1) * GLA_DK].astype(F32)
                v = v_ref[r0:r0 + c, h * GLA_DV:(h + 1) * GLA_DV]
                qe = (q * jnp.exp(b)).astype(BF16)
                kd = (k * jnp.exp(-b)).astype(BF16)
                kl = (k * jnp.exp(b_last - b)).astype(BF16)
                att = jnp.where(keep, _dot_nt(qe, kd), 0.0).astype(BF16)
                st = st_ref[d, h]
                o = _dot(att, v) + _dot(qe, st.astype(BF16))
                o_ref[r0:r0 + c, h * GLA_DV:(h + 1) * GLA_DV] = o.astype(o_ref.dtype)
                dec = jnp.exp(tot)
                st_ref[d, h] = jnp.concatenate([dec, dec], axis=1) * st + _dot_tn(kl, v)

    @pl.when(s == pl.num_programs(1) - 1)
    def _():
        sf_ref[0] = st_ref[0]
        sb_ref[0] = st_ref[1]


def _gla(main, small, wg, bg, s0f, s0b, *, batch, t):
    rows = _tile(t, 256, GLA_CHUNK)
    ns = t // rows
    hv, hk = GLA_HEADS * GLA_DV, GLA_HEADS * GLA_DK
    fwd = lambda blk: (lambda b, s: (b * ns + s, blk))
    bwd = lambda blk: (lambda b, s: (b * ns + ns - 1 - s, blk))
    st_spec = pl.BlockSpec((1, GLA_HEADS, GLA_DK, GLA_DV), lambda b, s: (b, 0, 0, 0))
    st_shape = jax.ShapeDtypeStruct((batch, GLA_HEADS, GLA_DK, GLA_DV), F32)

    def io(m):
        return [pl.BlockSpec((rows, hk), m(EV_GQ // hk)), pl.BlockSpec((rows, hk), m(EV_GK // hk)),
                pl.BlockSpec((rows, hv), m(EV_GV // hv)), pl.BlockSpec((rows, EV_SMALL), m(0))]

    return pl.pallas_call(
        functools.partial(_gla_kernel, n_sub=rows // GLA_CHUNK),
        grid=(batch, ns),
        in_specs=io(fwd) + io(bwd) + [
            pl.BlockSpec((2, EV_SMALL, hk), lambda b, s: (0, 0, 0)),
            pl.BlockSpec((2, 1, hk), lambda b, s: (0, 0, 0)),
            st_spec, st_spec],
        out_specs=[pl.BlockSpec((rows, hv), fwd(0)), pl.BlockSpec((rows, hv), bwd(0)), st_spec, st_spec],
        out_shape=[jax.ShapeDtypeStruct((batch * t, hv), BF16), jax.ShapeDtypeStruct((batch * t, hv), BF16),
                   st_shape, st_shape],
        scratch_shapes=[pltpu.VMEM((2, GLA_HEADS, GLA_DK, GLA_DV), F32)],
        compiler_params=pltpu.CompilerParams(dimension_semantics=("parallel", "arbitrary"),
                                             vmem_limit_bytes=_vmem(16 << 20)),
        name="gla_scan",
    )(main, main, main, small, main, main, main, small, wg, bg, s0f, s0b)


def _mla_q_kernel(x_ref, g_ref, w_ref, cs_ref, sn_ref, o_ref, hn_ref):
    @pl.when(pl.program_id(1) == 0)
    def _():
        hn_ref[...] = _rms(x_ref[...].astype(F32), g_ref[...]).astype(BF16)

    r = _dot(hn_ref[...], w_ref[0])
    qr = r[:, MLA_NOPE:MLA_NOPE + MLA_ROPE] * cs_ref[...] + r[:, MLA_NOPE + MLA_ROPE:] * sn_ref[...]
    o_ref[0, 0] = (jnp.concatenate([r[:, :MLA_NOPE], qr], axis=1) * MLA_Q_SCALE).astype(o_ref.dtype)


def _mla_q(main, g, w, cs, sn, *, t):
    m = main.shape[0]
    tm = _tile(t, 1024)
    tpb = t // tm
    return pl.pallas_call(
        _mla_q_kernel,
        grid=(m // tm, MLA_HEADS),
        in_specs=[pl.BlockSpec((tm, MLA_Q_RANK), lambda i, h: (i, EV_MQ // MLA_Q_RANK)),
                  pl.BlockSpec((1, MLA_Q_RANK), lambda i, h: (0, 0)),
                  pl.BlockSpec((1, MLA_Q_RANK, 2 * MLA_NOPE), lambda i, h: (h, 0, 0)),
                  pl.BlockSpec((tm, MLA_ROPE), lambda i, h: (i % tpb, 0)),
                  pl.BlockSpec((tm, MLA_ROPE), lambda i, h: (i % tpb, 0))],
        out_specs=pl.BlockSpec((1, 1, tm, MLA_DQK), lambda i, h: (i // tpb, h, i % tpb, 0)),
        out_shape=jax.ShapeDtypeStruct((m // t, MLA_HEADS, t, MLA_DQK), BF16),
        scratch_shapes=[pltpu.VMEM((tm, MLA_Q_RANK), BF16)],
        compiler_params=pltpu.CompilerParams(dimension_semantics=("parallel", "arbitrary")),
        name="mla_q_proj",
    )(main, g.reshape(1, -1), w, cs, sn)


def _mla_kv_kernel(x_ref, kr_ref, g_ref, w_ref, cs_ref, sn_ref, k_ref, v_ref, hn_ref):
    @pl.when(pl.program_id(1) == 0)
    def _():
        hn_ref[...] = _rms(x_ref[...].astype(F32), g_ref[...]).astype(BF16)

    r = _dot(hn_ref[...], w_ref[0])
    kr = kr_ref[...].astype(F32)
    krr = kr[:, :MLA_ROPE] * cs_ref[...] + kr[:, MLA_ROPE:2 * MLA_ROPE] * sn_ref[...]
    k_ref[0, 0] = jnp.concatenate([r[:, :MLA_NOPE], krr], axis=1).astype(k_ref.dtype)
    v_ref[0, 0] = r[:, MLA_NOPE:].astype(v_ref.dtype)


def _mla_kv(main, g, w, cs, sn, *, t):
    m = main.shape[0]
    tm = _tile(t, 1024)
    tpb = t // tm
    return pl.pallas_call(
        _mla_kv_kernel,
        grid=(m // tm, MLA_HEADS),
        in_specs=[pl.BlockSpec((tm, MLA_KV_RANK), lambda i, h: (i, EV_MKV // MLA_KV_RANK)),
                  pl.BlockSpec((tm, 256), lambda i, h: (i, EV_KR // 256)),
                  pl.BlockSpec((1, MLA_KV_RANK), lambda i, h: (0, 0)),
                  pl.BlockSpec((1, MLA_KV_RANK, MLA_NOPE + MLA_DV), lambda i, h: (h, 0, 0)),
                  pl.BlockSpec((tm, MLA_ROPE), lambda i, h: (i % tpb, 0)),
                  pl.BlockSpec((tm, MLA_ROPE), lambda i, h: (i % tpb, 0))],
        out_specs=[pl.BlockSpec((1, 1, tm, MLA_DQK), lambda i, h: (i // tpb, h, i % tpb, 0)),
                   pl.BlockSpec((1, 1, tm, MLA_DV), lambda i, h: (i // tpb, h, i % tpb, 0))],
        out_shape=[jax.ShapeDtypeStruct((m // t, MLA_HEADS, t, MLA_DQK), BF16),
                   jax.ShapeDtypeStruct((m // t, MLA_HEADS, t, MLA_DV), BF16)],
        scratch_shapes=[pltpu.VMEM((tm, MLA_KV_RANK), BF16)],
        compiler_params=pltpu.CompilerParams(dimension_semantics=("parallel", "arbitrary")),
        name="mla_kv_proj",
    )(main, main, g.reshape(1, -1), w, cs, sn)


ATTN_ROWS = 256


def _attn_kernel(q_ref, k_ref, v_ref, o_ref, m_ref, acc_ref):
    kv = pl.program_id(3)

    @pl.when(kv == 0)
    def _():
        m_ref[...] = jnp.full(m_ref.shape, -jnp.inf, F32)
        acc_ref[...] = jnp.zeros(acc_ref.shape, F32)

    k = k_ref[0, 0]
    v = v_ref[0, 0]
    v_ones = jnp.concatenate([v, jnp.ones(v.shape, v.dtype)], axis=1)
    tq = q_ref.shape[2]
    rc = min(ATTN_ROWS, tq)
    for r in range(tq // rc):
        rows = slice(r * rc, (r + 1) * rc)
        s = _dot_nt(q_ref[0, 0, rows, :], k)
        m_old = m_ref[rows, :]
        m_new = jnp.maximum(m_old, jnp.max(s, axis=-1, keepdims=True))
        p = jnp.exp2(s - m_new).astype(BF16)
        acc_ref[rows, :] = jnp.exp2(m_old - m_new) * acc_ref[rows, :] + _dot(p, v_ones)
        m_ref[rows, :] = m_new

    @pl.when(kv == pl.num_programs(3) - 1)
    def _():
        acc = acc_ref[...]
        o_ref[...] = (acc[:, :MLA_DV] / acc[:, MLA_DV:]).astype(o_ref.dtype)


def _attn(q, k, v, *, tq_pref=1024, tk_pref=3328):
    b, h, tq_all, dqk = q.shape
    tk_all = k.shape[2]
    tq, tk = _tile(tq_all, tq_pref), _tile(tk_all, tk_pref)
    nq = tq_all // tq
    return pl.pallas_call(
        _attn_kernel,
        grid=(b, h, nq, tk_all // tk),
        in_specs=[pl.BlockSpec((1, 1, tq, dqk), lambda bi, hi, i, j: (bi, hi, i, 0)),
                  pl.BlockSpec((1, 1, tk, dqk), lambda bi, hi, i, j: (bi, hi, j, 0)),
                  pl.BlockSpec((1, 1, tk, MLA_DV), lambda bi, hi, i, j: (bi, hi, j, 0))],
        out_specs=pl.BlockSpec((tq, MLA_DV), lambda bi, hi, i, j: (bi * nq + i, hi)),
        out_shape=jax.ShapeDtypeStruct((b * tq_all, h * MLA_DV), BF16),
        scratch_shapes=[pltpu.VMEM((tq, 1), F32), pltpu.VMEM((tq, 2 * MLA_DV), F32)],
        compiler_params=pltpu.CompilerParams(
            dimension_semantics=("parallel", "parallel", "parallel", "arbitrary"),
            vmem_limit_bytes=_vmem(6 * min(ATTN_ROWS, tq) * tk * 4 + 8 * tk * 256 * 2)),
        name="mla_attention",
    )(q, k, v)


def _mixout_kernel(*refs, n_heads, dv, center, has_mla):
    of_ref, ob_ref, gate_ref, g_ref = refs[:4]
    k = 4
    if has_mla:
        mla_ref = refs[k]; k += 1
    w_ref, x_ref, ga_ref, o_ref = refs[k:k + 4]
    o = of_ref[...].astype(F32) + ob_ref[...].astype(F32)
    g = g_ref[...]
    parts = []
    for h in range(n_heads):
        oh = o[:, h * dv:(h + 1) * dv]
        if center:
            oh = oh - jnp.mean(oh, axis=-1, keepdims=True)
        parts.append(oh * lax.rsqrt(jnp.mean(oh * oh, axis=-1, keepdims=True) + EPS) * g)
    lat = jnp.concatenate(parts, axis=1) * _silu(gate_ref[...].astype(F32))
    lhs = lat.astype(BF16)
    if has_mla:
        lhs = jnp.concatenate([lhs, mla_ref[...]], axis=1)
    o_ref[...] = x_ref[...] + ga_ref[0] * _dot(lhs, w_ref[...])


def _mixout(o_f, o_b, main, gate_blk, g, mla, w, x, ga, *, rows_per_group, n_heads, dv, center):
    m, d = x.shape
    hw = n_heads * dv
    tm = _tile(rows_per_group, 512)
    gpt = rows_per_group // tm
    in_specs = [pl.BlockSpec((tm, hw), lambda i: (i, 0)), pl.BlockSpec((tm, hw), lambda i: (i, 0)),
                pl.BlockSpec((tm, hw), lambda i: (i, gate_blk)), pl.BlockSpec((1, dv), lambda i: (0, 0))]
    args = [o_f, o_b, main, g.reshape(1, dv)]
    if mla is not None:
        in_specs.append(pl.BlockSpec((tm, mla.shape[1]), lambda i: (i, 0)))
        args.append(mla)
    in_specs += [pl.BlockSpec(w.shape, lambda i: (0, 0)), pl.BlockSpec((tm, d), lambda i: (i, 0)),
                 pl.BlockSpec((1, 1, d), lambda i: (i // gpt, 0, 0))]
    args += [w, x, ga]
    return pl.pallas_call(
        functools.partial(_mixout_kernel, n_heads=n_heads, dv=dv, center=center, has_mla=mla is not None),
        grid=(m // tm,),
        in_specs=in_specs,
        out_specs=pl.BlockSpec((tm, d), lambda i: (i, 0)),
        out_shape=jax.ShapeDtypeStruct((m, d), F32),
        compiler_params=pltpu.CompilerParams(
            dimension_semantics=("parallel",),
            vmem_limit_bytes=_vmem(2 * w.size * 2 + 4 * tm * d * 4 + 8 * tm * hw * 2 + 6 * tm * hw * 4)),
        name="mixer_out",
    )(*args)


def _ffn_kernel(te_ref, nv_ref, *refs, prenorm):
    if prenorm:
        x_ref, g_ref, sc_ref, sh_ref, wg_ref, wu_ref, wd_ref, gf_ref, o_ref, hn_ref, acc_ref = refs
    else:
        x_ref, wg_ref, wu_ref, wd_ref, o_ref, hn_ref, acc_ref = refs
    i, j = pl.program_id(0), pl.program_id(1)
    valid = i < nv_ref[0]

    @pl.when(jnp.logical_and(valid, j == 0))
    def _():
        if prenorm:
            y = _rms(x_ref[...], g_ref[...]) * sc_ref[0] + sh_ref[0]
        else:
            y = x_ref[...]
        hn_ref[...] = y.astype(BF16)
        acc_ref[...] = jnp.zeros(acc_ref.shape, F32)

    @pl.when(valid)
    def _():
        hn = hn_ref[...]
        a = (_silu(_dot(hn, wg_ref[0])) * _dot(hn, wu_ref[0])).astype(BF16)
        acc_ref[...] += _dot(a, wd_ref[0])

    last = j == pl.num_programs(1) - 1

    @pl.when(jnp.logical_and(valid, last))
    def _():
        if prenorm:
            o_ref[...] = x_ref[...] + gf_ref[0] * acc_ref[...]
        else:
            o_ref[...] = acc_ref[...]

    @pl.when(jnp.logical_and(jnp.logical_not(valid), last))
    def _():
        o_ref[...] = jnp.zeros(o_ref.shape, o_ref.dtype)


def _ffn(x, wg, wu, wd, tile_expert, n_valid, *, tm, norm=None):
    p, d = x.shape
    f = wg.shape[2]
    tf = _tile(f, 512)
    nt = p // tm
    wmap_in = lambda i, j, te, nv: (te[i], 0, j)
    wmap_out = lambda i, j, te, nv: (te[i], j, 0)
    row = lambda i, j, te, nv: (i, 0)
    in_specs = [pl.BlockSpec((tm, d), lambda i, j, te, nv: (jnp.minimum(i, nv[0] - 1), 0))]
    args = [x]
    if norm is not None:
        g, sc, sh, gf, rpg = norm
        gpt = rpg // tm
        grp = lambda i, j, te, nv: (i // gpt, 0, 0)
        in_specs += [pl.BlockSpec((1, d), lambda i, j, te, nv: (0, 0)), pl.BlockSpec((1, 1, d), grp),
                     pl.BlockSpec((1, 1, d), grp)]
        args += [g.reshape(1, d), sc, sh]
    in_specs += [pl.BlockSpec((1, d, tf), wmap_in), pl.BlockSpec((1, d, tf), wmap_in),
                 pl.BlockSpec((1, tf, d), wmap_out)]
    args += [wg, wu, wd]
    if norm is not None:
        in_specs.append(pl.BlockSpec((1, 1, d), grp))
        args.append(gf)
    return pl.pallas_call(
        functools.partial(_ffn_kernel, prenorm=norm is not None),
        grid_spec=pltpu.PrefetchScalarGridSpec(
            num_scalar_prefetch=2,
            grid=(nt, f // tf),
            in_specs=in_specs,
            out_specs=pl.BlockSpec((tm, d), row),
            scratch_shapes=[pltpu.VMEM((tm, d), BF16), pltpu.VMEM((tm, d), F32)]),
        out_shape=jax.ShapeDtypeStruct((p, d), F32),
        compiler_params=pltpu.CompilerParams(
            dimension_semantics=("parallel", "arbitrary"),
            vmem_limit_bytes=_vmem(4 * tm * d * 4 + tm * d * 6 + 6 * d * tf * 2 + 5 * tm * tf * 4)),
        name="swiglu_ffn",
    )(tile_expert, n_valid, *args)


def _ret_kernel(qf_ref, kf_ref, vf_ref, qb_ref, kb_ref, vb_ref, dm_ref, qd_ref, kd_ref, cd_ref,
                s0f_ref, s0b_ref, of_ref, ob_ref, sf_ref, sb_ref, st_ref, *, n_sub):
    s = pl.program_id(1)
    c = RET_CHUNK

    @pl.when(s == 0)
    def _():
        st_ref[0] = s0f_ref[0]
        st_ref[1] = s0b_ref[0]

    for d, (q_ref, k_ref, v_ref, o_ref) in enumerate(((qf_ref, kf_ref, vf_ref, of_ref),
                                                       (qb_ref, kb_ref, vb_ref, ob_ref))):
        order = range(n_sub) if d == 0 else range(n_sub - 1, -1, -1)
        for ci in order:
            r0 = ci * c
            for h in range(RET_HEADS):
                q = q_ref[r0:r0 + c, h * RET_DK:(h + 1) * RET_DK]
                k = (k_ref[r0:r0 + c, h * RET_DK:(h + 1) * RET_DK].astype(F32) * (RET_DK ** -0.5)).astype(BF16)
                v = v_ref[r0:r0 + c, h * RET_DV:(h + 1) * RET_DV]
                att = (_dot_nt(q, k) * dm_ref[d, h]).astype(BF16)
                st = st_ref[d, h]
                o = _dot(att, v) + qd_ref[d, h] * _dot(q, st.astype(BF16))
                o_ref[r0:r0 + c, h * RET_DV:(h + 1) * RET_DV] = o.astype(o_ref.dtype)
                vk = (v.astype(F32) * kd_ref[d, h]).astype(BF16)
                st_ref[d, h] = cd_ref[d, h] * st + _dot_tn(k, vk)

    @pl.when(s == pl.num_programs(1) - 1)
    def _():
        sf_ref[0] = st_ref[0]
        sb_ref[0] = st_ref[1]


def _ret(main, tables, s0f, s0b, *, batch, t):
    rows = _tile(t, 256, RET_CHUNK)
    ns = t // rows
    hk, hv = RET_HEADS * RET_DK, RET_HEADS * RET_DV
    fwd = lambda blk: (lambda b, s: (b * ns + s, blk))
    bwd = lambda blk: (lambda b, s: (b * ns + ns - 1 - s, blk))
    st_spec = pl.BlockSpec((1, RET_HEADS, RET_DK, RET_DV), lambda b, s: (b, 0, 0, 0))
    st_shape = jax.ShapeDtypeStruct((batch, RET_HEADS, RET_DK, RET_DV), F32)
    io = lambda m: [pl.BlockSpec((rows, hk), m(0)), pl.BlockSpec((rows, hk), m(1)), pl.BlockSpec((rows, hv), m(2))]
    full = lambda a: pl.BlockSpec(a.shape, lambda b, s: (0,) * a.ndim)
    return pl.pallas_call(
        functools.partial(_ret_kernel, n_sub=rows // RET_CHUNK),
        grid=(batch, ns),
        in_specs=io(fwd) + io(bwd) + [full(a) for a in tables] + [st_spec, st_spec],
        out_specs=[pl.BlockSpec((rows, hv), fwd(0)), pl.BlockSpec((rows, hv), bwd(0)), st_spec, st_spec],
        out_shape=[jax.ShapeDtypeStruct((batch * t, hv), BF16), jax.ShapeDtypeStruct((batch * t, hv), BF16),
                   st_shape, st_shape],
        scratch_shapes=[pltpu.VMEM((2, RET_HEADS, RET_DK, RET_DV), F32)],
        compiler_params=pltpu.CompilerParams(dimension_semantics=("parallel", "arbitrary"),
                                             vmem_limit_bytes=_vmem(40 << 20)),
        name="retention_scan",
    )(main, main, main, main, main, main, *tables, s0f, s0b)


def _router_kernel(x_ref, g_ref, sc_ref, sh_ref, wr_ref, meta_ref, gate_ref, cnt_ref, run_ref):
    i = pl.program_id(0)

    @pl.when(i == 0)
    def _():
        run_ref[...] = jnp.zeros(run_ref.shape, F32)

    y = _rms(x_ref[...], g_ref[...]) * sc_ref[0] + sh_ref[0]
    tm = y.shape[0]
    lt = jnp.transpose(_dot3(y, wr_ref[...]))[:N_EXPERTS, :]
    eid = lax.broadcasted_iota(jnp.int32, lt.shape, 0).astype(F32)
    v1 = jnp.max(lt, axis=0, keepdims=True)
    e1 = jnp.min(jnp.where(lt == v1, eid, float(N_EXPERTS)), axis=0, keepdims=True)
    lt2 = jnp.where(eid == e1, -jnp.inf, lt)
    v2 = jnp.max(lt2, axis=0, keepdims=True)
    e2 = jnp.min(jnp.where(lt2 == v2, eid, float(N_EXPERTS)), axis=0, keepdims=True)
    ex = jnp.exp(v2 - v1)
    w1 = 1.0 / (1.0 + ex)
    w2 = ex / (1.0 + ex)
    oh1 = jnp.where(eid == e1, 1.0, 0.0)
    oh2 = jnp.where(eid == e2, 1.0, 0.0)
    mem = oh1 + oh2
    r = lax.broadcasted_iota(jnp.int32, (tm, tm), 0)
    cidx = lax.broadcasted_iota(jnp.int32, (tm, tm), 1)
    before = jnp.where(r < cidx, 1.0, 0.0).astype(BF16)
    excl = _dot(mem.astype(BF16), before) + run_ref[...][:, :1]
    rk1 = jnp.sum(oh1 * excl, axis=0, keepdims=True)
    rk2 = jnp.sum(oh2 * excl, axis=0, keepdims=True)
    run_ref[...] = run_ref[...] + jnp.sum(mem, axis=1, keepdims=True)
    zero = jnp.zeros_like(e1)
    meta_ref[...] = jnp.concatenate([e1, e2, rk1, rk2, w1, w2, zero, zero], axis=0)
    wpad = jnp.concatenate([w1, w2, jnp.zeros((6, tm), F32)], axis=0)
    gate_ref[...] = jnp.transpose(jnp.concatenate([wpad] * 16, axis=0))
    cnt_ref[...] = run_ref[...]


def _router(x, g, sc, sh, wr, *, rows_per_group):
    m, d = x.shape
    tm = _tile(rows_per_group, ROUTER_ROWS)
    gpt = rows_per_group // tm
    grp = lambda i: (i // gpt, 0, 0)
    return pl.pallas_call(
        _router_kernel,
        grid=(m // tm,),
        in_specs=[pl.BlockSpec((tm, d), lambda i: (i, 0)), pl.BlockSpec((1, d), lambda i: (0, 0)),
                  pl.BlockSpec((1, 1, d), grp), pl.BlockSpec((1, 1, d), grp),
                  pl.BlockSpec((d, 128), lambda i: (0, 0))],
        out_specs=[pl.BlockSpec((8, tm), lambda i: (0, i)),
                   pl.BlockSpec((tm, 128), lambda i: (i, 0)), pl.BlockSpec((N_EXPERTS, 128), lambda i: (0, 0))],
        out_shape=[jax.ShapeDtypeStruct((8, m), F32),
                   jax.ShapeDtypeStruct((m, 128), F32), jax.ShapeDtypeStruct((N_EXPERTS, 128), F32)],
        scratch_shapes=[pltpu.VMEM((N_EXPERTS, 128), F32)],
        compiler_params=pltpu.CompilerParams(dimension_semantics=("arbitrary",),
                                             vmem_limit_bytes=_vmem(6 * tm * d * 4)),
        name="moe_router",
    )(x, g.reshape(1, d), sc, sh, wr)


def _scatter_kernel(zs_ref, pos_ref, x_ref, g_ref, sc_ref, sh_ref, o_ref, hbuf, sem, *, group_rows):
    tm = hbuf.shape[0]

    @pl.when(pl.program_id(0) == 0)
    def _():
        hbuf[...] = jnp.zeros(hbuf.shape, hbuf.dtype)
        for e in range(N_EXPERTS):
            @pl.when(zs_ref[e] >= 0)
            def _():
                start = pl.multiple_of(zs_ref[e], 8)
                for c in range(group_rows // tm):
                    cp = pltpu.make_async_copy(hbuf, o_ref.at[pl.ds(start + c * tm, tm)], sem.at[0])
                    cp.start()
                    cp.wait()

    hbuf[...] = _rms(x_ref[...], g_ref[...]) * sc_ref[0] + sh_ref[0]

    def copies(t):
        return (pltpu.make_async_copy(hbuf.at[t], o_ref.at[pos_ref[0, 0, t]], sem.at[0]),
                pltpu.make_async_copy(hbuf.at[t], o_ref.at[pos_ref[0, 1, t]], sem.at[1]))

    def issue(t, carry):
        for cp in copies(t):
            cp.start()
        return carry

    lax.fori_loop(0, tm, issue, 0)

    def drain(t, carry):
        for cp in copies(t):
            cp.wait()
        return carry

    lax.fori_loop(0, tm, drain, 0)


def _scatter_rows(x, g, sc, sh, pos, zero_start, *, n_rows, group_rows, rows_per_group):
    m, d = x.shape
    nt, _, tm = pos.shape
    gpt = rows_per_group // tm
    grp = lambda i, zs: (i // gpt, 0, 0)
    return pl.pallas_call(
        functools.partial(_scatter_kernel, group_rows=group_rows),
        grid_spec=pltpu.PrefetchScalarGridSpec(
            num_scalar_prefetch=1,
            grid=(nt,),
            in_specs=[pl.BlockSpec((1, 2, tm), lambda i, zs: (i, 0, 0), memory_space=pltpu.SMEM),
                      pl.BlockSpec((tm, d), lambda i, zs: (i, 0)), pl.BlockSpec((1, d), lambda i, zs: (0, 0)),
                      pl.BlockSpec((1, 1, d), grp), pl.BlockSpec((1, 1, d), grp)],
            out_specs=pl.BlockSpec(memory_space=pl.ANY),
            scratch_shapes=[pltpu.VMEM((tm, d), F32), pltpu.SemaphoreType.DMA((2,))]),
        out_shape=jax.ShapeDtypeStruct((n_rows, d), F32),
        compiler_params=pltpu.CompilerParams(dimension_semantics=("arbitrary",),
                                             vmem_limit_bytes=_vmem(6 * tm * d * 4)),
        name="moe_scatter",
    )(zero_start, pos, x, g.reshape(1, d), sc, sh)


def _combine_kernel(pos_ref, y_ref, x_ref, gate_ref, gf_ref, gn_ref, o_ref, buf_ref, sem):
    tm = pos_ref.shape[2]

    def copies(t):
        return (pltpu.make_async_copy(y_ref.at[pos_ref[0, 0, t]], buf_ref.at[0, t], sem.at[0]),
                pltpu.make_async_copy(y_ref.at[pos_ref[0, 1, t]], buf_ref.at[1, t], sem.at[1]))

    def issue(t, carry):
        for cp in copies(t):
            cp.start()
        return carry

    lax.fori_loop(0, tm, issue, 0)

    def drain(t, carry):
        for cp in copies(t):
            cp.wait()
        return carry

    lax.fori_loop(0, tm, drain, 0)
    w = gate_ref[...]
    moe = w[:, 0:1] * buf_ref[0] + w[:, 1:2] * buf_ref[1]
    o_ref[...] = _rms(x_ref[...] + gf_ref[0] * moe, gn_ref[...])


def _combine(y, pos, x, gate, gf, gn, *, rows_per_group):
    m, d = x.shape
    nt, _, tm = pos.shape
    gpt = rows_per_group // tm
    return pl.pallas_call(
        _combine_kernel,
        grid=(nt,),
        in_specs=[pl.BlockSpec((1, 2, tm), lambda i: (i, 0, 0), memory_space=pltpu.SMEM),
                  pl.BlockSpec(memory_space=pl.ANY),
                  pl.BlockSpec((tm, d), lambda i: (i, 0)), pl.BlockSpec((tm, 128), lambda i: (i, 0)),
                  pl.BlockSpec((1, 1, d), lambda i: (i // gpt, 0, 0)), pl.BlockSpec((1, d), lambda i: (0, 0))],
        out_specs=pl.BlockSpec((tm, d), lambda i: (i, 0)),
        out_shape=jax.ShapeDtypeStruct((m, d), F32),
        scratch_shapes=[pltpu.VMEM((2, tm, d), F32), pltpu.SemaphoreType.DMA((2,))],
        compiler_params=pltpu.CompilerParams(dimension_semantics=("arbitrary",),
                                             vmem_limit_bytes=_vmem(8 * tm * d * 4)),
        name="moe_combine",
    )(pos, y, x, gate, gf, gn.reshape(1, d))


def _axial_angles(rows, dim):
    row = jnp.repeat(jnp.arange(rows, dtype=F32), GRID_W)
    col = jnp.tile(jnp.arange(GRID_W, dtype=F32), rows)
    half = dim // 2
    inv = 1.0 / (ROPE_BASE ** (jnp.arange(0, half, 2, dtype=F32) / half))
    return jnp.concatenate([row[:, None] * inv, col[:, None] * inv], axis=-1)


def _rope_tables(rows, dim, t_ctx):
    ang = _axial_angles(rows, dim)
    cs = jnp.concatenate([jnp.cos(ang), jnp.cos(ang)], axis=1)
    sn = jnp.concatenate([-jnp.sin(ang), jnp.sin(ang)], axis=1)
    return (cs, sn), (jnp.ones((t_ctx, dim), F32), jnp.zeros((t_ctx, dim), F32))


def _evens_odds(w):
    pairs = w.reshape(w.shape[:-1] + (w.shape[-1] // 2, 2))
    return pairs[..., 0], pairs[..., 1]


def _even_weights(w_in, w_g2, b_g2, w_uq, w_ukv):
    d = w_in.shape[0]
    sizes = (512, 512, 1024, 1024, 16, 16, MLA_Q_RANK, MLA_KV_RANK, MLA_ROPE)
    offs = np.cumsum((0,) + sizes)
    gq, gk, gv, gr, gaf, gab, mq, mkv, mkr = [w_in[:, offs[i]:offs[i + 1]] for i in range(9)]
    kr_e, kr_o = _evens_odds(mkr)
    w_main = jnp.concatenate([gv, gr, gq, gk, mq, kr_e, kr_o, kr_o, kr_e, jnp.zeros((d, 128), F32), mkv],
                             axis=1).astype(BF16)
    w_small = jnp.concatenate([gaf, gab, jnp.zeros((d, EV_SMALL - 2 * GLA_RANK), F32)], axis=1)
    wg = jnp.zeros((2, EV_SMALL, GLA_HEADS * GLA_DK), F32)
    wg = wg.at[0, :GLA_RANK].set(w_g2[0]).at[1, GLA_RANK:2 * GLA_RANK].set(w_g2[1])
    bg = b_g2.reshape(2, 1, -1)
    wq = w_uq.reshape(MLA_Q_RANK, MLA_HEADS, MLA_DQK)
    qr_e, qr_o = _evens_odds(wq[:, :, MLA_NOPE:])
    wq = jnp.concatenate([wq[:, :, :MLA_NOPE], qr_e, qr_o, qr_o, qr_e], axis=2)
    wq = jnp.transpose(wq, (1, 0, 2)).astype(BF16)
    wkv = jnp.transpose(w_ukv.reshape(MLA_KV_RANK, MLA_HEADS, MLA_NOPE + MLA_DV), (1, 0, 2)).astype(BF16)
    return w_main, w_small, wg, bg, wq, wkv


def _odd_weights(w_in):
    d = w_in.shape[0]
    hk = RET_HEADS * RET_DK
    qk = w_in[:, :2 * hk].reshape(d, 2 * RET_HEADS, RET_DK)
    ev, od = _evens_odds(qk)
    qk = jnp.concatenate([ev, od], axis=2).reshape(d, 2 * hk)
    return jnp.concatenate([qk, w_in[:, 2 * hk:]], axis=1).astype(BF16)


def _ret_tables(log_decay):
    lg = -jnp.exp(log_decay.astype(F32))
    c = RET_CHUNK
    pos = jnp.arange(c, dtype=F32)
    diff = pos[:, None] - pos[None, :]
    lgd = lg[:, :, None, None]
    fmask = jnp.where(diff >= 0, jnp.exp(lgd * jnp.maximum(diff, 0.0)), 0.0)
    bmask = jnp.where(diff <= 0, jnp.exp(lgd * jnp.maximum(-diff, 0.0)), 0.0)
    dm = jnp.stack([fmask[0], bmask[1]])
    lgc = lg[:, :, None, None]
    qd = jnp.stack([jnp.exp(lgc[0] * (pos + 1.0)[None, :, None]), jnp.exp(lgc[1] * (c - pos)[None, :, None])])
    kd = jnp.stack([jnp.exp(lgc[0] * (c - 1.0 - pos)[None, :, None]), jnp.exp(lgc[1] * pos[None, :, None])])
    cd = jnp.exp(lgc * c)
    return dm, qd, kd, cd


def _dispatch_plan(meta, counts, tm, n_tiles):
    cnt = counts[:, 0].astype(jnp.int32)
    tiles = (cnt + tm - 1) // tm
    tile_end = jnp.cumsum(tiles)
    start = (tile_end - tiles) * tm
    e = meta[0:2].astype(jnp.int32)
    pos = start[e] + meta[2:4].astype(jnp.int32)
    tile_ids = jnp.arange(n_tiles, dtype=jnp.int32)
    n_valid = tile_end[-1:].astype(jnp.int32)
    last_valid = jnp.minimum(tile_ids, n_valid[0] - 1)
    te = jnp.sum((last_valid[:, None] >= tile_end[None, :]).astype(jnp.int32), axis=1)
    te = jnp.minimum(te, N_EXPERTS - 1).astype(jnp.int32)
    zero_start = jnp.where(tiles > 0, (tile_end - 1) * tm, -1).astype(jnp.int32)
    return pos, te, n_valid, zero_start


def kernel(x, c, ctx, c_ctx, ada_w, ada_b, norm_mix, norm_ffn, norm_final, ev_w_in, gla_w_gate2, gla_b_gate2, gla_norm, mla_q_norm, mla_w_uq, mla_kv_norm, mla_w_ukv, ev_w_out, ffn_w_gate, ffn_w_up, ffn_w_down, od_w_in, ret_log_decay, ret_norm, od_w_out, moe_router, moe_w_gate, moe_w_up, moe_w_down):
    B, T, D = x.shape
    TC = ctx.shape[1]
    depth = ada_w.shape[0]
    assert depth == 2 and T % GRID_W == 0 and B <= 7
    rows = T // GRID_W
    xl = x.reshape(B * T, D)
    xc = ctx.reshape(B * TC, D)

    c_rows = jnp.concatenate([c, c_ctx[None, :], jnp.zeros((8 - B - 1, D), F32)], axis=0)
    mods = []
    for i in range(depth):
        m = _ada(c_rows, ada_w[i], ada_b[i]).reshape(8, 6, 1, D)
        mods.append([(m[:B, k], m[B:B + 1, k]) for k in range(6)])

    (sh_a, sc_a, g_a, sh_f, sc_f, g_f) = mods[0]
    w_main, w_small, wg2, bg2, wq, wkv = _even_weights(ev_w_in[0], gla_w_gate2[0], gla_b_gate2[0],
                                                       mla_w_uq[0], mla_w_ukv[0])
    tn_ev = _tile(EV_MAIN, 768)
    main_l, small_l = _nmm(xl, norm_mix[0], 1.0 + sc_a[0], sh_a[0], w_main, rows_per_group=T, tn=tn_ev,
                           w_small=w_small)
    main_c, small_c = _nmm(xc, norm_mix[0], 1.0 + sc_a[1], sh_a[1], w_main, rows_per_group=B * TC, tn=tn_ev,
                           w_small=w_small)

    s0 = jnp.zeros((B, GLA_HEADS, GLA_DK, GLA_DV), F32)
    ocf, ocb, s_f, s_b = _gla(main_c, small_c, wg2, bg2, s0, s0, batch=B, t=TC)
    olf, olb, _, _ = _gla(main_l, small_l, wg2, bg2, s_f, s_b, batch=B, t=T)

    (cs_m, sn_m), (cs_1, sn_0) = _rope_tables(rows, MLA_ROPE, TC)
    q_l = _mla_q(main_l, mla_q_norm[0], wq, cs_m, sn_m, t=T)
    q_c = _mla_q(main_c, mla_q_norm[0], wq, cs_1, sn_0, t=TC)
    k_l, v_l = _mla_kv(main_l, mla_kv_norm[0], wkv, cs_m, sn_m, t=T)
    k_c, v_c = _mla_kv(main_c, mla_kv_norm[0], wkv, cs_1, sn_0, t=TC)
    mla_l = _attn(q_l, jnp.concatenate([k_l, k_c], axis=2), jnp.concatenate([v_l, v_c], axis=2))
    mla_c = _attn(q_c, k_c, v_c)

    w_out = ev_w_out[0].astype(BF16)
    gr_blk = EV_GR // (GLA_HEADS * GLA_DV)
    x1l = _mixout(olf, olb, main_l, gr_blk, gla_norm[0], mla_l, w_out, xl, g_a[0], rows_per_group=T,
                  n_heads=GLA_HEADS, dv=GLA_DV, center=False)
    x1c = _mixout(ocf, ocb, main_c, gr_blk, gla_norm[0], mla_c, w_out, xc, g_a[1], rows_per_group=B * TC,
                  n_heads=GLA_HEADS, dv=GLA_DV, center=False)

    fwg, fwu, fwd_ = ffn_w_gate.astype(BF16), ffn_w_up.astype(BF16), ffn_w_down.astype(BF16)
    tm_l = _tile(T, FFN_ROWS)
    x2l = _ffn(x1l, fwg, fwu, fwd_, jnp.zeros((B * T // tm_l,), jnp.int32), jnp.full((1,), B * T // tm_l, jnp.int32),
               tm=tm_l, norm=(norm_ffn[0], 1.0 + sc_f[0], sh_f[0], g_f[0], T))
    tm_c = _tile(B * TC, FFN_ROWS)
    x2c = _ffn(x1c, fwg, fwu, fwd_, jnp.zeros((B * TC // tm_c,), jnp.int32),
               jnp.full((1,), B * TC // tm_c, jnp.int32), tm=tm_c,
               norm=(norm_ffn[0], 1.0 + sc_f[1], sh_f[1], g_f[1], B * TC))

    (sh_a, sc_a, g_a, sh_f, sc_f, g_f) = mods[1]
    w_odd = _odd_weights(od_w_in[0])
    hk = RET_HEADS * RET_DK
    tn_od = _tile(4 * hk, 1024)
    rope_l, rope_c = _rope_tables(rows, RET_DK, B * TC)
    m2l = _nmm(x2l, norm_mix[1], 1.0 + sc_a[0], sh_a[0], w_odd, rows_per_group=T, tn=tn_od, rope=rope_l,
               rope_cols=2 * hk)
    m2c = _nmm(x2c, norm_mix[1], 1.0 + sc_a[1], sh_a[1], w_odd, rows_per_group=B * TC, tn=tn_od, rope=rope_c,
               rope_cols=2 * hk)
    tables = _ret_tables(ret_log_decay[0])
    s0 = jnp.zeros((B, RET_HEADS, RET_DK, RET_DV), F32)
    _, _, s_f, s_b = _ret(m2c, tables, s0, s0, batch=B, t=TC)
    orf, orb, _, _ = _ret(m2l, tables, s_f, s_b, batch=B, t=T)
    x3 = _mixout(orf, orb, m2l, 3, ret_norm[0], None, od_w_out[0].astype(BF16), x2l, g_a[0], rows_per_group=T,
                 n_heads=RET_HEADS, dv=RET_DV, center=True)

    wr = jnp.concatenate([moe_router[0], jnp.zeros((D, 128 - N_EXPERTS), F32)], axis=1)
    meta, gate, counts = _router(x3, norm_ffn[1], 1.0 + sc_f[0], sh_f[0], wr, rows_per_group=T)
    tm_e = _tile(T, FFN_ROWS)
    n_tiles = (2 * B * T) // tm_e + N_EXPERTS
    pos, te, n_valid, zero_start = _dispatch_plan(meta, counts, tm_e, n_tiles)
    tm_r = _tile(T, ROUTER_ROWS)
    pos_t = jnp.transpose(pos.reshape(2, -1, tm_r), (1, 0, 2))
    hs = _scatter_rows(x3, norm_ffn[1], 1.0 + sc_f[0], sh_f[0], pos_t, zero_start, n_rows=n_tiles * tm_e,
                       group_rows=tm_e, rows_per_group=T)
    ys = _ffn(hs, moe_w_gate[0].astype(BF16), moe_w_up[0].astype(BF16), moe_w_down[0].astype(BF16), te, n_valid,
              tm=tm_e)
    out = _combine(ys, pos_t, x3, gate, g_f[0], norm_final, rows_per_group=T)
    return out.reshape(B, T, D)
```

```python
import functools

import jax
import jax.numpy as jnp
import numpy as np
from jax import lax
from jax.experimental import pallas as pl
from jax.experimental.pallas import tpu as pltpu

F32 = jnp.float32
BF16 = jnp.bfloat16
EPS = 1e-6
ROPE_BASE = 10000.0
GRID_W = 64

GLA_HEADS, GLA_DK, GLA_DV, GLA_RANK, GLA_GATE_NORM, GLA_CHUNK = 4, 128, 256, 16, 16.0, 64
MLA_HEADS, MLA_Q_RANK, MLA_KV_RANK, MLA_NOPE, MLA_ROPE, MLA_DV = 8, 768, 512, 128, 64, 128
MLA_DQK = MLA_NOPE + MLA_ROPE
MLA_SCALE = MLA_DQK ** -0.5
MLA_Q_SCALE = MLA_SCALE * 1.4426950408889634
RET_HEADS, RET_DK, RET_DV, RET_CHUNK = 8, 256, 256, 128
N_EXPERTS = 8

EV_GV, EV_GR, EV_GQ, EV_GK, EV_MQ, EV_KR, EV_MKV, EV_MAIN = 0, 1024, 2048, 2560, 3072, 3840, 4096, 4608
EV_SMALL = 128

V7X_VMEM_BYTES = 64 << 20
VMEM_HEADROOM_BYTES = 6 << 20
FFN_ROWS = 512
MOE_ROWS = 1024
ROUTER_ROWS = 512


def _vmem(nbytes):
    return int(min(V7X_VMEM_BYTES - VMEM_HEADROOM_BYTES, max(32 << 20, nbytes + (8 << 20))))


def _tile(n, pref, align=128):
    if n <= pref:
        return n
    t = (pref // align) * align
    while t >= align:
        if n % t == 0:
            return t
        t -= align
    return n


def _split_bf16(a):
    hi = a.astype(BF16)
    lo = (a - hi.astype(F32)).astype(BF16)
    return hi, lo


def _dot(a, b):
    return jnp.dot(a, b, preferred_element_type=F32)


def _dot3(a, b):
    ah, al = _split_bf16(a)
    bh, bl = _split_bf16(b)
    return _dot(ah, bh) + _dot(al, bh) + _dot(ah, bl)


def _dot_nt(a, b):
    return lax.dot_general(a, b, (((1,), (1,)), ((), ())), preferred_element_type=F32)


def _dot_tn(a, b):
    return lax.dot_general(a, b, (((0,), (0,)), ((), ())), preferred_element_type=F32)


def _silu(x):
    return x * (1.0 / (1.0 + jnp.exp(-x)))


def _log_sigmoid(z):
    return -(jnp.maximum(-z, 0.0) + jnp.log(1.0 + jnp.exp(-jnp.abs(z))))


def _rms(x, g):
    return x * lax.rsqrt(jnp.mean(x * x, axis=-1, keepdims=True) + EPS) * g


def _pack_rows(y):
    half = y.shape[1] // 2
    lo = lax.bitcast_convert_type(y[:, :half].astype(BF16).astype(F32), jnp.uint32)
    hi = lax.bitcast_convert_type(y[:, half:].astype(BF16).astype(F32), jnp.uint32)
    return lax.shift_right_logical(lo, jnp.uint32(16)) | hi


def _packed_shape(m, d):
    return (m, d // 2), jnp.uint32


def _unpack_rows(w):
    lo = lax.bitcast_convert_type(lax.shift_left(w, jnp.uint32(16)), F32)
    hi = lax.bitcast_convert_type(w & jnp.uint32(0xFFFF0000), F32)
    return jnp.concatenate([lo, hi], axis=1)


def _ada_kernel(c_ref, w_ref, b_ref, o_ref):
    o_ref[0] = _dot3(_silu(c_ref[...]), w_ref[0]) + b_ref[0]


def _ada(c_rows, w, b):
    nl, d, n = w.shape
    tn = _tile(n, 1536)
    return pl.pallas_call(
        _ada_kernel,
        grid=(nl, n // tn),
        in_specs=[pl.BlockSpec((8, d), lambda l, j: (0, 0)),
                  pl.BlockSpec((1, d, tn), lambda l, j: (l, 0, j)),
                  pl.BlockSpec((1, 1, tn), lambda l, j: (l, 0, j))],
        out_specs=pl.BlockSpec((1, 8, tn), lambda l, j: (l, 0, j)),
        out_shape=jax.ShapeDtypeStruct((nl, 8, n), F32),
        compiler_params=pltpu.CompilerParams(dimension_semantics=("parallel", "parallel"),
                                             vmem_limit_bytes=_vmem(2 * d * tn * 4 * 3)),
        name="ada_mod",
    )(c_rows, w, b.reshape(nl, 1, n))


def _nmm_kernel(*refs, has_small, rope_tiles, heads_per_tile):
    x_ref, g_ref, sc_ref, sh_ref, w_ref = refs[:5]
    k = 5
    if has_small:
        w2_ref = refs[k]; k += 1
    if rope_tiles:
        cs_ref, sn_ref = refs[k], refs[k + 1]; k += 2
    o_ref = refs[k]; k += 1
    if has_small:
        o2_ref = refs[k]; k += 1
    hn_ref = refs[k]
    j = pl.program_id(1)

    @pl.when(j == 0)
    def _():
        y = _rms(x_ref[...], g_ref[...]) * sc_ref[0] + sh_ref[0]
        hn_ref[...] = y.astype(BF16)
        if has_small:
            o2_ref[...] = _dot3(y, w2_ref[...])

    acc = _dot(hn_ref[...], w_ref[...])
    if rope_tiles:
        @pl.when(j < rope_tiles)
        def _():
            cs = cs_ref[...]
            sn = sn_ref[...]
            half = cs.shape[1] // 2
            outs = []
            for h in range(heads_per_tile):
                r = acc[:, h * 2 * half:(h + 1) * 2 * half]
                rs = jnp.concatenate([r[:, half:], r[:, :half]], axis=1)
                outs.append(r * cs + rs * sn)
            o_ref[...] = jnp.concatenate(outs, axis=1).astype(o_ref.dtype)

        @pl.when(j >= rope_tiles)
        def _():
            o_ref[...] = acc.astype(o_ref.dtype)
    else:
        o_ref[...] = acc.astype(o_ref.dtype)


def _nmm(x, g, sc, sh, w, *, rows_per_group, tn, w_small=None, rope=None, rope_cols=0):
    m, d = x.shape
    n = w.shape[1]
    tm = _tile(rows_per_group, 1024)
    gpt = rows_per_group // tm
    grp = lambda i, j: (i // gpt, 0, 0)
    in_specs = [pl.BlockSpec((tm, d), lambda i, j: (i, 0)),
                pl.BlockSpec((1, d), lambda i, j: (0, 0)),
                pl.BlockSpec((1, 1, d), grp),
                pl.BlockSpec((1, 1, d), grp),
                pl.BlockSpec((d, tn), lambda i, j: (0, j))]
    args = [x, g.reshape(1, d), sc, sh, w]
    out_specs = [pl.BlockSpec((tm, tn), lambda i, j: (i, j))]
    out_shape = [jax.ShapeDtypeStruct((m, n), BF16)]
    if w_small is not None:
        in_specs.append(pl.BlockSpec((d, EV_SMALL), lambda i, j: (0, 0)))
        args.append(w_small)
        out_specs.append(pl.BlockSpec((tm, EV_SMALL), lambda i, j: (i, 0)))
        out_shape.append(jax.ShapeDtypeStruct((m, EV_SMALL), F32))
    rope_tiles = heads_per_tile = 0
    if rope is not None:
        cs, sn = rope
        t_rows, hd = cs.shape
        rope_tiles, heads_per_tile = rope_cols // tn, tn // hd
        tpb = t_rows // tm
        in_specs += [pl.BlockSpec((tm, hd), lambda i, j: (i % tpb, 0)),
                     pl.BlockSpec((tm, hd), lambda i, j: (i % tpb, 0))]
        args += [cs, sn]
    kern = functools.partial(_nmm_kernel, has_small=w_small is not None, rope_tiles=rope_tiles,
                             heads_per_tile=heads_per_tile)
    res = pl.pallas_call(
        kern,
        grid=(m // tm, n // tn),
        in_specs=in_specs,
        out_specs=out_specs,
        out_shape=out_shape,
        scratch_shapes=[pltpu.VMEM((tm, d), BF16)],
        compiler_params=pltpu.CompilerParams(
            dimension_semantics=("parallel", "arbitrary"),
            vmem_limit_bytes=_vmem(2 * tm * d * 4 + tm * d * 2 + 2 * d * tn * 2 + 5 * tm * tn * 4)),
        name="norm_proj",
    )(*args)
    return res if w_small is not None else res[0]


def _gla_kernel(mf_q, mf_k, mf_v, sm_f, mb_q, mb_k, mb_v, sm_b, wg_ref, bg_ref, s0f_ref, s0b_ref,
                of_ref, ob_ref, sf_ref, sb_ref, st_ref, *, n_sub):
    s = pl.program_id(1)
    c = GLA_CHUNK

    @pl.when(s == 0)
    def _():
        st_ref[0] = s0f_ref[0]
        st_ref[1] = s0b_ref[0]

    row = lax.broadcasted_iota(jnp.int32, (c, c), 0)
    col = lax.broadcasted_iota(jnp.int32, (c, c), 1)
    ones_col = jnp.ones((c, GLA_DK), BF16)
    for d, (q_ref, k_ref, v_ref, sm_ref, o_ref) in enumerate(
            ((mf_q, mf_k, mf_v, sm_f, of_ref), (mb_q, mb_k, mb_v, sm_b, ob_ref))):
        keep = (col <= row) if d == 0 else (col >= row)
        tri = jnp.where(keep, 1.0, 0.0).astype(BF16)
        z = _dot3(sm_ref[...], wg_ref[d]) + bg_ref[d]
        la_all = _log_sigmoid(z) * (1.0 / GLA_GATE_NORM)
        order = range(n_sub) if d == 0 else range(n_sub - 1, -1, -1)
        for ci in order:
            r0 = ci * c
            for h in range(GLA_HEADS):
                la = la_all[r0:r0 + c, h * GLA_DK:(h + 1) * GLA_DK]
                lh, ll = _split_bf16(la)
                b = _dot(tri, lh) + _dot(tri, ll)
                tot = _dot_tn(lh, ones_col) + _dot_tn(ll, ones_col)
                b_last = b[c - 1:c, :] if d == 0 else b[0:1, :]
                q = q_ref[r0:r0 + c, h * GLA_DK:(h + 1) * GLA_DK].astype(F32) * (GLA_DK ** -0.5)
                k = k_ref[r0:r0 + c, h * GLA_DK:(h + 1) * GLA_DK].astype(F32)
                v = v_ref[r0:r0 + c, h * GLA_DV:(h + 1) * GLA_DV]
                qe = (q * jnp.exp(b)).astype(BF16)
                kd = (k * jnp.exp(-b)).astype(BF16)
                kl = (k * jnp.exp(b_last - b)).astype(BF16)
                att = jnp.where(keep, _dot_nt(qe, kd), 0.0).astype(BF16)
                st = st_ref[d, h]
                o = _dot(att, v) + _dot(qe, st.astype(BF16))
                o_ref[r0:r0 + c, h * GLA_DV:(h + 1) * GLA_DV] = o.astype(o_ref.dtype)
                dec = jnp.exp(tot)
                st_ref[d, h] = jnp.concatenate([dec, dec], axis=1) * st + _dot_tn(kl, v)

    @pl.when(s == pl.num_programs(1) - 1)
    def _():
        sf_ref[0] = st_ref[0]
        sb_ref[0] = st_ref[1]


def _gla(main, small, wg, bg, s0f, s0b, *, batch, t):
    rows = _tile(t, 256, GLA_CHUNK)
    ns = t // rows
    hv, hk = GLA_HEADS * GLA_DV, GLA_HEADS * GLA_DK
    fwd = lambda blk: (lambda b, s: (b * ns + s, blk))
    bwd = lambda blk: (lambda b, s: (b * ns + ns - 1 - s, blk))
    st_spec = pl.BlockSpec((1, GLA_HEADS, GLA_DK, GLA_DV), lambda b, s: (b, 0, 0, 0))
    st_shape = jax.ShapeDtypeStruct((batch, GLA_HEADS, GLA_DK, GLA_DV), F32)

    def io(m):
        return [pl.BlockSpec((rows, hk), m(EV_GQ // hk)), pl.BlockSpec((rows, hk), m(EV_GK // hk)),
                pl.BlockSpec((rows, hv), m(EV_GV // hv)), pl.BlockSpec((rows, EV_SMALL), m(0))]

    return pl.pallas_call(
        functools.partial(_gla_kernel, n_sub=rows // GLA_CHUNK),
        grid=(batch, ns),
        in_specs=io(fwd) + io(bwd) + [
            pl.BlockSpec((2, EV_SMALL, hk), lambda b, s: (0, 0, 0)),
            pl.BlockSpec((2, 1, hk), lambda b, s: (0, 0, 0)),
            st_spec, st_spec],
        out_specs=[pl.BlockSpec((rows, hv), fwd(0)), pl.BlockSpec((rows, hv), bwd(0)), st_spec, st_spec],
        out_shape=[jax.ShapeDtypeStruct((batch * t, hv), BF16), jax.ShapeDtypeStruct((batch * t, hv), BF16),
                   st_shape, st_shape],
        scratch_shapes=[pltpu.VMEM((2, GLA_HEADS, GLA_DK, GLA_DV), F32)],
        compiler_params=pltpu.CompilerParams(dimension_semantics=("parallel", "arbitrary"),
                                             vmem_limit_bytes=_vmem(16 << 20)),
        name="gla_scan",
    )(main, main, main, small, main, main, main, small, wg, bg, s0f, s0b)


def _mla_q_kernel(x_ref, g_ref, w_ref, cs_ref, sn_ref, o_ref, hn_ref):
    @pl.when(pl.program_id(1) == 0)
    def _():
        hn_ref[...] = _rms(x_ref[...].astype(F32), g_ref[...]).astype(BF16)

    r = _dot(hn_ref[...], w_ref[0])
    qr = r[:, MLA_NOPE:MLA_NOPE + MLA_ROPE] * cs_ref[...] + r[:, MLA_NOPE + MLA_ROPE:] * sn_ref[...]
    o_ref[0, 0] = (jnp.concatenate([r[:, :MLA_NOPE], qr], axis=1) * MLA_Q_SCALE).astype(o_ref.dtype)


def _mla_q(main, g, w, cs, sn, *, t):
    m = main.shape[0]
    tm = _tile(t, 1024)
    tpb = t // tm
    return pl.pallas_call(
        _mla_q_kernel,
        grid=(m // tm, MLA_HEADS),
        in_specs=[pl.BlockSpec((tm, MLA_Q_RANK), lambda i, h: (i, EV_MQ // MLA_Q_RANK)),
                  pl.BlockSpec((1, MLA_Q_RANK), lambda i, h: (0, 0)),
                  pl.BlockSpec((1, MLA_Q_RANK, 2 * MLA_NOPE), lambda i, h: (h, 0, 0)),
                  pl.BlockSpec((tm, MLA_ROPE), lambda i, h: (i % tpb, 0)),
                  pl.BlockSpec((tm, MLA_ROPE), lambda i, h: (i % tpb, 0))],
        out_specs=pl.BlockSpec((1, 1, tm, MLA_DQK), lambda i, h: (i // tpb, h, i % tpb, 0)),
        out_shape=jax.ShapeDtypeStruct((m // t, MLA_HEADS, t, MLA_DQK), BF16),
        scratch_shapes=[pltpu.VMEM((tm, MLA_Q_RANK), BF16)],
        compiler_params=pltpu.CompilerParams(dimension_semantics=("parallel", "arbitrary")),
        name="mla_q_proj",
    )(main, g.reshape(1, -1), w, cs, sn)


def _mla_kv_kernel(x_ref, kr_ref, g_ref, w_ref, cs_ref, sn_ref, k_ref, v_ref, hn_ref):
    @pl.when(pl.program_id(1) == 0)
    def _():
        hn_ref[...] = _rms(x_ref[...].astype(F32), g_ref[...]).astype(BF16)

    r = _dot(hn_ref[...], w_ref[0])
    kr = kr_ref[...].astype(F32)
    krr = kr[:, :MLA_ROPE] * cs_ref[...] + kr[:, MLA_ROPE:2 * MLA_ROPE] * sn_ref[...]
    k_ref[0, 0] = jnp.concatenate([r[:, :MLA_NOPE], krr], axis=1).astype(k_ref.dtype)
    v_ref[0, 0] = r[:, MLA_NOPE:].astype(v_ref.dtype)


def _mla_kv(main, g, w, cs, sn, *, t):
    m = main.shape[0]
    tm = _tile(t, 1024)
    tpb = t // tm
    return pl.pallas_call(
        _mla_kv_kernel,
        grid=(m // tm, MLA_HEADS),
        in_specs=[pl.BlockSpec((tm, MLA_KV_RANK), lambda i, h: (i, EV_MKV // MLA_KV_RANK)),
                  pl.BlockSpec((tm, 256), lambda i, h: (i, EV_KR // 256)),
                  pl.BlockSpec((1, MLA_KV_RANK), lambda i, h: (0, 0)),
                  pl.BlockSpec((1, MLA_KV_RANK, MLA_NOPE + MLA_DV), lambda i, h: (h, 0, 0)),
                  pl.BlockSpec((tm, MLA_ROPE), lambda i, h: (i % tpb, 0)),
                  pl.BlockSpec((tm, MLA_ROPE), lambda i, h: (i % tpb, 0))],
        out_specs=[pl.BlockSpec((1, 1, tm, MLA_DQK), lambda i, h: (i // tpb, h, i % tpb, 0)),
                   pl.BlockSpec((1, 1, tm, MLA_DV), lambda i, h: (i // tpb, h, i % tpb, 0))],
        out_shape=[jax.ShapeDtypeStruct((m // t, MLA_HEADS, t, MLA_DQK), BF16),
                   jax.ShapeDtypeStruct((m // t, MLA_HEADS, t, MLA_DV), BF16)],
        scratch_shapes=[pltpu.VMEM((tm, MLA_KV_RANK), BF16)],
        compiler_params=pltpu.CompilerParams(dimension_semantics=("parallel", "arbitrary")),
        name="mla_kv_proj",
    )(main, main, g.reshape(1, -1), w, cs, sn)


ATTN_ROWS = 256


def _attn_kernel(*refs, n_main, has_extra):
    if has_extra:
        q_ref, k_ref, v_ref, kx_ref, vx_ref, o_ref, m_ref, acc_ref = refs
    else:
        q_ref, k_ref, v_ref, o_ref, m_ref, acc_ref = refs
    kv = pl.program_id(3)

    @pl.when(kv == 0)
    def _():
        m_ref[...] = jnp.full(m_ref.shape, -jnp.inf, F32)
        acc_ref[...] = jnp.zeros(acc_ref.shape, F32)

    tq = q_ref.shape[2]
    rc = min(ATTN_ROWS, tq)

    def step(k, v):
        v_ones = jnp.concatenate([v, jnp.ones(v.shape, v.dtype)], axis=1)
        for r in range(tq // rc):
            rows = slice(r * rc, (r + 1) * rc)
            s = _dot_nt(q_ref[0, 0, rows, :], k)
            m_old = m_ref[rows, :]
            m_new = jnp.maximum(m_old, jnp.max(s, axis=-1, keepdims=True))
            p = jnp.exp2(s - m_new).astype(BF16)
            acc_ref[rows, :] = jnp.exp2(m_old - m_new) * acc_ref[rows, :] + _dot(p, v_ones)
            m_ref[rows, :] = m_new

    pl.when(kv < n_main)(lambda: step(k_ref[0, 0], v_ref[0, 0]))
    if has_extra:
        pl.when(kv == n_main)(lambda: step(kx_ref[0, 0], vx_ref[0, 0]))

    @pl.when(kv == pl.num_programs(3) - 1)
    def _():
        acc = acc_ref[...]
        o_ref[...] = (acc[:, :MLA_DV] / acc[:, MLA_DV:]).astype(o_ref.dtype)


def _attn(q, k, v, kx=None, vx=None, *, tq_pref=1024, tk_pref=4096):
    b, h, tq_all, dqk = q.shape
    tk_all = k.shape[2]
    tq, tk = _tile(tq_all, tq_pref), _tile(tk_all, tk_pref)
    nq, n_main = tq_all // tq, tk_all // tk
    main = lambda bi, hi, i, j: (bi, hi, jnp.minimum(j, n_main - 1), 0)
    in_specs = [pl.BlockSpec((1, 1, tq, dqk), lambda bi, hi, i, j: (bi, hi, i, 0)),
                pl.BlockSpec((1, 1, tk, dqk), main), pl.BlockSpec((1, 1, tk, MLA_DV), main)]
    args = [q, k, v]
    if kx is not None:
        tx = kx.shape[2]
        in_specs += [pl.BlockSpec((1, 1, tx, dqk), lambda bi, hi, i, j: (bi, hi, 0, 0)),
                     pl.BlockSpec((1, 1, tx, MLA_DV), lambda bi, hi, i, j: (bi, hi, 0, 0))]
        args += [kx, vx]
    return pl.pallas_call(
        functools.partial(_attn_kernel, n_main=n_main, has_extra=kx is not None),
        grid=(b, h, nq, n_main + (kx is not None)),
        in_specs=in_specs,
        out_specs=pl.BlockSpec((tq, MLA_DV), lambda bi, hi, i, j: (bi * nq + i, hi)),
        out_shape=jax.ShapeDtypeStruct((b * tq_all, h * MLA_DV), BF16),
        scratch_shapes=[pltpu.VMEM((tq, 1), F32), pltpu.VMEM((tq, 2 * MLA_DV), F32)],
        compiler_params=pltpu.CompilerParams(
            dimension_semantics=("parallel", "parallel", "parallel", "arbitrary"),
            vmem_limit_bytes=_vmem(5 * min(ATTN_ROWS, tq) * tk * 4 + 8 * tk * 256 * 2)),
        name="mla_attention",
    )(*args)


def _mixout_kernel(*refs, n_heads, dv, center, has_mla):
    of_ref, ob_ref, gate_ref, g_ref = refs[:4]
    k = 4
    if has_mla:
        mla_ref = refs[k]; k += 1
    w_ref, x_ref, ga_ref, o_ref = refs[k:k + 4]
    o = of_ref[...].astype(F32) + ob_ref[...].astype(F32)
    g = g_ref[...]
    parts = []
    for h in range(n_heads):
        oh = o[:, h * dv:(h + 1) * dv]
        if center:
            oh = oh - jnp.mean(oh, axis=-1, keepdims=True)
        parts.append(oh * lax.rsqrt(jnp.mean(oh * oh, axis=-1, keepdims=True) + EPS) * g)
    lat = jnp.concatenate(parts, axis=1) * _silu(gate_ref[...].astype(F32))
    lhs = lat.astype(BF16)
    if has_mla:
        lhs = jnp.concatenate([lhs, mla_ref[...]], axis=1)
    o_ref[...] = x_ref[...] + ga_ref[0] * _dot(lhs, w_ref[...])


def _mixout(o_f, o_b, main, gate_blk, g, mla, w, x, ga, *, rows_per_group, n_heads, dv, center):
    m, d = x.shape
    hw = n_heads * dv
    tm = _tile(rows_per_group, 512)
    gpt = rows_per_group // tm
    in_specs = [pl.BlockSpec((tm, hw), lambda i: (i, 0)), pl.BlockSpec((tm, hw), lambda i: (i, 0)),
                pl.BlockSpec((tm, hw), lambda i: (i, gate_blk)), pl.BlockSpec((1, dv), lambda i: (0, 0))]
    args = [o_f, o_b, main, g.reshape(1, dv)]
    if mla is not None:
        in_specs.append(pl.BlockSpec((tm, mla.shape[1]), lambda i: (i, 0)))
        args.append(mla)
    in_specs += [pl.BlockSpec(w.shape, lambda i: (0, 0)), pl.BlockSpec((tm, d), lambda i: (i, 0)),
                 pl.BlockSpec((1, 1, d), lambda i: (i // gpt, 0, 0))]
    args += [w, x, ga]
    return pl.pallas_call(
        functools.partial(_mixout_kernel, n_heads=n_heads, dv=dv, center=center, has_mla=mla is not None),
        grid=(m // tm,),
        in_specs=in_specs,
        out_specs=pl.BlockSpec((tm, d), lambda i: (i, 0)),
        out_shape=jax.ShapeDtypeStruct((m, d), F32),
        compiler_params=pltpu.CompilerParams(
            dimension_semantics=("parallel",),
            vmem_limit_bytes=_vmem(2 * w.size * 2 + 4 * tm * d * 4 + 8 * tm * hw * 2 + 6 * tm * hw * 4)),
        name="mixer_out",
    )(*args)


def _ffn_kernel(te_ref, nv_ref, *refs, prenorm):
    if prenorm:
        x_ref, g_ref, sc_ref, sh_ref, wg_ref, wu_ref, wd_ref, gf_ref, o_ref, hn_ref, acc_ref = refs
    else:
        x_ref, wg_ref, wu_ref, wd_ref, o_ref, hn_ref, acc_ref = refs
    i, j = pl.program_id(0), pl.program_id(1)
    valid = i < nv_ref[0]

    @pl.when(jnp.logical_and(valid, j == 0))
    def _():
        if prenorm:
            y = _rms(x_ref[...], g_ref[...]) * sc_ref[0] + sh_ref[0]
        else:
            y = _unpack_rows(x_ref[...])
        hn_ref[...] = y.astype(BF16)
        acc_ref[...] = jnp.zeros(acc_ref.shape, F32)

    @pl.when(valid)
    def _():
        hn = hn_ref[...]
        a = (_silu(_dot(hn, wg_ref[0])) * _dot(hn, wu_ref[0])).astype(BF16)
        acc_ref[...] += _dot(a, wd_ref[0])

    last = j == pl.num_programs(1) - 1

    @pl.when(jnp.logical_and(valid, last))
    def _():
        if prenorm:
            o_ref[...] = x_ref[...] + gf_ref[0] * acc_ref[...]
        else:
            o_ref[...] = _pack_rows(acc_ref[...])

    @pl.when(jnp.logical_and(jnp.logical_not(valid), last))
    def _():
        o_ref[...] = jnp.zeros(o_ref.shape, o_ref.dtype)


def _ffn(x, wg, wu, wd, tile_expert, n_valid, *, tm, norm=None):
    p = x.shape[0]
    d = wg.shape[1]
    f = wg.shape[2]
    tf = _tile(f, 512)
    nt = p // tm
    wmap_in = lambda i, j, te, nv: (te[i], 0, j)
    wmap_out = lambda i, j, te, nv: (te[i], j, 0)
    row = lambda i, j, te, nv: (i, 0)
    in_specs = [pl.BlockSpec((tm, x.shape[1]), lambda i, j, te, nv: (jnp.minimum(i, nv[0] - 1), 0))]
    args = [x]
    if norm is not None:
        g, sc, sh, gf, rpg = norm
        gpt = rpg // tm
        grp = lambda i, j, te, nv: (i // gpt, 0, 0)
        in_specs += [pl.BlockSpec((1, d), lambda i, j, te, nv: (0, 0)), pl.BlockSpec((1, 1, d), grp),
                     pl.BlockSpec((1, 1, d), grp)]
        args += [g.reshape(1, d), sc, sh]
    in_specs += [pl.BlockSpec((1, d, tf), wmap_in), pl.BlockSpec((1, d, tf), wmap_in),
                 pl.BlockSpec((1, tf, d), wmap_out)]
    args += [wg, wu, wd]
    if norm is not None:
        in_specs.append(pl.BlockSpec((1, 1, d), grp))
        args.append(gf)
    return pl.pallas_call(
        functools.partial(_ffn_kernel, prenorm=norm is not None),
        grid_spec=pltpu.PrefetchScalarGridSpec(
            num_scalar_prefetch=2,
            grid=(nt, f // tf),
            in_specs=in_specs,
            out_specs=pl.BlockSpec((tm, x.shape[1]), row),
            scratch_shapes=[pltpu.VMEM((tm, d), BF16), pltpu.VMEM((tm, d), F32)]),
        out_shape=jax.ShapeDtypeStruct(x.shape, x.dtype),
        compiler_params=pltpu.CompilerParams(
            dimension_semantics=("parallel", "arbitrary"),
            vmem_limit_bytes=_vmem(4 * tm * x.shape[1] * 4 + tm * d * 6 + 6 * d * tf * 2 + 5 * tm * tf * 4)),
        name="swiglu_ffn",
    )(tile_expert, n_valid, *args)


def _ret_kernel(qf_ref, kf_ref, vf_ref, qb_ref, kb_ref, vb_ref, dm_ref, qd_ref, kd_ref, cd_ref,
                s0f_ref, s0b_ref, of_ref, ob_ref, sf_ref, sb_ref, st_ref, *, n_sub):
    s = pl.program_id(1)
    c = RET_CHUNK

    @pl.when(s == 0)
    def _():
        st_ref[0] = s0f_ref[0]
        st_ref[1] = s0b_ref[0]

    for d, (q_ref, k_ref, v_ref, o_ref) in enumerate(((qf_ref, kf_ref, vf_ref, of_ref),
                                                       (qb_ref, kb_ref, vb_ref, ob_ref))):
        order = range(n_sub) if d == 0 else range(n_sub - 1, -1, -1)
        for ci in order:
            r0 = ci * c
            for h in range(RET_HEADS):
                q = q_ref[r0:r0 + c, h * RET_DK:(h + 1) * RET_DK]
                k = (k_ref[r0:r0 + c, h * RET_DK:(h + 1) * RET_DK].astype(F32) * (RET_DK ** -0.5)).astype(BF16)
                v = v_ref[r0:r0 + c, h * RET_DV:(h + 1) * RET_DV]
                att = (_dot_nt(q, k) * dm_ref[d, h]).astype(BF16)
                st = st_ref[d, h]
                o = _dot(att, v) + qd_ref[d, h] * _dot(q, st.astype(BF16))
                o_ref[r0:r0 + c, h * RET_DV:(h + 1) * RET_DV] = o.astype(o_ref.dtype)
                vk = (v.astype(F32) * kd_ref[d, h]).astype(BF16)
                st_ref[d, h] = cd_ref[d, h] * st + _dot_tn(k, vk)

    @pl.when(s == pl.num_programs(1) - 1)
    def _():
        sf_ref[0] = st_ref[0]
        sb_ref[0] = st_ref[1]


def _ret(main, tables, s0f, s0b, *, batch, t):
    rows = _tile(t, 256, RET_CHUNK)
    ns = t // rows
    hk, hv = RET_HEADS * RET_DK, RET_HEADS * RET_DV
    fwd = lambda blk: (lambda b, s: (b * ns + s, blk))
    bwd = lambda blk: (lambda b, s: (b * ns + ns - 1 - s, blk))
    st_spec = pl.BlockSpec((1, RET_HEADS, RET_DK, RET_DV), lambda b, s: (b, 0, 0, 0))
    st_shape = jax.ShapeDtypeStruct((batch, RET_HEADS, RET_DK, RET_DV), F32)
    io = lambda m: [pl.BlockSpec((rows, hk), m(0)), pl.BlockSpec((rows, hk), m(1)), pl.BlockSpec((rows, hv), m(2))]
    full = lambda a: pl.BlockSpec(a.shape, lambda b, s: (0,) * a.ndim)
    return pl.pallas_call(
        functools.partial(_ret_kernel, n_sub=rows // RET_CHUNK),
        grid=(batch, ns),
        in_specs=io(fwd) + io(bwd) + [full(a) for a in tables] + [st_spec, st_spec],
        out_specs=[pl.BlockSpec((rows, hv), fwd(0)), pl.BlockSpec((rows, hv), bwd(0)), st_spec, st_spec],
        out_shape=[jax.ShapeDtypeStruct((batch * t, hv), BF16), jax.ShapeDtypeStruct((batch * t, hv), BF16),
                   st_shape, st_shape],
        scratch_shapes=[pltpu.VMEM((2, RET_HEADS, RET_DK, RET_DV), F32)],
        compiler_params=pltpu.CompilerParams(dimension_semantics=("parallel", "arbitrary"),
                                             vmem_limit_bytes=_vmem(40 << 20)),
        name="retention_scan",
    )(main, main, main, main, main, main, *tables, s0f, s0b)


def _router_kernel(x_ref, g_ref, sc_ref, sh_ref, wr_ref, meta_ref, gate_ref, cnt_ref, run_ref):
    i = pl.program_id(0)

    @pl.when(i == 0)
    def _():
        run_ref[...] = jnp.zeros(run_ref.shape, F32)

    y = _rms(x_ref[...], g_ref[...]) * sc_ref[0] + sh_ref[0]
    tm = y.shape[0]
    lt = jnp.transpose(_dot3(y, wr_ref[...]))[:N_EXPERTS, :]
    eid = lax.broadcasted_iota(jnp.int32, lt.shape, 0).astype(F32)
    v1 = jnp.max(lt, axis=0, keepdims=True)
    e1 = jnp.min(jnp.where(lt == v1, eid, float(N_EXPERTS)), axis=0, keepdims=True)
    lt2 = jnp.where(eid == e1, -jnp.inf, lt)
    v2 = jnp.max(lt2, axis=0, keepdims=True)
    e2 = jnp.min(jnp.where(lt2 == v2, eid, float(N_EXPERTS)), axis=0, keepdims=True)
    ex = jnp.exp(v2 - v1)
    w1 = 1.0 / (1.0 + ex)
    w2 = ex / (1.0 + ex)
    oh1 = jnp.where(eid == e1, 1.0, 0.0)
    oh2 = jnp.where(eid == e2, 1.0, 0.0)
    mem = oh1 + oh2
    r = lax.broadcasted_iota(jnp.int32, (tm, tm), 0)
    cidx = lax.broadcasted_iota(jnp.int32, (tm, tm), 1)
    before = jnp.where(r < cidx, 1.0, 0.0).astype(BF16)
    excl = _dot(mem.astype(BF16), before) + run_ref[...][:, :1]
    rk1 = jnp.sum(oh1 * excl, axis=0, keepdims=True)
    rk2 = jnp.sum(oh2 * excl, axis=0, keepdims=True)
    run_ref[...] = run_ref[...] + jnp.sum(mem, axis=1, keepdims=True)
    zero = jnp.zeros_like(e1)
    meta_ref[...] = jnp.concatenate([e1, e2, rk1, rk2, w1, w2, zero, zero], axis=0)
    wpad = jnp.concatenate([w1, w2, jnp.zeros((6, tm), F32)], axis=0)
    gate_ref[...] = jnp.transpose(jnp.concatenate([wpad] * 16, axis=0))
    cnt_ref[...] = run_ref[...]


def _router(x, g, sc, sh, wr, *, rows_per_group):
    m, d = x.shape
    tm = _tile(rows_per_group, ROUTER_ROWS)
    gpt = rows_per_group // tm
    grp = lambda i: (i // gpt, 0, 0)
    return pl.pallas_call(
        _router_kernel,
        grid=(m // tm,),
        in_specs=[pl.BlockSpec((tm, d), lambda i: (i, 0)), pl.BlockSpec((1, d), lambda i: (0, 0)),
                  pl.BlockSpec((1, 1, d), grp), pl.BlockSpec((1, 1, d), grp),
                  pl.BlockSpec((d, 128), lambda i: (0, 0))],
        out_specs=[pl.BlockSpec((8, tm), lambda i: (0, i)),
                   pl.BlockSpec((tm, 128), lambda i: (i, 0)), pl.BlockSpec((N_EXPERTS, 128), lambda i: (0, 0))],
        out_shape=[jax.ShapeDtypeStruct((8, m), F32),
                   jax.ShapeDtypeStruct((m, 128), F32), jax.ShapeDtypeStruct((N_EXPERTS, 128), F32)],
        scratch_shapes=[pltpu.VMEM((N_EXPERTS, 128), F32)],
        compiler_params=pltpu.CompilerParams(dimension_semantics=("arbitrary",),
                                             vmem_limit_bytes=_vmem(6 * tm * d * 4)),
        name="moe_router",
    )(x, g.reshape(1, d), sc, sh, wr)


def _scatter_kernel(zs_ref, pos_ref, x_ref, g_ref, sc_ref, sh_ref, o_ref, hbuf, sem, *, group_rows, n_tiles):
    tm = hbuf.shape[0]

    def zero_tile(start):
        start = pl.multiple_of(start, 8)
        for c in range(group_rows // tm):
            cp = pltpu.make_async_copy(hbuf, o_ref.at[pl.ds(start + c * tm, tm)], sem.at[0])
            cp.start()
            cp.wait()

    @pl.when(pl.program_id(0) == 0)
    def _():
        hbuf[...] = jnp.zeros(hbuf.shape, hbuf.dtype)
        for e in range(N_EXPERTS):
            pl.when(zs_ref[e] >= 0)(functools.partial(zero_tile, zs_ref[e]))
            tail = zs_ref[N_EXPERTS] + e
            pl.when(tail < n_tiles)(functools.partial(zero_tile, tail * group_rows))

    hbuf[...] = _pack_rows(_rms(x_ref[...], g_ref[...]) * sc_ref[0] + sh_ref[0])

    def copies(t):
        return (pltpu.make_async_copy(hbuf.at[t], o_ref.at[pos_ref[0, 0, t]], sem.at[0]),
                pltpu.make_async_copy(hbuf.at[t], o_ref.at[pos_ref[0, 1, t]], sem.at[1]))

    def issue(t, carry):
        for cp in copies(t):
            cp.start()
        return carry

    lax.fori_loop(0, tm, issue, 0)

    def drain(t, carry):
        for cp in copies(t):
            cp.wait()
        return carry

    lax.fori_loop(0, tm, drain, 0)


def _scatter_rows(x, g, sc, sh, pos, zero_plan, *, n_tiles, group_rows, rows_per_group):
    m, d = x.shape
    nt, _, tm = pos.shape
    assert group_rows % tm == 0
    gpt = rows_per_group // tm
    grp = lambda i, zs: (i // gpt, 0, 0)
    pshape, pdtype = _packed_shape(tm, d)
    return pl.pallas_call(
        functools.partial(_scatter_kernel, group_rows=group_rows, n_tiles=n_tiles),
        grid_spec=pltpu.PrefetchScalarGridSpec(
            num_scalar_prefetch=1,
            grid=(nt,),
            in_specs=[pl.BlockSpec((1, 2, tm), lambda i, zs: (i, 0, 0), memory_space=pltpu.SMEM),
                      pl.BlockSpec((tm, d), lambda i, zs: (i, 0)), pl.BlockSpec((1, d), lambda i, zs: (0, 0)),
                      pl.BlockSpec((1, 1, d), grp), pl.BlockSpec((1, 1, d), grp)],
            out_specs=pl.BlockSpec(memory_space=pl.ANY),
            scratch_shapes=[pltpu.VMEM(pshape, pdtype), pltpu.SemaphoreType.DMA((2,))]),
        out_shape=jax.ShapeDtypeStruct((n_tiles * group_rows, pshape[1]), pdtype),
        compiler_params=pltpu.CompilerParams(dimension_semantics=("arbitrary",),
                                             vmem_limit_bytes=_vmem(6 * tm * d * 4)),
        name="moe_scatter",
    )(zero_plan, pos, x, g.reshape(1, d), sc, sh)


def _combine_kernel(pos_ref, y_ref, x_ref, gate_ref, gf_ref, gn_ref, o_ref, buf_ref, sem):
    tm = pos_ref.shape[2]

    def copies(t):
        return (pltpu.make_async_copy(y_ref.at[pos_ref[0, 0, t]], buf_ref.at[0, t], sem.at[0]),
                pltpu.make_async_copy(y_ref.at[pos_ref[0, 1, t]], buf_ref.at[1, t], sem.at[1]))

    def issue(t, carry):
        for cp in copies(t):
            cp.start()
        return carry

    lax.fori_loop(0, tm, issue, 0)

    def drain(t, carry):
        for cp in copies(t):
            cp.wait()
        return carry

    lax.fori_loop(0, tm, drain, 0)
    w = gate_ref[...]
    moe = w[:, 0:1] * _unpack_rows(buf_ref[0]) + w[:, 1:2] * _unpack_rows(buf_ref[1])
    o_ref[...] = _rms(x_ref[...] + gf_ref[0] * moe, gn_ref[...])


def _combine(y, pos, x, gate, gf, gn, *, rows_per_group):
    m, d = x.shape
    nt, _, tm = pos.shape
    gpt = rows_per_group // tm
    return pl.pallas_call(
        _combine_kernel,
        grid=(nt,),
        in_specs=[pl.BlockSpec((1, 2, tm), lambda i: (i, 0, 0), memory_space=pltpu.SMEM),
                  pl.BlockSpec(memory_space=pl.ANY),
                  pl.BlockSpec((tm, d), lambda i: (i, 0)), pl.BlockSpec((tm, 128), lambda i: (i, 0)),
                  pl.BlockSpec((1, 1, d), lambda i: (i // gpt, 0, 0)), pl.BlockSpec((1, d), lambda i: (0, 0))],
        out_specs=pl.BlockSpec((tm, d), lambda i: (i, 0)),
        out_shape=jax.ShapeDtypeStruct((m, d), F32),
        scratch_shapes=[pltpu.VMEM((2, tm, y.shape[1]), y.dtype), pltpu.SemaphoreType.DMA((2,))],
        compiler_params=pltpu.CompilerParams(dimension_semantics=("arbitrary",),
                                             vmem_limit_bytes=_vmem(8 * tm * d * 4)),
        name="moe_combine",
    )(pos, y, x, gate, gf, gn.reshape(1, d))


def _axial_angles(rows, dim):
    row = jnp.repeat(jnp.arange(rows, dtype=F32), GRID_W)
    col = jnp.tile(jnp.arange(GRID_W, dtype=F32), rows)
    half = dim // 2
    inv = 1.0 / (ROPE_BASE ** (jnp.arange(0, half, 2, dtype=F32) / half))
    return jnp.concatenate([row[:, None] * inv, col[:, None] * inv], axis=-1)


def _rope_tables(rows, dim, t_ctx):
    ang = _axial_angles(rows, dim)
    cs = jnp.concatenate([jnp.cos(ang), jnp.cos(ang)], axis=1)
    sn = jnp.concatenate([-jnp.sin(ang), jnp.sin(ang)], axis=1)
    return (cs, sn), (jnp.ones((t_ctx, dim), F32), jnp.zeros((t_ctx, dim), F32))


def _evens_odds(w):
    pairs = w.reshape(w.shape[:-1] + (w.shape[-1] // 2, 2))
    return pairs[..., 0], pairs[..., 1]


def _even_weights(w_in, w_g2, b_g2, w_uq, w_ukv):
    d = w_in.shape[0]
    sizes = (512, 512, 1024, 1024, 16, 16, MLA_Q_RANK, MLA_KV_RANK, MLA_ROPE)
    offs = np.cumsum((0,) + sizes)
    gq, gk, gv, gr, gaf, gab, mq, mkv, mkr = [w_in[:, offs[i]:offs[i + 1]] for i in range(9)]
    kr_e, kr_o = _evens_odds(mkr)
    w_main = jnp.concatenate([gv, gr, gq, gk, mq, kr_e, kr_o, kr_o, kr_e, jnp.zeros((d, 128), F32), mkv],
                             axis=1).astype(BF16)
    w_small = jnp.concatenate([gaf, gab, jnp.zeros((d, EV_SMALL - 2 * GLA_RANK), F32)], axis=1)
    wg = jnp.zeros((2, EV_SMALL, GLA_HEADS * GLA_DK), F32)
    wg = wg.at[0, :GLA_RANK].set(w_g2[0]).at[1, GLA_RANK:2 * GLA_RANK].set(w_g2[1])
    bg = b_g2.reshape(2, 1, -1)
    wq = w_uq.reshape(MLA_Q_RANK, MLA_HEADS, MLA_DQK)
    qr_e, qr_o = _evens_odds(wq[:, :, MLA_NOPE:])
    wq = jnp.concatenate([wq[:, :, :MLA_NOPE], qr_e, qr_o, qr_o, qr_e], axis=2)
    wq = jnp.transpose(wq, (1, 0, 2)).astype(BF16)
    wkv = jnp.transpose(w_ukv.reshape(MLA_KV_RANK, MLA_HEADS, MLA_NOPE + MLA_DV), (1, 0, 2)).astype(BF16)
    return w_main, w_small, wg, bg, wq, wkv


def _perm_kernel(w_ref, p_ref, o_ref):
    o_ref[...] = _dot(w_ref[...].astype(BF16), p_ref[...]).astype(o_ref.dtype)


def _odd_weights(w_in):
    d, n = w_in.shape
    hd = RET_DK
    n_perm = 2 * RET_HEADS
    src = np.concatenate([np.arange(0, hd, 2), np.arange(1, hd, 2)])
    perm = np.zeros((2, hd, hd), np.float32)
    perm[0, src, np.arange(hd)] = 1.0
    perm[1] = np.eye(hd, dtype=np.float32)
    return pl.pallas_call(
        _perm_kernel,
        grid=(n // hd,),
        in_specs=[pl.BlockSpec((d, hd), lambda j: (0, j)),
                  pl.BlockSpec((None, hd, hd), lambda j: (jnp.where(j < n_perm, 0, 1), 0, 0))],
        out_specs=pl.BlockSpec((d, hd), lambda j: (0, j)),
        out_shape=jax.ShapeDtypeStruct((d, n), BF16),
        compiler_params=pltpu.CompilerParams(dimension_semantics=("parallel",)),
        name="weight_reorder",
    )(w_in, jnp.asarray(perm, BF16))


def _ret_tables(log_decay):
    lg = -jnp.exp(log_decay.astype(F32))
    c = RET_CHUNK
    pos = jnp.arange(c, dtype=F32)
    diff = pos[:, None] - pos[None, :]
    lgd = lg[:, :, None, None]
    fmask = jnp.where(diff >= 0, jnp.exp(lgd * jnp.maximum(diff, 0.0)), 0.0)
    bmask = jnp.where(diff <= 0, jnp.exp(lgd * jnp.maximum(-diff, 0.0)), 0.0)
    dm = jnp.stack([fmask[0], bmask[1]])
    lgc = lg[:, :, None, None]
    qd = jnp.stack([jnp.exp(lgc[0] * (pos + 1.0)[None, :, None]), jnp.exp(lgc[1] * (c - pos)[None, :, None])])
    kd = jnp.stack([jnp.exp(lgc[0] * (c - 1.0 - pos)[None, :, None]), jnp.exp(lgc[1] * pos[None, :, None])])
    cd = jnp.exp(lgc * c)
    return dm, qd, kd, cd


def _dispatch_plan(meta, counts, tm, n_tiles):
    cnt = counts[:, 0].astype(jnp.int32)
    tiles = (cnt + tm - 1) // tm
    tile_end = jnp.cumsum(tiles)
    start = (tile_end - tiles) * tm
    e = meta[0:2].astype(jnp.int32)
    start_e = sum(jnp.where(e == k, start[k], 0) for k in range(N_EXPERTS))
    pos = start_e + meta[2:4].astype(jnp.int32)
    tile_ids = jnp.arange(n_tiles, dtype=jnp.int32)
    n_valid = tile_end[-1:].astype(jnp.int32)
    last_valid = jnp.minimum(tile_ids, n_valid[0] - 1)
    te = jnp.sum((last_valid[:, None] >= tile_end[None, :]).astype(jnp.int32), axis=1)
    te = jnp.minimum(te, N_EXPERTS - 1).astype(jnp.int32)
    zero_start = jnp.where(tiles > 0, (tile_end - 1) * tm, -1).astype(jnp.int32)
    return pos, te, n_valid, jnp.concatenate([zero_start, n_valid])


def kernel(x, c, ctx, c_ctx, ada_w, ada_b, norm_mix, norm_ffn, norm_final, ev_w_in, gla_w_gate2, gla_b_gate2, gla_norm, mla_q_norm, mla_w_uq, mla_kv_norm, mla_w_ukv, ev_w_out, ffn_w_gate, ffn_w_up, ffn_w_down, od_w_in, ret_log_decay, ret_norm, od_w_out, moe_router, moe_w_gate, moe_w_up, moe_w_down):
    B, T, D = x.shape
    TC = ctx.shape[1]
    depth = ada_w.shape[0]
    assert depth == 2 and T % GRID_W == 0 and B <= 7
    rows = T // GRID_W
    xl = x.reshape(B * T, D)
    xc = ctx.reshape(B * TC, D)

    c_rows = jnp.concatenate([c, c_ctx[None, :], jnp.zeros((8 - B - 1, D), F32)], axis=0)
    m_all = _ada(c_rows, ada_w, ada_b).reshape(depth, 8, 6, 1, D)
    mods = [[(m_all[i, :B, k], m_all[i, B:B + 1, k]) for k in range(6)] for i in range(depth)]

    (sh_a, sc_a, g_a, sh_f, sc_f, g_f) = mods[0]
    w_main, w_small, wg2, bg2, wq, wkv = _even_weights(ev_w_in[0], gla_w_gate2[0], gla_b_gate2[0],
                                                       mla_w_uq[0], mla_w_ukv[0])
    tn_ev = _tile(EV_MAIN, 768)
    main_l, small_l = _nmm(xl, norm_mix[0], 1.0 + sc_a[0], sh_a[0], w_main, rows_per_group=T, tn=tn_ev,
                           w_small=w_small)
    main_c, small_c = _nmm(xc, norm_mix[0], 1.0 + sc_a[1], sh_a[1], w_main, rows_per_group=B * TC, tn=tn_ev,
                           w_small=w_small)

    s0 = jnp.zeros((B, GLA_HEADS, GLA_DK, GLA_DV), F32)
    ocf, ocb, s_f, s_b = _gla(main_c, small_c, wg2, bg2, s0, s0, batch=B, t=TC)
    olf, olb, _, _ = _gla(main_l, small_l, wg2, bg2, s_f, s_b, batch=B, t=T)

    (cs_m, sn_m), (cs_1, sn_0) = _rope_tables(rows, MLA_ROPE, TC)
    q_l = _mla_q(main_l, mla_q_norm[0], wq, cs_m, sn_m, t=T)
    q_c = _mla_q(main_c, mla_q_norm[0], wq, cs_1, sn_0, t=TC)
    k_l, v_l = _mla_kv(main_l, mla_kv_norm[0], wkv, cs_m, sn_m, t=T)
    k_c, v_c = _mla_kv(main_c, mla_kv_norm[0], wkv, cs_1, sn_0, t=TC)
    mla_l = _attn(q_l, k_l, v_l, k_c, v_c)
    mla_c = _attn(q_c, k_c, v_c)

    w_out = ev_w_out[0].astype(BF16)
    gr_blk = EV_GR // (GLA_HEADS * GLA_DV)
    x1l = _mixout(olf, olb, main_l, gr_blk, gla_norm[0], mla_l, w_out, xl, g_a[0], rows_per_group=T,
                  n_heads=GLA_HEADS, dv=GLA_DV, center=False)
    x1c = _mixout(ocf, ocb, main_c, gr_blk, gla_norm[0], mla_c, w_out, xc, g_a[1], rows_per_group=B * TC,
                  n_heads=GLA_HEADS, dv=GLA_DV, center=False)

    fwg, fwu, fwd_ = ffn_w_gate.astype(BF16), ffn_w_up.astype(BF16), ffn_w_down.astype(BF16)
    tm_l = _tile(T, FFN_ROWS)
    x2l = _ffn(x1l, fwg, fwu, fwd_, jnp.zeros((B * T // tm_l,), jnp.int32), jnp.full((1,), B * T // tm_l, jnp.int32),
               tm=tm_l, norm=(norm_ffn[0], 1.0 + sc_f[0], sh_f[0], g_f[0], T))
    tm_c = _tile(B * TC, FFN_ROWS)
    x2c = _ffn(x1c, fwg, fwu, fwd_, jnp.zeros((B * TC // tm_c,), jnp.int32),
               jnp.full((1,), B * TC // tm_c, jnp.int32), tm=tm_c,
               norm=(norm_ffn[0], 1.0 + sc_f[1], sh_f[1], g_f[1], B * TC))

    (sh_a, sc_a, g_a, sh_f, sc_f, g_f) = mods[1]
    w_odd = _odd_weights(od_w_in[0])
    hk = RET_HEADS * RET_DK
    tn_od = _tile(4 * hk, 1024)
    rope_l, rope_c = _rope_tables(rows, RET_DK, B * TC)
    m2l = _nmm(x2l, norm_mix[1], 1.0 + sc_a[0], sh_a[0], w_odd, rows_per_group=T, tn=tn_od, rope=rope_l,
               rope_cols=2 * hk)
    m2c = _nmm(x2c, norm_mix[1], 1.0 + sc_a[1], sh_a[1], w_odd, rows_per_group=B * TC, tn=tn_od, rope=rope_c,
               rope_cols=2 * hk)
    tables = _ret_tables(ret_log_decay[0])
    s0 = jnp.zeros((B, RET_HEADS, RET_DK, RET_DV), F32)
    _, _, s_f, s_b = _ret(m2c, tables, s0, s0, batch=B, t=TC)
    orf, orb, _, _ = _ret(m2l, tables, s_f, s_b, batch=B, t=T)
    x3 = _mixout(orf, orb, m2l, 3, ret_norm[0], None, od_w_out[0].astype(BF16), x2l, g_a[0], rows_per_group=T,
                 n_heads=RET_HEADS, dv=RET_DV, center=True)

    wr = jnp.concatenate([moe_router[0], jnp.zeros((D, 128 - N_EXPERTS), F32)], axis=1)
    meta, gate, counts = _router(x3, norm_ffn[1], 1.0 + sc_f[0], sh_f[0], wr, rows_per_group=T)
    tm_e = _tile(T, MOE_ROWS)
    n_tiles = (2 * B * T) // tm_e + N_EXPERTS
    pos, te, n_valid, zero_plan = _dispatch_plan(meta, counts, tm_e, n_tiles)
    tm_r = _tile(T, ROUTER_ROWS)
    pos_t = jnp.transpose(pos.reshape(2, -1, tm_r), (1, 0, 2))
    hs = _scatter_rows(x3, norm_ffn[1], 1.0 + sc_f[0], sh_f[0], pos_t, zero_plan, n_tiles=n_tiles,
                       group_rows=tm_e, rows_per_group=T)
    ys = _ffn(hs, moe_w_gate[0].astype(BF16), moe_w_up[0].astype(BF16), moe_w_down[0].astype(BF16), te, n_valid,
              tm=tm_e)
    out = _combine(ys, pos_t, x3, gate, g_f[0], norm_final, rows_per_group=T)
    return out.reshape(B, T, D)
```

```python
import functools

import jax
import jax.numpy as jnp
import numpy as np
from jax import lax
from jax.experimental import pallas as pl
from jax.experimental.pallas import tpu as pltpu

F32 = jnp.float32
BF16 = jnp.bfloat16
EPS = 1e-6
ROPE_BASE = 10000.0
GRID_W = 64

GLA_HEADS, GLA_DK, GLA_DV, GLA_RANK, GLA_GATE_NORM, GLA_CHUNK = 4, 128, 256, 16, 16.0, 64
MLA_HEADS, MLA_Q_RANK, MLA_KV_RANK, MLA_NOPE, MLA_ROPE, MLA_DV = 8, 768, 512, 128, 64, 128
MLA_DQK = MLA_NOPE + MLA_ROPE
MLA_SCALE = MLA_DQK ** -0.5
MLA_Q_SCALE = MLA_SCALE * 1.4426950408889634
RET_HEADS, RET_DK, RET_DV, RET_CHUNK = 8, 256, 256, 128
N_EXPERTS = 8

EV_GV, EV_GR, EV_GQ, EV_GK, EV_MQ, EV_KR, EV_MKV, EV_MAIN = 0, 1024, 2048, 2560, 3072, 3840, 4096, 4608
EV_SMALL = 128

V7X_VMEM_BYTES = 64 << 20
VMEM_HEADROOM_BYTES = 6 << 20
FFN_ROWS = 512
MOE_ROWS = 1024
ROUTER_ROWS = 512
ROW_DMA_UNROLL = 8


def _vmem(nbytes):
    return int(min(V7X_VMEM_BYTES - VMEM_HEADROOM_BYTES, max(32 << 20, nbytes + (8 << 20))))


def _tile(n, pref, align=128):
    if n <= pref:
        return n
    t = (pref // align) * align
    while t >= align:
        if n % t == 0:
            return t
        t -= align
    return n


def _split_bf16(a):
    hi = a.astype(BF16)
    lo = (a - hi.astype(F32)).astype(BF16)
    return hi, lo


def _dot(a, b):
    return jnp.dot(a, b, preferred_element_type=F32)


def _dot3(a, b):
    ah, al = _split_bf16(a)
    bh, bl = _split_bf16(b)
    return _dot(ah, bh) + _dot(al, bh) + _dot(ah, bl)


def _dot_nt(a, b):
    return lax.dot_general(a, b, (((1,), (1,)), ((), ())), preferred_element_type=F32)


def _dot_tn(a, b):
    return lax.dot_general(a, b, (((0,), (0,)), ((), ())), preferred_element_type=F32)


def _silu(x):
    return x * (1.0 / (1.0 + jnp.exp(-x)))


def _log_sigmoid(z):
    return -(jnp.maximum(-z, 0.0) + jnp.log(1.0 + jnp.exp(-jnp.abs(z))))


def _rms(x, g):
    return x * lax.rsqrt(jnp.mean(x * x, axis=-1, keepdims=True) + EPS) * g


def _pack_rows(y):
    half = y.shape[1] // 2
    lo = lax.bitcast_convert_type(y[:, :half].astype(BF16).astype(F32), jnp.uint32)
    hi = lax.bitcast_convert_type(y[:, half:].astype(BF16).astype(F32), jnp.uint32)
    return lax.shift_right_logical(lo, jnp.uint32(16)) | hi


def _packed_shape(m, d):
    return (m, d // 2), jnp.uint32


def _unpack_rows(w):
    lo = lax.bitcast_convert_type(lax.shift_left(w, jnp.uint32(16)), F32)
    hi = lax.bitcast_convert_type(w & jnp.uint32(0xFFFF0000), F32)
    return jnp.concatenate([lo, hi], axis=1)


def _ada_kernel(c_ref, w_ref, b_ref, o_ref):
    o_ref[0] = _dot3(_silu(c_ref[...]), w_ref[0]) + b_ref[0]


def _ada(c_rows, w, b):
    nl, d, n = w.shape
    tn = _tile(n, 1536)
    return pl.pallas_call(
        _ada_kernel,
        grid=(nl, n // tn),
        in_specs=[pl.BlockSpec((8, d), lambda l, j: (0, 0)),
                  pl.BlockSpec((1, d, tn), lambda l, j: (l, 0, j)),
                  pl.BlockSpec((1, 1, tn), lambda l, j: (l, 0, j))],
        out_specs=pl.BlockSpec((1, 8, tn), lambda l, j: (l, 0, j)),
        out_shape=jax.ShapeDtypeStruct((nl, 8, n), F32),
        compiler_params=pltpu.CompilerParams(dimension_semantics=("parallel", "parallel"),
                                             vmem_limit_bytes=_vmem(2 * d * tn * 4 * 3)),
        name="ada_mod",
    )(c_rows, w, b.reshape(nl, 1, n))


def _nmm_kernel(*refs, has_small, rope_tiles, heads_per_tile):
    x_ref, g_ref, sc_ref, sh_ref, w_ref = refs[:5]
    k = 5
    if has_small:
        w2_ref = refs[k]; k += 1
    if rope_tiles:
        cs_ref, sn_ref = refs[k], refs[k + 1]; k += 2
    o_ref = refs[k]; k += 1
    if has_small:
        o2_ref = refs[k]; k += 1
    hn_ref = refs[k]
    j = pl.program_id(1)

    @pl.when(j == 0)
    def _():
        y = _rms(x_ref[...], g_ref[...]) * sc_ref[0] + sh_ref[0]
        hn_ref[...] = y.astype(BF16)
        if has_small:
            o2_ref[...] = _dot3(y, w2_ref[...])

    acc = _dot(hn_ref[...], w_ref[...])
    if rope_tiles:
        @pl.when(j < rope_tiles)
        def _():
            cs = cs_ref[...]
            sn = sn_ref[...]
            half = cs.shape[1] // 2
            outs = []
            for h in range(heads_per_tile):
                r = acc[:, h * 2 * half:(h + 1) * 2 * half]
                rs = jnp.concatenate([r[:, half:], r[:, :half]], axis=1)
                outs.append(r * cs + rs * sn)
            o_ref[...] = jnp.concatenate(outs, axis=1).astype(o_ref.dtype)

        @pl.when(j >= rope_tiles)
        def _():
            o_ref[...] = acc.astype(o_ref.dtype)
    else:
        o_ref[...] = acc.astype(o_ref.dtype)


def _nmm(x, g, sc, sh, w, *, rows_per_group, tn, w_small=None, rope=None, rope_cols=0):
    m, d = x.shape
    n = w.shape[1]
    tm = _tile(rows_per_group, 1024)
    gpt = rows_per_group // tm
    grp = lambda i, j: (i // gpt, 0, 0)
    in_specs = [pl.BlockSpec((tm, d), lambda i, j: (i, 0)),
                pl.BlockSpec((1, d), lambda i, j: (0, 0)),
                pl.BlockSpec((1, 1, d), grp),
                pl.BlockSpec((1, 1, d), grp),
                pl.BlockSpec((d, tn), lambda i, j: (0, j))]
    args = [x, g.reshape(1, d), sc, sh, w]
    out_specs = [pl.BlockSpec((tm, tn), lambda i, j: (i, j))]
    out_shape = [jax.ShapeDtypeStruct((m, n), BF16)]
    if w_small is not None:
        in_specs.append(pl.BlockSpec((d, EV_SMALL), lambda i, j: (0, 0)))
        args.append(w_small)
        out_specs.append(pl.BlockSpec((tm, EV_SMALL), lambda i, j: (i, 0)))
        out_shape.append(jax.ShapeDtypeStruct((m, EV_SMALL), F32))
    rope_tiles = heads_per_tile = 0
    if rope is not None:
        cs, sn = rope
        t_rows, hd = cs.shape
        rope_tiles, heads_per_tile = rope_cols // tn, tn // hd
        tpb = t_rows // tm
        in_specs += [pl.BlockSpec((tm, hd), lambda i, j: (i % tpb, 0)),
                     pl.BlockSpec((tm, hd), lambda i, j: (i % tpb, 0))]
        args += [cs, sn]
    kern = functools.partial(_nmm_kernel, has_small=w_small is not None, rope_tiles=rope_tiles,
                             heads_per_tile=heads_per_tile)
    res = pl.pallas_call(
        kern,
        grid=(m // tm, n // tn),
        in_specs=in_specs,
        out_specs=out_specs,
        out_shape=out_shape,
        scratch_shapes=[pltpu.VMEM((tm, d), BF16)],
        compiler_params=pltpu.CompilerParams(
            dimension_semantics=("parallel", "arbitrary"),
            vmem_limit_bytes=_vmem(2 * tm * d * 4 + tm * d * 2 + 2 * d * tn * 2 + 5 * tm * tn * 4)),
        name="norm_proj",
    )(*args)
    return res if w_small is not None else res[0]


def _gla_kernel(mf_q, mf_k, mf_v, sm_f, mb_q, mb_k, mb_v, sm_b, wg_ref, bg_ref, s0f_ref, s0b_ref,
                of_ref, ob_ref, sf_ref, sb_ref, st_ref, *, n_sub):
    s = pl.program_id(1)
    c = GLA_CHUNK

    @pl.when(s == 0)
    def _():
        st_ref[0] = s0f_ref[0]
        st_ref[1] = s0b_ref[0]

    rows = n_sub * c
    row = lax.broadcasted_iota(jnp.int32, (c, c), 0)
    col = lax.broadcasted_iota(jnp.int32, (c, c), 1)
    brow = lax.broadcasted_iota(jnp.int32, (rows, rows), 0)
    bcol = lax.broadcasted_iota(jnp.int32, (rows, rows), 1)
    same_chunk = (brow // c) == (bcol // c)
    for d, (q_ref, k_ref, v_ref, sm_ref, o_ref) in enumerate(
            ((mf_q, mf_k, mf_v, sm_f, of_ref), (mb_q, mb_k, mb_v, sm_b, ob_ref))):
        keep = (col <= row) if d == 0 else (col >= row)
        bkeep = jnp.logical_and(same_chunk, (bcol <= brow) if d == 0 else (bcol >= brow))
        tri = jnp.where(bkeep, 1.0, 0.0).astype(BF16)
        z = _dot3(sm_ref[...], wg_ref[d]) + bg_ref[d]
        lh, ll = _split_bf16(_log_sigmoid(z) * (1.0 / GLA_GATE_NORM))
        b_all = _dot(tri, lh) + _dot(tri, ll)
        order = range(n_sub) if d == 0 else range(n_sub - 1, -1, -1)
        for ci in order:
            r0 = ci * c
            for h in range(GLA_HEADS):
                b = b_all[r0:r0 + c, h * GLA_DK:(h + 1) * GLA_DK]
                b_last = b[c - 1:c, :] if d == 0 else b[0:1, :]
                q = q_ref[r0:r0 + c, h * GLA_DK:(h + 1) * GLA_DK].astype(F32) * (GLA_DK ** -0.5)
                k = k_ref[r0:r0 + c, h * GLA_DK:(h + 1) * GLA_DK].astype(F32)
                v = v_ref[r0:r0 + c, h * GLA_DV:(h + 1) * GLA_DV]
                qe = (q * jnp.exp(b)).astype(BF16)
                kd = (k * jnp.exp(-b)).astype(BF16)
                kl = (k * jnp.exp(b_last - b)).astype(BF16)
                att = jnp.where(keep, _dot_nt(qe, kd), 0.0).astype(BF16)
                st = st_ref[d, h]
                o = _dot(att, v) + _dot_nt(qe, st.astype(BF16))
                o_ref[r0:r0 + c, h * GLA_DV:(h + 1) * GLA_DV] = o.astype(o_ref.dtype)
                st_ref[d, h] = jnp.exp(b_last) * st + _dot_tn(v, kl)

    @pl.when(s == pl.num_programs(1) - 1)
    def _():
        sf_ref[0] = st_ref[0]
        sb_ref[0] = st_ref[1]


def _gla(main, small, wg, bg, s0f, s0b, *, batch, t):
    rows = _tile(t, 256, GLA_CHUNK)
    ns = t // rows
    hv, hk = GLA_HEADS * GLA_DV, GLA_HEADS * GLA_DK
    fwd = lambda blk: (lambda b, s: (b * ns + s, blk))
    bwd = lambda blk: (lambda b, s: (b * ns + ns - 1 - s, blk))
    st_spec = pl.BlockSpec((1, GLA_HEADS, GLA_DV, GLA_DK), lambda b, s: (b, 0, 0, 0))
    st_shape = jax.ShapeDtypeStruct((batch, GLA_HEADS, GLA_DV, GLA_DK), F32)

    def io(m):
        return [pl.BlockSpec((rows, hk), m(EV_GQ // hk)), pl.BlockSpec((rows, hk), m(EV_GK // hk)),
                pl.BlockSpec((rows, hv), m(EV_GV // hv)), pl.BlockSpec((rows, EV_SMALL), m(0))]

    return pl.pallas_call(
        functools.partial(_gla_kernel, n_sub=rows // GLA_CHUNK),
        grid=(batch, ns),
        in_specs=io(fwd) + io(bwd) + [
            pl.BlockSpec((2, EV_SMALL, hk), lambda b, s: (0, 0, 0)),
            pl.BlockSpec((2, 1, hk), lambda b, s: (0, 0, 0)),
            st_spec, st_spec],
        out_specs=[pl.BlockSpec((rows, hv), fwd(0)), pl.BlockSpec((rows, hv), bwd(0)), st_spec, st_spec],
        out_shape=[jax.ShapeDtypeStruct((batch * t, hv), BF16), jax.ShapeDtypeStruct((batch * t, hv), BF16),
                   st_shape, st_shape],
        scratch_shapes=[pltpu.VMEM((2, GLA_HEADS, GLA_DV, GLA_DK), F32)],
        compiler_params=pltpu.CompilerParams(dimension_semantics=("parallel", "arbitrary"),
                                             vmem_limit_bytes=_vmem(16 << 20)),
        name="gla_scan",
    )(main, main, main, small, main, main, main, small, wg, bg, s0f, s0b)


def _mla_q_kernel(x_ref, g_ref, w_ref, cs_ref, sn_ref, o_ref, hn_ref):
    @pl.when(pl.program_id(1) == 0)
    def _():
        hn_ref[...] = _rms(x_ref[...].astype(F32), g_ref[...]).astype(BF16)

    r = _dot(hn_ref[...], w_ref[0])
    qr = r[:, MLA_NOPE:MLA_NOPE + MLA_ROPE] * cs_ref[...] + r[:, MLA_NOPE + MLA_ROPE:] * sn_ref[...]
    o_ref[0, 0] = (jnp.concatenate([r[:, :MLA_NOPE], qr], axis=1) * MLA_Q_SCALE).astype(o_ref.dtype)


def _mla_q(main, g, w, cs, sn, *, t):
    m = main.shape[0]
    tm = _tile(t, 1024)
    tpb = t // tm
    return pl.pallas_call(
        _mla_q_kernel,
        grid=(m // tm, MLA_HEADS),
        in_specs=[pl.BlockSpec((tm, MLA_Q_RANK), lambda i, h: (i, EV_MQ // MLA_Q_RANK)),
                  pl.BlockSpec((1, MLA_Q_RANK), lambda i, h: (0, 0)),
                  pl.BlockSpec((1, MLA_Q_RANK, 2 * MLA_NOPE), lambda i, h: (h, 0, 0)),
                  pl.BlockSpec((tm, MLA_ROPE), lambda i, h: (i % tpb, 0)),
                  pl.BlockSpec((tm, MLA_ROPE), lambda i, h: (i % tpb, 0))],
        out_specs=pl.BlockSpec((1, 1, tm, MLA_DQK), lambda i, h: (i // tpb, h, i % tpb, 0)),
        out_shape=jax.ShapeDtypeStruct((m // t, MLA_HEADS, t, MLA_DQK), BF16),
        scratch_shapes=[pltpu.VMEM((tm, MLA_Q_RANK), BF16)],
        compiler_params=pltpu.CompilerParams(dimension_semantics=("parallel", "arbitrary")),
        name="mla_q_proj",
    )(main, g.reshape(1, -1), w, cs, sn)


def _mla_kv_kernel(x_ref, kr_ref, g_ref, w_ref, cs_ref, sn_ref, k_ref, v_ref, hn_ref):
    @pl.when(pl.program_id(1) == 0)
    def _():
        hn_ref[...] = _rms(x_ref[...].astype(F32), g_ref[...]).astype(BF16)

    r = _dot(hn_ref[...], w_ref[0])
    kr = kr_ref[...].astype(F32)
    krr = kr[:, :MLA_ROPE] * cs_ref[...] + kr[:, MLA_ROPE:2 * MLA_ROPE] * sn_ref[...]
    k_ref[0, 0] = jnp.concatenate([r[:, :MLA_NOPE], krr], axis=1).astype(k_ref.dtype)
    v_ref[0, 0] = r[:, MLA_NOPE:].astype(v_ref.dtype)


def _mla_kv(main, g, w, cs, sn, *, t):
    m = main.shape[0]
    tm = _tile(t, 1024)
    tpb = t // tm
    return pl.pallas_call(
        _mla_kv_kernel,
        grid=(m // tm, MLA_HEADS),
        in_specs=[pl.BlockSpec((tm, MLA_KV_RANK), lambda i, h: (i, EV_MKV // MLA_KV_RANK)),
                  pl.BlockSpec((tm, 256), lambda i, h: (i, EV_KR // 256)),
                  pl.BlockSpec((1, MLA_KV_RANK), lambda i, h: (0, 0)),
                  pl.BlockSpec((1, MLA_KV_RANK, MLA_NOPE + MLA_DV), lambda i, h: (h, 0, 0)),
                  pl.BlockSpec((tm, MLA_ROPE), lambda i, h: (i % tpb, 0)),
                  pl.BlockSpec((tm, MLA_ROPE), lambda i, h: (i % tpb, 0))],
        out_specs=[pl.BlockSpec((1, 1, tm, MLA_DQK), lambda i, h: (i // tpb, h, i % tpb, 0)),
                   pl.BlockSpec((1, 1, tm, MLA_DV), lambda i, h: (i // tpb, h, i % tpb, 0))],
        out_shape=[jax.ShapeDtypeStruct((m // t, MLA_HEADS, t, MLA_DQK), BF16),
                   jax.ShapeDtypeStruct((m // t, MLA_HEADS, t, MLA_DV), BF16)],
        scratch_shapes=[pltpu.VMEM((tm, MLA_KV_RANK), BF16)],
        compiler_params=pltpu.CompilerParams(dimension_semantics=("parallel", "arbitrary")),
        name="mla_kv_proj",
    )(main, main, g.reshape(1, -1), w, cs, sn)


ATTN_ROWS = 256


def _attn_kernel(*refs, n_main, has_extra):
    if has_extra:
        q_ref, k_ref, v_ref, kx_ref, vx_ref, o_ref, m_ref, acc_ref = refs
    else:
        q_ref, k_ref, v_ref, o_ref, m_ref, acc_ref = refs
    kv = pl.program_id(3)

    @pl.when(kv == 0)
    def _():
        m_ref[...] = jnp.full(m_ref.shape, -jnp.inf, F32)
        acc_ref[...] = jnp.zeros(acc_ref.shape, F32)

    tq = q_ref.shape[2]
    rc = min(ATTN_ROWS, tq)

    def step(k, v):
        v_ones = jnp.concatenate([v, jnp.ones(v.shape, v.dtype)], axis=1)
        for r in range(tq // rc):
            rows = slice(r * rc, (r + 1) * rc)
            s = _dot_nt(q_ref[0, 0, rows, :], k)
            m_old = m_ref[rows, :]
            m_new = jnp.maximum(m_old, jnp.max(s, axis=-1, keepdims=True))
            p = jnp.exp2(s - m_new).astype(BF16)
            acc_ref[rows, :] = jnp.exp2(m_old - m_new) * acc_ref[rows, :] + _dot(p, v_ones)
            m_ref[rows, :] = m_new

    pl.when(kv < n_main)(lambda: step(k_ref[0, 0], v_ref[0, 0]))
    if has_extra:
        pl.when(kv == n_main)(lambda: step(kx_ref[0, 0], vx_ref[0, 0]))

    @pl.when(kv == pl.num_programs(3) - 1)
    def _():
        acc = acc_ref[...]
        o_ref[...] = (acc[:, :MLA_DV] / acc[:, MLA_DV:]).astype(o_ref.dtype)


def _attn(q, k, v, kx=None, vx=None, *, tq_pref=2048, tk_pref=8192):
    b, h, tq_all, dqk = q.shape
    tk_all = k.shape[2]
    tq, tk = _tile(tq_all, tq_pref), _tile(tk_all, tk_pref)
    nq, n_main = tq_all // tq, tk_all // tk
    main = lambda bi, hi, i, j: (bi, hi, jnp.minimum(j, n_main - 1), 0)
    in_specs = [pl.BlockSpec((1, 1, tq, dqk), lambda bi, hi, i, j: (bi, hi, i, 0)),
                pl.BlockSpec((1, 1, tk, dqk), main), pl.BlockSpec((1, 1, tk, MLA_DV), main)]
    args = [q, k, v]
    if kx is not None:
        tx = kx.shape[2]
        in_specs += [pl.BlockSpec((1, 1, tx, dqk), lambda bi, hi, i, j: (bi, hi, 0, 0)),
                     pl.BlockSpec((1, 1, tx, MLA_DV), lambda bi, hi, i, j: (bi, hi, 0, 0))]
        args += [kx, vx]
    return pl.pallas_call(
        functools.partial(_attn_kernel, n_main=n_main, has_extra=kx is not None),
        grid=(b, h, nq, n_main + (kx is not None)),
        in_specs=in_specs,
        out_specs=pl.BlockSpec((tq, MLA_DV), lambda bi, hi, i, j: (bi * nq + i, hi)),
        out_shape=jax.ShapeDtypeStruct((b * tq_all, h * MLA_DV), BF16),
        scratch_shapes=[pltpu.VMEM((tq, 1), F32), pltpu.VMEM((tq, 2 * MLA_DV), F32)],
        compiler_params=pltpu.CompilerParams(
            dimension_semantics=("parallel", "parallel", "parallel", "arbitrary"),
            vmem_limit_bytes=_vmem(5 * min(ATTN_ROWS, tq) * tk * 4 + 8 * tk * 256 * 2)),
        name="mla_attention",
    )(*args)


def _mixout_kernel(*refs, n_heads, dv, center, has_mla):
    of_ref, ob_ref, gate_ref, g_ref = refs[:4]
    k = 4
    if has_mla:
        mla_ref = refs[k]; k += 1
    w_ref, x_ref, ga_ref, o_ref = refs[k:k + 4]
    o = of_ref[...].astype(F32) + ob_ref[...].astype(F32)
    g = g_ref[...]
    parts = []
    for h in range(n_heads):
        oh = o[:, h * dv:(h + 1) * dv]
        if center:
            oh = oh - jnp.mean(oh, axis=-1, keepdims=True)
        parts.append(oh * lax.rsqrt(jnp.mean(oh * oh, axis=-1, keepdims=True) + EPS) * g)
    lat = jnp.concatenate(parts, axis=1) * _silu(gate_ref[...].astype(F32))
    lhs = lat.astype(BF16)
    if has_mla:
        lhs = jnp.concatenate([lhs, mla_ref[...]], axis=1)
    o_ref[...] = x_ref[...] + ga_ref[0] * _dot(lhs, w_ref[...])


def _mixout(o_f, o_b, main, gate_blk, g, mla, w, x, ga, *, rows_per_group, n_heads, dv, center):
    m, d = x.shape
    hw = n_heads * dv
    tm = _tile(rows_per_group, 512)
    gpt = rows_per_group // tm
    in_specs = [pl.BlockSpec((tm, hw), lambda i: (i, 0)), pl.BlockSpec((tm, hw), lambda i: (i, 0)),
                pl.BlockSpec((tm, hw), lambda i: (i, gate_blk)), pl.BlockSpec((1, dv), lambda i: (0, 0))]
    args = [o_f, o_b, main, g.reshape(1, dv)]
    if mla is not None:
        in_specs.append(pl.BlockSpec((tm, mla.shape[1]), lambda i: (i, 0)))
        args.append(mla)
    in_specs += [pl.BlockSpec(w.shape, lambda i: (0, 0)), pl.BlockSpec((tm, d), lambda i: (i, 0)),
                 pl.BlockSpec((1, 1, d), lambda i: (i // gpt, 0, 0))]
    args += [w, x, ga]
    return pl.pallas_call(
        functools.partial(_mixout_kernel, n_heads=n_heads, dv=dv, center=center, has_mla=mla is not None),
        grid=(m // tm,),
        in_specs=in_specs,
        out_specs=pl.BlockSpec((tm, d), lambda i: (i, 0)),
        out_shape=jax.ShapeDtypeStruct((m, d), F32),
        compiler_params=pltpu.CompilerParams(
            dimension_semantics=("parallel",),
            vmem_limit_bytes=_vmem(2 * w.size * 2 + 4 * tm * d * 4 + 8 * tm * hw * 2 + 6 * tm * hw * 4)),
        name="mixer_out",
    )(*args)


def _ffn_kernel(te_ref, nv_ref, *refs, prenorm):
    if prenorm:
        x_ref, g_ref, sc_ref, sh_ref, wg_ref, wu_ref, wd_ref, gf_ref, o_ref, hn_ref, acc_ref = refs
    else:
        x_ref, wg_ref, wu_ref, wd_ref, o_ref, hn_ref, acc_ref = refs
    i, j = pl.program_id(0), pl.program_id(1)
    valid = i < nv_ref[0]

    @pl.when(jnp.logical_and(valid, j == 0))
    def _():
        if prenorm:
            y = _rms(x_ref[...], g_ref[...]) * sc_ref[0] + sh_ref[0]
        else:
            y = _unpack_rows(x_ref[...])
        hn_ref[...] = y.astype(BF16)
        acc_ref[...] = jnp.zeros(acc_ref.shape, F32)

    @pl.when(valid)
    def _():
        hn = hn_ref[...]
        a = (_silu(_dot(hn, wg_ref[0])) * _dot(hn, wu_ref[0])).astype(BF16)
        acc_ref[...] += _dot(a, wd_ref[0])

    last = j == pl.num_programs(1) - 1

    @pl.when(jnp.logical_and(valid, last))
    def _():
        if prenorm:
            o_ref[...] = x_ref[...] + gf_ref[0] * acc_ref[...]
        else:
            o_ref[...] = _pack_rows(acc_ref[...])

    @pl.when(jnp.logical_and(jnp.logical_not(valid), last))
    def _():
        o_ref[...] = jnp.zeros(o_ref.shape, o_ref.dtype)


def _ffn(x, wg, wu, wd, tile_expert, n_valid, *, tm, norm=None):
    p = x.shape[0]
    d = wg.shape[1]
    f = wg.shape[2]
    tf = _tile(f, 512)
    nt = p // tm
    wmap_in = lambda i, j, te, nv: (te[i], 0, j)
    wmap_out = lambda i, j, te, nv: (te[i], j, 0)
    row = lambda i, j, te, nv: (i, 0)
    in_specs = [pl.BlockSpec((tm, x.shape[1]), lambda i, j, te, nv: (jnp.minimum(i, nv[0] - 1), 0))]
    args = [x]
    if norm is not None:
        g, sc, sh, gf, rpg = norm
        gpt = rpg // tm
        grp = lambda i, j, te, nv: (i // gpt, 0, 0)
        in_specs += [pl.BlockSpec((1, d), lambda i, j, te, nv: (0, 0)), pl.BlockSpec((1, 1, d), grp),
                     pl.BlockSpec((1, 1, d), grp)]
        args += [g.reshape(1, d), sc, sh]
    in_specs += [pl.BlockSpec((1, d, tf), wmap_in), pl.BlockSpec((1, d, tf), wmap_in),
                 pl.BlockSpec((1, tf, d), wmap_out)]
    args += [wg, wu, wd]
    if norm is not None:
        in_specs.append(pl.BlockSpec((1, 1, d), grp))
        args.append(gf)
    return pl.pallas_call(
        functools.partial(_ffn_kernel, prenorm=norm is not None),
        grid_spec=pltpu.PrefetchScalarGridSpec(
            num_scalar_prefetch=2,
            grid=(nt, f // tf),
            in_specs=in_specs,
            out_specs=pl.BlockSpec((tm, x.shape[1]), row),
            scratch_shapes=[pltpu.VMEM((tm, d), BF16), pltpu.VMEM((tm, d), F32)]),
        out_shape=jax.ShapeDtypeStruct(x.shape, x.dtype),
        compiler_params=pltpu.CompilerParams(
            dimension_semantics=("parallel", "arbitrary"),
            vmem_limit_bytes=_vmem(4 * tm * x.shape[1] * 4 + tm * d * 6 + 6 * d * tf * 2 + 5 * tm * tf * 4)),
        name="swiglu_ffn",
    )(tile_expert, n_valid, *args)


def _ret_kernel(qf_ref, kf_ref, vf_ref, qb_ref, kb_ref, vb_ref, dm_ref, qd_ref, kd_ref, cd_ref,
                s0f_ref, s0b_ref, of_ref, ob_ref, sf_ref, sb_ref, st_ref, *, n_sub):
    s = pl.program_id(1)
    c = RET_CHUNK

    @pl.when(s == 0)
    def _():
        st_ref[0] = s0f_ref[0]
        st_ref[1] = s0b_ref[0]

    for d, (q_ref, k_ref, v_ref, o_ref) in enumerate(((qf_ref, kf_ref, vf_ref, of_ref),
                                                       (qb_ref, kb_ref, vb_ref, ob_ref))):
        order = range(n_sub) if d == 0 else range(n_sub - 1, -1, -1)
        for ci in order:
            r0 = ci * c
            for h in range(RET_HEADS):
                q = q_ref[r0:r0 + c, h * RET_DK:(h + 1) * RET_DK]
                k = (k_ref[r0:r0 + c, h * RET_DK:(h + 1) * RET_DK].astype(F32) * (RET_DK ** -0.5)).astype(BF16)
                v = v_ref[r0:r0 + c, h * RET_DV:(h + 1) * RET_DV]
                att = (_dot_nt(q, k) * dm_ref[d, h]).astype(BF16)
                st = st_ref[d, h]
                o = _dot(att, v) + qd_ref[d, h] * _dot(q, st.astype(BF16))
                o_ref[r0:r0 + c, h * RET_DV:(h + 1) * RET_DV] = o.astype(o_ref.dtype)
                vk = (v.astype(F32) * kd_ref[d, h]).astype(BF16)
                st_ref[d, h] = cd_ref[d, h] * st + _dot_tn(k, vk)

    @pl.when(s == pl.num_programs(1) - 1)
    def _():
        sf_ref[0] = st_ref[0]
        sb_ref[0] = st_ref[1]


def _ret(main, tables, s0f, s0b, *, batch, t):
    rows = _tile(t, 256, RET_CHUNK)
    ns = t // rows
    hk, hv = RET_HEADS * RET_DK, RET_HEADS * RET_DV
    fwd = lambda blk: (lambda b, s: (b * ns + s, blk))
    bwd = lambda blk: (lambda b, s: (b * ns + ns - 1 - s, blk))
    st_spec = pl.BlockSpec((1, RET_HEADS, RET_DK, RET_DV), lambda b, s: (b, 0, 0, 0))
    st_shape = jax.ShapeDtypeStruct((batch, RET_HEADS, RET_DK, RET_DV), F32)
    io = lambda m: [pl.BlockSpec((rows, hk), m(0)), pl.BlockSpec((rows, hk), m(1)), pl.BlockSpec((rows, hv), m(2))]
    full = lambda a: pl.BlockSpec(a.shape, lambda b, s: (0,) * a.ndim)
    return pl.pallas_call(
        functools.partial(_ret_kernel, n_sub=rows // RET_CHUNK),
        grid=(batch, ns),
        in_specs=io(fwd) + io(bwd) + [full(a) for a in tables] + [st_spec, st_spec],
        out_specs=[pl.BlockSpec((rows, hv), fwd(0)), pl.BlockSpec((rows, hv), bwd(0)), st_spec, st_spec],
        out_shape=[jax.ShapeDtypeStruct((batch * t, hv), BF16), jax.ShapeDtypeStruct((batch * t, hv), BF16),
                   st_shape, st_shape],
        scratch_shapes=[pltpu.VMEM((2, RET_HEADS, RET_DK, RET_DV), F32)],
        compiler_params=pltpu.CompilerParams(dimension_semantics=("parallel", "arbitrary"),
                                             vmem_limit_bytes=_vmem(40 << 20)),
        name="retention_scan",
    )(main, main, main, main, main, main, *tables, s0f, s0b)


def _router_kernel(x_ref, g_ref, sc_ref, sh_ref, wr_ref, meta_ref, gate_ref, cnt_ref, run_ref):
    i = pl.program_id(0)

    @pl.when(i == 0)
    def _():
        run_ref[...] = jnp.zeros(run_ref.shape, F32)

    y = _rms(x_ref[...], g_ref[...]) * sc_ref[0] + sh_ref[0]
    tm = y.shape[0]
    lt = jnp.transpose(_dot3(y, wr_ref[...]))[:N_EXPERTS, :]
    eid = lax.broadcasted_iota(jnp.int32, lt.shape, 0).astype(F32)
    v1 = jnp.max(lt, axis=0, keepdims=True)
    e1 = jnp.min(jnp.where(lt == v1, eid, float(N_EXPERTS)), axis=0, keepdims=True)
    lt2 = jnp.where(eid == e1, -jnp.inf, lt)
    v2 = jnp.max(lt2, axis=0, keepdims=True)
    e2 = jnp.min(jnp.where(lt2 == v2, eid, float(N_EXPERTS)), axis=0, keepdims=True)
    ex = jnp.exp(v2 - v1)
    w1 = 1.0 / (1.0 + ex)
    w2 = ex / (1.0 + ex)
    oh1 = jnp.where(eid == e1, 1.0, 0.0)
    oh2 = jnp.where(eid == e2, 1.0, 0.0)
    mem = oh1 + oh2
    r = lax.broadcasted_iota(jnp.int32, (tm, tm), 0)
    cidx = lax.broadcasted_iota(jnp.int32, (tm, tm), 1)
    before = jnp.where(r < cidx, 1.0, 0.0).astype(BF16)
    excl = _dot(mem.astype(BF16), before) + run_ref[...][:, :1]
    rk1 = jnp.sum(oh1 * excl, axis=0, keepdims=True)
    rk2 = jnp.sum(oh2 * excl, axis=0, keepdims=True)
    run_ref[...] = run_ref[...] + jnp.sum(mem, axis=1, keepdims=True)
    zero = jnp.zeros_like(e1)
    meta_ref[...] = jnp.concatenate([e1, e2, rk1, rk2, w1, w2, zero, zero], axis=0)
    wpad = jnp.concatenate([w1, w2, jnp.zeros((6, tm), F32)], axis=0)
    gate_ref[...] = jnp.transpose(jnp.concatenate([wpad] * 16, axis=0))
    cnt_ref[...] = run_ref[...]


def _router(x, g, sc, sh, wr, *, rows_per_group):
    m, d = x.shape
    tm = _tile(rows_per_group, ROUTER_ROWS)
    gpt = rows_per_group // tm
    grp = lambda i: (i // gpt, 0, 0)
    return pl.pallas_call(
        _router_kernel,
        grid=(m // tm,),
        in_specs=[pl.BlockSpec((tm, d), lambda i: (i, 0)), pl.BlockSpec((1, d), lambda i: (0, 0)),
                  pl.BlockSpec((1, 1, d), grp), pl.BlockSpec((1, 1, d), grp),
                  pl.BlockSpec((d, 128), lambda i: (0, 0))],
        out_specs=[pl.BlockSpec((8, tm), lambda i: (0, i)),
                   pl.BlockSpec((tm, 128), lambda i: (i, 0)), pl.BlockSpec((N_EXPERTS, 128), lambda i: (0, 0))],
        out_shape=[jax.ShapeDtypeStruct((8, m), F32),
                   jax.ShapeDtypeStruct((m, 128), F32), jax.ShapeDtypeStruct((N_EXPERTS, 128), F32)],
        scratch_shapes=[pltpu.VMEM((N_EXPERTS, 128), F32)],
        compiler_params=pltpu.CompilerParams(dimension_semantics=("arbitrary",),
                                             vmem_limit_bytes=_vmem(6 * tm * d * 4)),
        name="moe_router",
    )(x, g.reshape(1, d), sc, sh, wr)


def _scatter_kernel(zs_ref, pos_ref, x_ref, g_ref, sc_ref, sh_ref, o_ref, hbuf, sem, *, group_rows, n_tiles):
    tm = hbuf.shape[0]

    def zero_tile(start):
        start = pl.multiple_of(start, 8)
        for c in range(group_rows // tm):
            cp = pltpu.make_async_copy(hbuf, o_ref.at[pl.ds(start + c * tm, tm)], sem.at[0])
            cp.start()
            cp.wait()

    @pl.when(pl.program_id(0) == 0)
    def _():
        hbuf[...] = jnp.zeros(hbuf.shape, hbuf.dtype)
        for e in range(N_EXPERTS):
            pl.when(zs_ref[e] >= 0)(functools.partial(zero_tile, zs_ref[e]))
            tail = zs_ref[N_EXPERTS] + e
            pl.when(tail < n_tiles)(functools.partial(zero_tile, tail * group_rows))

    hbuf[...] = _pack_rows(_rms(x_ref[...], g_ref[...]) * sc_ref[0] + sh_ref[0])

    def copies(t):
        return (pltpu.make_async_copy(hbuf.at[t], o_ref.at[pos_ref[0, 0, t]], sem.at[0]),
                pltpu.make_async_copy(hbuf.at[t], o_ref.at[pos_ref[0, 1, t]], sem.at[1]))

    def issue(t, carry):
        for cp in copies(t):
            cp.start()
        return carry

    lax.fori_loop(0, tm, issue, 0, unroll=ROW_DMA_UNROLL)

    def drain(t, carry):
        for cp in copies(t):
            cp.wait()
        return carry

    lax.fori_loop(0, tm, drain, 0, unroll=ROW_DMA_UNROLL)


def _scatter_rows(x, g, sc, sh, pos, zero_plan, *, n_tiles, group_rows, rows_per_group):
    m, d = x.shape
    nt, _, tm = pos.shape
    assert group_rows % tm == 0
    gpt = rows_per_group // tm
    grp = lambda i, zs: (i // gpt, 0, 0)
    pshape, pdtype = _packed_shape(tm, d)
    return pl.pallas_call(
        functools.partial(_scatter_kernel, group_rows=group_rows, n_tiles=n_tiles),
        grid_spec=pltpu.PrefetchScalarGridSpec(
            num_scalar_prefetch=1,
            grid=(nt,),
            in_specs=[pl.BlockSpec((1, 2, tm), lambda i, zs: (i, 0, 0), memory_space=pltpu.SMEM),
                      pl.BlockSpec((tm, d), lambda i, zs: (i, 0)), pl.BlockSpec((1, d), lambda i, zs: (0, 0)),
                      pl.BlockSpec((1, 1, d), grp), pl.BlockSpec((1, 1, d), grp)],
            out_specs=pl.BlockSpec(memory_space=pl.ANY),
            scratch_shapes=[pltpu.VMEM(pshape, pdtype), pltpu.SemaphoreType.DMA((2,))]),
        out_shape=jax.ShapeDtypeStruct((n_tiles * group_rows, pshape[1]), pdtype),
        compiler_params=pltpu.CompilerParams(dimension_semantics=("arbitrary",),
                                             vmem_limit_bytes=_vmem(6 * tm * d * 4)),
        name="moe_scatter",
    )(zero_plan, pos, x, g.reshape(1, d), sc, sh)


def _combine_kernel(pos_ref, y_ref, x_ref, gate_ref, gf_ref, gn_ref, o_ref, buf_ref, sem):
    tm = pos_ref.shape[2]

    def copies(t):
        return (pltpu.make_async_copy(y_ref.at[pos_ref[0, 0, t]], buf_ref.at[0, t], sem.at[0]),
                pltpu.make_async_copy(y_ref.at[pos_ref[0, 1, t]], buf_ref.at[1, t], sem.at[1]))

    def issue(t, carry):
        for cp in copies(t):
            cp.start()
        return carry

    lax.fori_loop(0, tm, issue, 0, unroll=ROW_DMA_UNROLL)

    def drain(t, carry):
        for cp in copies(t):
            cp.wait()
        return carry

    lax.fori_loop(0, tm, drain, 0, unroll=ROW_DMA_UNROLL)
    w = gate_ref[...]
    moe = w[:, 0:1] * _unpack_rows(buf_ref[0]) + w[:, 1:2] * _unpack_rows(buf_ref[1])
    o_ref[...] = _rms(x_ref[...] + gf_ref[0] * moe, gn_ref[...])


def _combine(y, pos, x, gate, gf, gn, *, rows_per_group):
    m, d = x.shape
    nt, _, tm = pos.shape
    gpt = rows_per_group // tm
    return pl.pallas_call(
        _combine_kernel,
        grid=(nt,),
        in_specs=[pl.BlockSpec((1, 2, tm), lambda i: (i, 0, 0), memory_space=pltpu.SMEM),
                  pl.BlockSpec(memory_space=pl.ANY),
                  pl.BlockSpec((tm, d), lambda i: (i, 0)), pl.BlockSpec((tm, 128), lambda i: (i, 0)),
                  pl.BlockSpec((1, 1, d), lambda i: (i // gpt, 0, 0)), pl.BlockSpec((1, d), lambda i: (0, 0))],
        out_specs=pl.BlockSpec((tm, d), lambda i: (i, 0)),
        out_shape=jax.ShapeDtypeStruct((m, d), F32),
        scratch_shapes=[pltpu.VMEM((2, tm, y.shape[1]), y.dtype), pltpu.SemaphoreType.DMA((2,))],
        compiler_params=pltpu.CompilerParams(dimension_semantics=("arbitrary",),
                                             vmem_limit_bytes=_vmem(8 * tm * d * 4)),
        name="moe_combine",
    )(pos, y, x, gate, gf, gn.reshape(1, d))


def _axial_angles(rows, dim):
    row = jnp.repeat(jnp.arange(rows, dtype=F32), GRID_W)
    col = jnp.tile(jnp.arange(GRID_W, dtype=F32), rows)
    half = dim // 2
    inv = 1.0 / (ROPE_BASE ** (jnp.arange(0, half, 2, dtype=F32) / half))
    return jnp.concatenate([row[:, None] * inv, col[:, None] * inv], axis=-1)


def _rope_tables(rows, dim, t_ctx):
    ang = _axial_angles(rows, dim)
    cs = jnp.concatenate([jnp.cos(ang), jnp.cos(ang)], axis=1)
    sn = jnp.concatenate([-jnp.sin(ang), jnp.sin(ang)], axis=1)
    return (cs, sn), (jnp.ones((t_ctx, dim), F32), jnp.zeros((t_ctx, dim), F32))


def _evens_odds(w):
    pairs = w.reshape(w.shape[:-1] + (w.shape[-1] // 2, 2))
    return pairs[..., 0], pairs[..., 1]


def _even_weights(w_in, w_g2, b_g2, w_uq, w_ukv):
    d = w_in.shape[0]
    sizes = (512, 512, 1024, 1024, 16, 16, MLA_Q_RANK, MLA_KV_RANK, MLA_ROPE)
    offs = np.cumsum((0,) + sizes)
    gq, gk, gv, gr, gaf, gab, mq, mkv, mkr = [w_in[:, offs[i]:offs[i + 1]] for i in range(9)]
    kr_e, kr_o = _evens_odds(mkr)
    w_main = jnp.concatenate([gv, gr, gq, gk, mq, kr_e, kr_o, kr_o, kr_e, jnp.zeros((d, 128), F32), mkv],
                             axis=1).astype(BF16)
    w_small = jnp.concatenate([gaf, gab, jnp.zeros((d, EV_SMALL - 2 * GLA_RANK), F32)], axis=1)
    wg = jnp.zeros((2, EV_SMALL, GLA_HEADS * GLA_DK), F32)
    wg = wg.at[0, :GLA_RANK].set(w_g2[0]).at[1, GLA_RANK:2 * GLA_RANK].set(w_g2[1])
    bg = b_g2.reshape(2, 1, -1)
    wq = w_uq.reshape(MLA_Q_RANK, MLA_HEADS, MLA_DQK)
    qr_e, qr_o = _evens_odds(wq[:, :, MLA_NOPE:])
    wq = jnp.concatenate([wq[:, :, :MLA_NOPE], qr_e, qr_o, qr_o, qr_e], axis=2)
    wq = jnp.transpose(wq, (1, 0, 2)).astype(BF16)
    wkv = jnp.transpose(w_ukv.reshape(MLA_KV_RANK, MLA_HEADS, MLA_NOPE + MLA_DV), (1, 0, 2)).astype(BF16)
    return w_main, w_small, wg, bg, wq, wkv


def _perm_kernel(w_ref, p_ref, o_ref):
    o_ref[...] = _dot(w_ref[...].astype(BF16), p_ref[...]).astype(o_ref.dtype)


def _odd_weights(w_in):
    d, n = w_in.shape
    hd = RET_DK
    n_perm = 2 * RET_HEADS
    src = np.concatenate([np.arange(0, hd, 2), np.arange(1, hd, 2)])
    perm = np.zeros((2, hd, hd), np.float32)
    perm[0, src, np.arange(hd)] = 1.0
    perm[1] = np.eye(hd, dtype=np.float32)
    return pl.pallas_call(
        _perm_kernel,
        grid=(n // hd,),
        in_specs=[pl.BlockSpec((d, hd), lambda j: (0, j)),
                  pl.BlockSpec((None, hd, hd), lambda j: (jnp.where(j < n_perm, 0, 1), 0, 0))],
        out_specs=pl.BlockSpec((d, hd), lambda j: (0, j)),
        out_shape=jax.ShapeDtypeStruct((d, n), BF16),
        compiler_params=pltpu.CompilerParams(dimension_semantics=("parallel",)),
        name="weight_reorder",
    )(w_in, jnp.asarray(perm, BF16))


def _ret_tables(log_decay):
    lg = -jnp.exp(log_decay.astype(F32))
    c = RET_CHUNK
    pos = jnp.arange(c, dtype=F32)
    diff = pos[:, None] - pos[None, :]
    lgd = lg[:, :, None, None]
    fmask = jnp.where(diff >= 0, jnp.exp(lgd * jnp.maximum(diff, 0.0)), 0.0)
    bmask = jnp.where(diff <= 0, jnp.exp(lgd * jnp.maximum(-diff, 0.0)), 0.0)
    dm = jnp.stack([fmask[0], bmask[1]])
    lgc = lg[:, :, None, None]
    qd = jnp.stack([jnp.exp(lgc[0] * (pos + 1.0)[None, :, None]), jnp.exp(lgc[1] * (c - pos)[None, :, None])])
    kd = jnp.stack([jnp.exp(lgc[0] * (c - 1.0 - pos)[None, :, None]), jnp.exp(lgc[1] * pos[None, :, None])])
    cd = jnp.exp(lgc * c)
    return dm, qd, kd, cd


def _dispatch_plan(meta, counts, tm, n_tiles):
    cnt = counts[:, 0].astype(jnp.int32)
    tiles = (cnt + tm - 1) // tm
    tile_end = jnp.cumsum(tiles)
    start = (tile_end - tiles) * tm
    e = meta[0:2].astype(jnp.int32)
    start_e = sum(jnp.where(e == k, start[k], 0) for k in range(N_EXPERTS))
    pos = start_e + meta[2:4].astype(jnp.int32)
    tile_ids = jnp.arange(n_tiles, dtype=jnp.int32)
    n_valid = tile_end[-1:].astype(jnp.int32)
    last_valid = jnp.minimum(tile_ids, n_valid[0] - 1)
    te = jnp.sum((last_valid[:, None] >= tile_end[None, :]).astype(jnp.int32), axis=1)
    te = jnp.minimum(te, N_EXPERTS - 1).astype(jnp.int32)
    zero_start = jnp.where(tiles > 0, (tile_end - 1) * tm, -1).astype(jnp.int32)
    return pos, te, n_valid, jnp.concatenate([zero_start, n_valid])


def kernel(x, c, ctx, c_ctx, ada_w, ada_b, norm_mix, norm_ffn, norm_final, ev_w_in, gla_w_gate2, gla_b_gate2, gla_norm, mla_q_norm, mla_w_uq, mla_kv_norm, mla_w_ukv, ev_w_out, ffn_w_gate, ffn_w_up, ffn_w_down, od_w_in, ret_log_decay, ret_norm, od_w_out, moe_router, moe_w_gate, moe_w_up, moe_w_down):
    B, T, D = x.shape
    TC = ctx.shape[1]
    depth = ada_w.shape[0]
    assert depth == 2 and T % GRID_W == 0 and B <= 7
    rows = T // GRID_W
    xl = x.reshape(B * T, D)
    xc = ctx.reshape(B * TC, D)

    c_rows = jnp.concatenate([c, c_ctx[None, :], jnp.zeros((8 - B - 1, D), F32)], axis=0)
    m_all = _ada(c_rows, ada_w, ada_b).reshape(depth, 8, 6, 1, D)
    mods = [[(m_all[i, :B, k], m_all[i, B:B + 1, k]) for k in range(6)] for i in range(depth)]

    (sh_a, sc_a, g_a, sh_f, sc_f, g_f) = mods[0]
    w_main, w_small, wg2, bg2, wq, wkv = _even_weights(ev_w_in[0], gla_w_gate2[0], gla_b_gate2[0],
                                                       mla_w_uq[0], mla_w_ukv[0])
    tn_ev = _tile(EV_MAIN, 768)
    main_l, small_l = _nmm(xl, norm_mix[0], 1.0 + sc_a[0], sh_a[0], w_main, rows_per_group=T, tn=tn_ev,
                           w_small=w_small)
    main_c, small_c = _nmm(xc, norm_mix[0], 1.0 + sc_a[1], sh_a[1], w_main, rows_per_group=B * TC, tn=tn_ev,
                           w_small=w_small)

    s0 = jnp.zeros((B, GLA_HEADS, GLA_DV, GLA_DK), F32)
    ocf, ocb, s_f, s_b = _gla(main_c, small_c, wg2, bg2, s0, s0, batch=B, t=TC)
    olf, olb, _, _ = _gla(main_l, small_l, wg2, bg2, s_f, s_b, batch=B, t=T)

    (cs_m, sn_m), (cs_1, sn_0) = _rope_tables(rows, MLA_ROPE, TC)
    q_l = _mla_q(main_l, mla_q_norm[0], wq, cs_m, sn_m, t=T)
    q_c = _mla_q(main_c, mla_q_norm[0], wq, cs_1, sn_0, t=TC)
    k_l, v_l = _mla_kv(main_l, mla_kv_norm[0], wkv, cs_m, sn_m, t=T)
    k_c, v_c = _mla_kv(main_c, mla_kv_norm[0], wkv, cs_1, sn_0, t=TC)
    mla_l = _attn(q_l, k_l, v_l, k_c, v_c)
    mla_c = _attn(q_c, k_c, v_c)

    w_out = ev_w_out[0].astype(BF16)
    gr_blk = EV_GR // (GLA_HEADS * GLA_DV)
    x1l = _mixout(olf, olb, main_l, gr_blk, gla_norm[0], mla_l, w_out, xl, g_a[0], rows_per_group=T,
                  n_heads=GLA_HEADS, dv=GLA_DV, center=False)
    x1c = _mixout(ocf, ocb, main_c, gr_blk, gla_norm[0], mla_c, w_out, xc, g_a[1], rows_per_group=B * TC,
                  n_heads=GLA_HEADS, dv=GLA_DV, center=False)

    fwg, fwu, fwd_ = ffn_w_gate.astype(BF16), ffn_w_up.astype(BF16), ffn_w_down.astype(BF16)
    tm_l = _tile(T, FFN_ROWS)
    x2l = _ffn(x1l, fwg, fwu, fwd_, jnp.zeros((B * T // tm_l,), jnp.int32), jnp.full((1,), B * T // tm_l, jnp.int32),
               tm=tm_l, norm=(norm_ffn[0], 1.0 + sc_f[0], sh_f[0], g_f[0], T))
    tm_c = _tile(B * TC, FFN_ROWS)
    x2c = _ffn(x1c, fwg, fwu, fwd_, jnp.zeros((B * TC // tm_c,), jnp.int32),
               jnp.full((1,), B * TC // tm_c, jnp.int32), tm=tm_c,
               norm=(norm_ffn[0], 1.0 + sc_f[1], sh_f[1], g_f[1], B * TC))

    (sh_a, sc_a, g_a, sh_f, sc_f, g_f) = mods[1]
    w_odd = _odd_weights(od_w_in[0])
    hk = RET_HEADS * RET_DK
    tn_od = _tile(4 * hk, 1024)
    rope_l, rope_c = _rope_tables(rows, RET_DK, B * TC)
    m2l = _nmm(x2l, norm_mix[1], 1.0 + sc_a[0], sh_a[0], w_odd, rows_per_group=T, tn=tn_od, rope=rope_l,
               rope_cols=2 * hk)
    m2c = _nmm(x2c, norm_mix[1], 1.0 + sc_a[1], sh_a[1], w_odd, rows_per_group=B * TC, tn=tn_od, rope=rope_c,
               rope_cols=2 * hk)
    tables = _ret_tables(ret_log_decay[0])
    s0 = jnp.zeros((B, RET_HEADS, RET_DK, RET_DV), F32)
    _, _, s_f, s_b = _ret(m2c, tables, s0, s0, batch=B, t=TC)
    orf, orb, _, _ = _ret(m2l, tables, s_f, s_b, batch=B, t=T)
    x3 = _mixout(orf, orb, m2l, 3, ret_norm[0], None, od_w_out[0].astype(BF16), x2l, g_a[0], rows_per_group=T,
                 n_heads=RET_HEADS, dv=RET_DV, center=True)

    wr = jnp.concatenate([moe_router[0], jnp.zeros((D, 128 - N_EXPERTS), F32)], axis=1)
    meta, gate, counts = _router(x3, norm_ffn[1], 1.0 + sc_f[0], sh_f[0], wr, rows_per_group=T)
    tm_e = _tile(T, MOE_ROWS)
    n_tiles = (2 * B * T) // tm_e + N_EXPERTS
    pos, te, n_valid, zero_plan = _dispatch_plan(meta, counts, tm_e, n_tiles)
    tm_r = _tile(T, ROUTER_ROWS)
    pos_t = jnp.transpose(pos.reshape(2, -1, tm_r), (1, 0, 2))
    hs = _scatter_rows(x3, norm_ffn[1], 1.0 + sc_f[0], sh_f[0], pos_t, zero_plan, n_tiles=n_tiles,
                       group_rows=tm_e, rows_per_group=T)
    ys = _ffn(hs, moe_w_gate[0].astype(BF16), moe_w_up[0].astype(BF16), moe_w_down[0].astype(BF16), te, n_valid,
              tm=tm_e)
    out = _combine(ys, pos_t, x3, gate, g_f[0], norm_final, rows_per_group=T)
    return out.reshape(B, T, D)
```

```python
import functools

import jax
import jax.numpy as jnp
import numpy as np
from jax import lax
from jax.experimental import pallas as pl
from jax.experimental.pallas import tpu as pltpu

F32 = jnp.float32
BF16 = jnp.bfloat16
EPS = 1e-6
ROPE_BASE = 10000.0
GRID_W = 64

GLA_HEADS, GLA_DK, GLA_DV, GLA_RANK, GLA_GATE_NORM, GLA_CHUNK = 4, 128, 256, 16, 16.0, 64
MLA_HEADS, MLA_Q_RANK, MLA_KV_RANK, MLA_NOPE, MLA_ROPE, MLA_DV = 8, 768, 512, 128, 64, 128
MLA_DQK = MLA_NOPE + MLA_ROPE
MLA_SCALE = MLA_DQK ** -0.5
MLA_Q_SCALE = MLA_SCALE * 1.4426950408889634
RET_HEADS, RET_DK, RET_DV = 8, 256, 256
RET_CHUNK = 256
N_EXPERTS = 8

EV_GV, EV_GR, EV_GQ, EV_GK, EV_MQ, EV_KR, EV_MKV, EV_MAIN = 0, 1024, 2048, 2560, 3072, 3840, 4096, 4608
EV_SMALL = 128

V7X_VMEM_BYTES = 64 << 20
VMEM_HEADROOM_BYTES = 6 << 20
FFN_ROWS = 512
MOE_ROWS = 1024
ROUTER_ROWS = 512
ROW_DMA_UNROLL = 8


def _vmem(nbytes):
    return int(min(V7X_VMEM_BYTES - VMEM_HEADROOM_BYTES, max(32 << 20, nbytes + (8 << 20))))


def _tile(n, pref, align=128):
    if n <= pref:
        return n
    t = (pref // align) * align
    while t >= align:
        if n % t == 0:
            return t
        t -= align
    return n


def _split_bf16(a):
    hi = a.astype(BF16)
    lo = (a - hi.astype(F32)).astype(BF16)
    return hi, lo


def _dot(a, b):
    return jnp.dot(a, b, preferred_element_type=F32)


def _dot3(a, b):
    ah, al = _split_bf16(a)
    bh, bl = _split_bf16(b)
    return _dot(ah, bh) + _dot(al, bh) + _dot(ah, bl)


def _dot_nt(a, b):
    return lax.dot_general(a, b, (((1,), (1,)), ((), ())), preferred_element_type=F32)


def _dot_tn(a, b):
    return lax.dot_general(a, b, (((0,), (0,)), ((), ())), preferred_element_type=F32)


def _silu(x):
    return x * (1.0 / (1.0 + jnp.exp(-x)))


def _log_sigmoid(z):
    return -(jnp.maximum(-z, 0.0) + jnp.log(1.0 + jnp.exp(-jnp.abs(z))))


def _rms(x, g):
    return x * lax.rsqrt(jnp.mean(x * x, axis=-1, keepdims=True) + EPS) * g


def _pack_rows(y):
    half = y.shape[1] // 2
    lo = lax.bitcast_convert_type(y[:, :half].astype(BF16).astype(F32), jnp.uint32)
    hi = lax.bitcast_convert_type(y[:, half:].astype(BF16).astype(F32), jnp.uint32)
    return lax.shift_right_logical(lo, jnp.uint32(16)) | hi


def _packed_shape(m, d):
    return (m, d // 2), jnp.uint32


def _unpack_rows(w):
    lo = lax.bitcast_convert_type(lax.shift_left(w, jnp.uint32(16)), F32)
    hi = lax.bitcast_convert_type(w & jnp.uint32(0xFFFF0000), F32)
    return jnp.concatenate([lo, hi], axis=1)


def _ada_kernel(c_ref, w_ref, b_ref, o_ref):
    o_ref[0] = _dot3(_silu(c_ref[...]), w_ref[0]) + b_ref[0]


def _ada(c_rows, w, b):
    nl, d, n = w.shape
    tn = _tile(n, 1536)
    return pl.pallas_call(
        _ada_kernel,
        grid=(nl, n // tn),
        in_specs=[pl.BlockSpec((8, d), lambda l, j: (0, 0)),
                  pl.BlockSpec((1, d, tn), lambda l, j: (l, 0, j)),
                  pl.BlockSpec((1, 1, tn), lambda l, j: (l, 0, j))],
        out_specs=pl.BlockSpec((1, 8, tn), lambda l, j: (l, 0, j)),
        out_shape=jax.ShapeDtypeStruct((nl, 8, n), F32),
        compiler_params=pltpu.CompilerParams(dimension_semantics=("parallel", "parallel"),
                                             vmem_limit_bytes=_vmem(2 * d * tn * 4 * 3)),
        name="ada_mod",
    )(c_rows, w, b.reshape(nl, 1, n))


def _nmm_kernel(*refs, has_small, rope_tiles, heads_per_tile):
    x_ref, g_ref, sc_ref, sh_ref, w_ref = refs[:5]
    k = 5
    if has_small:
        w2_ref = refs[k]; k += 1
    if rope_tiles:
        cs_ref, sn_ref = refs[k], refs[k + 1]; k += 2
    o_ref = refs[k]; k += 1
    if has_small:
        o2_ref = refs[k]; k += 1
    hn_ref = refs[k]
    j = pl.program_id(1)

    @pl.when(j == 0)
    def _():
        y = _rms(x_ref[...], g_ref[...]) * sc_ref[0] + sh_ref[0]
        hn_ref[...] = y.astype(BF16)
        if has_small:
            o2_ref[...] = _dot3(y, w2_ref[...])

    acc = _dot(hn_ref[...], w_ref[...])
    if rope_tiles:
        @pl.when(j < rope_tiles)
        def _():
            cs = cs_ref[...]
            sn = sn_ref[...]
            half = cs.shape[1] // 2
            outs = []
            for h in range(heads_per_tile):
                r = acc[:, h * 2 * half:(h + 1) * 2 * half]
                rs = jnp.concatenate([r[:, half:], r[:, :half]], axis=1)
                outs.append(r * cs + rs * sn)
            o_ref[...] = jnp.concatenate(outs, axis=1).astype(o_ref.dtype)

        @pl.when(j >= rope_tiles)
        def _():
            o_ref[...] = acc.astype(o_ref.dtype)
    else:
        o_ref[...] = acc.astype(o_ref.dtype)


def _nmm(x, g, sc, sh, w, *, rows_per_group, tn, w_small=None, rope=None, rope_cols=0):
    m, d = x.shape
    n = w.shape[1]
    tm = _tile(rows_per_group, 1024)
    gpt = rows_per_group // tm
    grp = lambda i, j: (i // gpt, 0, 0)
    in_specs = [pl.BlockSpec((tm, d), lambda i, j: (i, 0)),
                pl.BlockSpec((1, d), lambda i, j: (0, 0)),
                pl.BlockSpec((1, 1, d), grp),
                pl.BlockSpec((1, 1, d), grp),
                pl.BlockSpec((d, tn), lambda i, j: (0, j))]
    args = [x, g.reshape(1, d), sc, sh, w]
    out_specs = [pl.BlockSpec((tm, tn), lambda i, j: (i, j))]
    out_shape = [jax.ShapeDtypeStruct((m, n), BF16)]
    if w_small is not None:
        in_specs.append(pl.BlockSpec((d, EV_SMALL), lambda i, j: (0, 0)))
        args.append(w_small)
        out_specs.append(pl.BlockSpec((tm, EV_SMALL), lambda i, j: (i, 0)))
        out_shape.append(jax.ShapeDtypeStruct((m, EV_SMALL), F32))
    rope_tiles = heads_per_tile = 0
    if rope is not None:
        cs, sn = rope
        t_rows, hd = cs.shape
        rope_tiles, heads_per_tile = rope_cols // tn, tn // hd
        tpb = t_rows // tm
        in_specs += [pl.BlockSpec((tm, hd), lambda i, j: (i % tpb, 0)),
                     pl.BlockSpec((tm, hd), lambda i, j: (i % tpb, 0))]
        args += [cs, sn]
    kern = functools.partial(_nmm_kernel, has_small=w_small is not None, rope_tiles=rope_tiles,
                             heads_per_tile=heads_per_tile)
    res = pl.pallas_call(
        kern,
        grid=(m // tm, n // tn),
        in_specs=in_specs,
        out_specs=out_specs,
        out_shape=out_shape,
        scratch_shapes=[pltpu.VMEM((tm, d), BF16)],
        compiler_params=pltpu.CompilerParams(
            dimension_semantics=("parallel", "arbitrary"),
            vmem_limit_bytes=_vmem(2 * tm * d * 4 + tm * d * 2 + 2 * d * tn * 2 + 5 * tm * tn * 4)),
        name="norm_proj",
    )(*args)
    return res if w_small is not None else res[0]


def _gla_kernel(mf_q, mf_k, mf_v, sm_f, mb_q, mb_k, mb_v, sm_b, wg_ref, bg_ref, s0f_ref, s0b_ref,
                of_ref, ob_ref, sf_ref, sb_ref, st_ref, *, n_sub):
    s = pl.program_id(1)
    c = GLA_CHUNK

    @pl.when(s == 0)
    def _():
        st_ref[0] = s0f_ref[0]
        st_ref[1] = s0b_ref[0]

    rows = n_sub * c
    row = lax.broadcasted_iota(jnp.int32, (c, c), 0)
    col = lax.broadcasted_iota(jnp.int32, (c, c), 1)
    brow = lax.broadcasted_iota(jnp.int32, (rows, rows), 0)
    bcol = lax.broadcasted_iota(jnp.int32, (rows, rows), 1)
    same_chunk = (brow // c) == (bcol // c)
    for d, (q_ref, k_ref, v_ref, sm_ref, o_ref) in enumerate(
            ((mf_q, mf_k, mf_v, sm_f, of_ref), (mb_q, mb_k, mb_v, sm_b, ob_ref))):
        keep = (col <= row) if d == 0 else (col >= row)
        bkeep = jnp.logical_and(same_chunk, (bcol <= brow) if d == 0 else (bcol >= brow))
        tri = jnp.where(bkeep, 1.0, 0.0).astype(BF16)
        z = _dot3(sm_ref[...], wg_ref[d]) + bg_ref[d]
        lh, ll = _split_bf16(_log_sigmoid(z) * (1.0 / GLA_GATE_NORM))
        b_all = _dot(tri, lh) + _dot(tri, ll)
        order = range(n_sub) if d == 0 else range(n_sub - 1, -1, -1)
        for ci in order:
            r0 = ci * c
            for h in range(GLA_HEADS):
                b = b_all[r0:r0 + c, h * GLA_DK:(h + 1) * GLA_DK]
                b_last = b[c - 1:c, :] if d == 0 else b[0:1, :]
                q = q_ref[r0:r0 + c, h * GLA_DK:(h + 1) * GLA_DK].astype(F32) * (GLA_DK ** -0.5)
                k = k_ref[r0:r0 + c, h * GLA_DK:(h + 1) * GLA_DK].astype(F32)
                v = v_ref[r0:r0 + c, h * GLA_DV:(h + 1) * GLA_DV]
                qe = (q * jnp.exp(b)).astype(BF16)
                kd = (k * jnp.exp(-b)).astype(BF16)
                kl = (k * jnp.exp(b_last - b)).astype(BF16)
                att = jnp.where(keep, _dot_nt(qe, kd), 0.0).astype(BF16)
                st = st_ref[d, h]
                o = _dot(att, v) + _dot_nt(qe, st.astype(BF16))
                o_ref[r0:r0 + c, h * GLA_DV:(h + 1) * GLA_DV] = o.astype(o_ref.dtype)
                st_ref[d, h] = jnp.exp(b_last) * st + _dot_tn(v, kl)

    @pl.when(s == pl.num_programs(1) - 1)
    def _():
        sf_ref[0] = st_ref[0]
        sb_ref[0] = st_ref[1]


def _gla(main, small, wg, bg, s0f, s0b, *, batch, t):
    rows = _tile(t, 256, GLA_CHUNK)
    ns = t // rows
    hv, hk = GLA_HEADS * GLA_DV, GLA_HEADS * GLA_DK
    fwd = lambda blk: (lambda b, s: (b * ns + s, blk))
    bwd = lambda blk: (lambda b, s: (b * ns + ns - 1 - s, blk))
    st_spec = pl.BlockSpec((1, GLA_HEADS, GLA_DV, GLA_DK), lambda b, s: (b, 0, 0, 0))
    st_shape = jax.ShapeDtypeStruct((batch, GLA_HEADS, GLA_DV, GLA_DK), F32)

    def io(m):
        return [pl.BlockSpec((rows, hk), m(EV_GQ // hk)), pl.BlockSpec((rows, hk), m(EV_GK // hk)),
                pl.BlockSpec((rows, hv), m(EV_GV // hv)), pl.BlockSpec((rows, EV_SMALL), m(0))]

    return pl.pallas_call(
        functools.partial(_gla_kernel, n_sub=rows // GLA_CHUNK),
        grid=(batch, ns),
        in_specs=io(fwd) + io(bwd) + [
            pl.BlockSpec((2, EV_SMALL, hk), lambda b, s: (0, 0, 0)),
            pl.BlockSpec((2, 1, hk), lambda b, s: (0, 0, 0)),
            st_spec, st_spec],
        out_specs=[pl.BlockSpec((rows, hv), fwd(0)), pl.BlockSpec((rows, hv), bwd(0)), st_spec, st_spec],
        out_shape=[jax.ShapeDtypeStruct((batch * t, hv), BF16), jax.ShapeDtypeStruct((batch * t, hv), BF16),
                   st_shape, st_shape],
        scratch_shapes=[pltpu.VMEM((2, GLA_HEADS, GLA_DV, GLA_DK), F32)],
        compiler_params=pltpu.CompilerParams(dimension_semantics=("parallel", "arbitrary"),
                                             vmem_limit_bytes=_vmem(16 << 20)),
        name="gla_scan",
    )(main, main, main, small, main, main, main, small, wg, bg, s0f, s0b)


def _mla_proj_kernel(xq_ref, xkv_ref, kr_ref, gq_ref, gkv_ref, wq_ref, wkv_ref, cs_ref, sn_ref,
                     q_ref, k_ref, v_ref):
    cs, sn = cs_ref[...], sn_ref[...]
    hq = _rms(xq_ref[...].astype(F32), gq_ref[...]).astype(BF16)
    hkv = _rms(xkv_ref[...].astype(F32), gkv_ref[...]).astype(BF16)
    kr = kr_ref[...].astype(F32)
    krr = kr[:, :MLA_ROPE] * cs + kr[:, MLA_ROPE:2 * MLA_ROPE] * sn
    for h in range(MLA_HEADS):
        r = _dot(hq, wq_ref[h])
        qr = r[:, MLA_NOPE:MLA_NOPE + MLA_ROPE] * cs + r[:, MLA_NOPE + MLA_ROPE:] * sn
        q_ref[0, h] = (jnp.concatenate([r[:, :MLA_NOPE], qr], axis=1) * MLA_Q_SCALE).astype(q_ref.dtype)
        r = _dot(hkv, wkv_ref[h])
        k_ref[0, h] = jnp.concatenate([r[:, :MLA_NOPE], krr], axis=1).astype(k_ref.dtype)
        v_ref[0, h] = r[:, MLA_NOPE:].astype(v_ref.dtype)


def _mla_proj(main, gq, gkv, wq, wkv, cs, sn, *, t):
    m = main.shape[0]
    tm = _tile(t, 512)
    tpb = t // tm
    out = lambda i: (i // tpb, 0, i % tpb, 0)
    rope = lambda i: (i % tpb, 0)
    full = lambda a: pl.BlockSpec(a.shape, lambda i: (0,) * a.ndim)
    gq, gkv = gq.reshape(1, -1), gkv.reshape(1, -1)
    return pl.pallas_call(
        _mla_proj_kernel,
        grid=(m // tm,),
        in_specs=[pl.BlockSpec((tm, MLA_Q_RANK), lambda i: (i, EV_MQ // MLA_Q_RANK)),
                  pl.BlockSpec((tm, MLA_KV_RANK), lambda i: (i, EV_MKV // MLA_KV_RANK)),
                  pl.BlockSpec((tm, 256), lambda i: (i, EV_KR // 256)),
                  full(gq), full(gkv), full(wq), full(wkv),
                  pl.BlockSpec((tm, MLA_ROPE), rope), pl.BlockSpec((tm, MLA_ROPE), rope)],
        out_specs=[pl.BlockSpec((1, MLA_HEADS, tm, MLA_DQK), out),
                   pl.BlockSpec((1, MLA_HEADS, tm, MLA_DQK), out),
                   pl.BlockSpec((1, MLA_HEADS, tm, MLA_DV), out)],
        out_shape=[jax.ShapeDtypeStruct((m // t, MLA_HEADS, t, MLA_DQK), BF16),
                   jax.ShapeDtypeStruct((m // t, MLA_HEADS, t, MLA_DQK), BF16),
                   jax.ShapeDtypeStruct((m // t, MLA_HEADS, t, MLA_DV), BF16)],
        compiler_params=pltpu.CompilerParams(dimension_semantics=("parallel",),
                                             vmem_limit_bytes=_vmem(40 * MLA_HEADS * tm * 256)),
        name="mla_proj",
    )(main, main, main, gq, gkv, wq, wkv, cs, sn)


ATTN_ROWS = 256


def _attn_kernel(*refs, n_main, has_extra):
    if has_extra:
        q_ref, k_ref, v_ref, kx_ref, vx_ref, o_ref, m_ref, acc_ref = refs
    else:
        q_ref, k_ref, v_ref, o_ref, m_ref, acc_ref = refs
    kv = pl.program_id(3)

    @pl.when(kv == 0)
    def _():
        m_ref[...] = jnp.full(m_ref.shape, -jnp.inf, F32)
        acc_ref[...] = jnp.zeros(acc_ref.shape, F32)

    tq = q_ref.shape[2]
    rc = min(ATTN_ROWS, tq)

    def step(key_values):
        ks = [k for k, _ in key_values]
        v_ones = [jnp.concatenate([v, jnp.ones(v.shape, v.dtype)], axis=1) for _, v in key_values]
        for r in range(tq // rc):
            rows = slice(r * rc, (r + 1) * rc)
            q = q_ref[0, 0, rows, :]
            ss = [_dot_nt(q, k) for k in ks]
            m_old = m_ref[rows, :]
            m_new = m_old
            for s in ss:
                m_new = jnp.maximum(m_new, jnp.max(s, axis=-1, keepdims=True))
            pv = sum(_dot(jnp.exp2(s - m_new).astype(BF16), vo) for s, vo in zip(ss, v_ones))
            acc_ref[rows, :] = jnp.exp2(m_old - m_new) * acc_ref[rows, :] + pv
            m_ref[rows, :] = m_new

    main = (k_ref[0, 0], v_ref[0, 0])
    if has_extra:
        if n_main > 1:
            pl.when(kv < n_main - 1)(lambda: step([main]))
        pl.when(kv == n_main - 1)(lambda: step([main, (kx_ref[0, 0], vx_ref[0, 0])]))
    else:
        step([main])

    @pl.when(kv == pl.num_programs(3) - 1)
    def _():
        acc = acc_ref[...]
        o_ref[...] = (acc[:, :MLA_DV] / acc[:, MLA_DV:]).astype(o_ref.dtype)


def _attn(q, k, v, kx=None, vx=None, *, tq_pref=2048, tk_pref=8192):
    b, h, tq_all, dqk = q.shape
    tk_all = k.shape[2]
    tq, tk = _tile(tq_all, tq_pref), _tile(tk_all, tk_pref)
    nq, n_main = tq_all // tq, tk_all // tk
    main = lambda bi, hi, i, j: (bi, hi, j, 0)
    in_specs = [pl.BlockSpec((1, 1, tq, dqk), lambda bi, hi, i, j: (bi, hi, i, 0)),
                pl.BlockSpec((1, 1, tk, dqk), main), pl.BlockSpec((1, 1, tk, MLA_DV), main)]
    args = [q, k, v]
    if kx is not None:
        tx = kx.shape[2]
        in_specs += [pl.BlockSpec((1, 1, tx, dqk), lambda bi, hi, i, j: (bi, hi, 0, 0)),
                     pl.BlockSpec((1, 1, tx, MLA_DV), lambda bi, hi, i, j: (bi, hi, 0, 0))]
        args += [kx, vx]
    return pl.pallas_call(
        functools.partial(_attn_kernel, n_main=n_main, has_extra=kx is not None),
        grid=(b, h, nq, n_main),
        in_specs=in_specs,
        out_specs=pl.BlockSpec((tq, MLA_DV), lambda bi, hi, i, j: (bi * nq + i, hi)),
        out_shape=jax.ShapeDtypeStruct((b * tq_all, h * MLA_DV), BF16),
        scratch_shapes=[pltpu.VMEM((tq, 1), F32), pltpu.VMEM((tq, 2 * MLA_DV), F32)],
        compiler_params=pltpu.CompilerParams(
            dimension_semantics=("parallel", "parallel", "parallel", "arbitrary"),
            vmem_limit_bytes=_vmem(5 * min(ATTN_ROWS, tq) * tk * 4 + 8 * tk * 256 * 2)),
        name="mla_attention",
    )(*args)


def _mixout_kernel(*refs, n_heads, dv, center, has_mla):
    of_ref, ob_ref, gate_ref, g_ref = refs[:4]
    k = 4
    if has_mla:
        mla_ref = refs[k]; k += 1
    w_ref, x_ref, ga_ref, o_ref = refs[k:k + 4]
    o = of_ref[...].astype(F32) + ob_ref[...].astype(F32)
    g = g_ref[...]
    parts = []
    for h in range(n_heads):
        oh = o[:, h * dv:(h + 1) * dv]
        if center:
            oh = oh - jnp.mean(oh, axis=-1, keepdims=True)
        parts.append(oh * lax.rsqrt(jnp.mean(oh * oh, axis=-1, keepdims=True) + EPS) * g)
    lat = jnp.concatenate(parts, axis=1) * _silu(gate_ref[...].astype(F32))
    lhs = lat.astype(BF16)
    if has_mla:
        lhs = jnp.concatenate([lhs, mla_ref[...]], axis=1)
    o_ref[...] = x_ref[...] + ga_ref[0] * _dot(lhs, w_ref[...])


def _mixout(o_f, o_b, main, gate_blk, g, mla, w, x, ga, *, rows_per_group, n_heads, dv, center):
    m, d = x.shape
    hw = n_heads * dv
    tm = _tile(rows_per_group, 512)
    gpt = rows_per_group // tm
    in_specs = [pl.BlockSpec((tm, hw), lambda i: (i, 0)), pl.BlockSpec((tm, hw), lambda i: (i, 0)),
                pl.BlockSpec((tm, hw), lambda i: (i, gate_blk)), pl.BlockSpec((1, dv), lambda i: (0, 0))]
    args = [o_f, o_b, main, g.reshape(1, dv)]
    if mla is not None:
        in_specs.append(pl.BlockSpec((tm, mla.shape[1]), lambda i: (i, 0)))
        args.append(mla)
    in_specs += [pl.BlockSpec(w.shape, lambda i: (0, 0)), pl.BlockSpec((tm, d), lambda i: (i, 0)),
                 pl.BlockSpec((1, 1, d), lambda i: (i // gpt, 0, 0))]
    args += [w, x, ga]
    return pl.pallas_call(
        functools.partial(_mixout_kernel, n_heads=n_heads, dv=dv, center=center, has_mla=mla is not None),
        grid=(m // tm,),
        in_specs=in_specs,
        out_specs=pl.BlockSpec((tm, d), lambda i: (i, 0)),
        out_shape=jax.ShapeDtypeStruct((m, d), F32),
        compiler_params=pltpu.CompilerParams(
            dimension_semantics=("parallel",),
            vmem_limit_bytes=_vmem(2 * w.size * 2 + 4 * tm * d * 4 + 8 * tm * hw * 2 + 6 * tm * hw * 4)),
        name="mixer_out",
    )(*args)


def _ffn_kernel(te_ref, nv_ref, *refs, prenorm):
    if prenorm:
        x_ref, g_ref, sc_ref, sh_ref, wg_ref, wu_ref, wd_ref, gf_ref, o_ref, hn_ref, acc_ref = refs
    else:
        x_ref, wg_ref, wu_ref, wd_ref, o_ref, hn_ref, acc_ref = refs
    i, j = pl.program_id(0), pl.program_id(1)
    valid = i < nv_ref[0]

    @pl.when(jnp.logical_and(valid, j == 0))
    def _():
        if prenorm:
            y = _rms(x_ref[...], g_ref[...]) * sc_ref[0] + sh_ref[0]
        else:
            y = _unpack_rows(x_ref[...])
        hn_ref[...] = y.astype(BF16)
        acc_ref[...] = jnp.zeros(acc_ref.shape, F32)

    @pl.when(valid)
    def _():
        hn = hn_ref[...]
        a = (_silu(_dot(hn, wg_ref[0])) * _dot(hn, wu_ref[0])).astype(BF16)
        acc_ref[...] += _dot(a, wd_ref[0])

    last = j == pl.num_programs(1) - 1

    @pl.when(jnp.logical_and(valid, last))
    def _():
        if prenorm:
            o_ref[...] = x_ref[...] + gf_ref[0] * acc_ref[...]
        else:
            o_ref[...] = _pack_rows(acc_ref[...])

    @pl.when(jnp.logical_and(jnp.logical_not(valid), last))
    def _():
        o_ref[...] = jnp.zeros(o_ref.shape, o_ref.dtype)


def _ffn(x, wg, wu, wd, tile_expert, n_valid, *, tm, norm=None):
    p = x.shape[0]
    d = wg.shape[1]
    f = wg.shape[2]
    tf = _tile(f, 512)
    nt = p // tm
    wmap_in = lambda i, j, te, nv: (te[i], 0, j)
    wmap_out = lambda i, j, te, nv: (te[i], j, 0)
    row = lambda i, j, te, nv: (i, 0)
    in_specs = [pl.BlockSpec((tm, x.shape[1]), lambda i, j, te, nv: (jnp.minimum(i, nv[0] - 1), 0))]
    args = [x]
    if norm is not None:
        g, sc, sh, gf, rpg = norm
        gpt = rpg // tm
        grp = lambda i, j, te, nv: (i // gpt, 0, 0)
        in_specs += [pl.BlockSpec((1, d), lambda i, j, te, nv: (0, 0)), pl.BlockSpec((1, 1, d), grp),
                     pl.BlockSpec((1, 1, d), grp)]
        args += [g.reshape(1, d), sc, sh]
    in_specs += [pl.BlockSpec((1, d, tf), wmap_in), pl.BlockSpec((1, d, tf), wmap_in),
                 pl.BlockSpec((1, tf, d), wmap_out)]
    args += [wg, wu, wd]
    if norm is not None:
        in_specs.append(pl.BlockSpec((1, 1, d), grp))
        args.append(gf)
    return pl.pallas_call(
        functools.partial(_ffn_kernel, prenorm=norm is not None),
        grid_spec=pltpu.PrefetchScalarGridSpec(
            num_scalar_prefetch=2,
            grid=(nt, f // tf),
            in_specs=in_specs,
            out_specs=pl.BlockSpec((tm, x.shape[1]), row),
            scratch_shapes=[pltpu.VMEM((tm, d), BF16), pltpu.VMEM((tm, d), F32)]),
        out_shape=jax.ShapeDtypeStruct(x.shape, x.dtype),
        compiler_params=pltpu.CompilerParams(
            dimension_semantics=("parallel", "arbitrary"),
            vmem_limit_bytes=_vmem(4 * tm * x.shape[1] * 4 + tm * d * 6 + 6 * d * tf * 2 + 5 * tm * tf * 4)),
        name="swiglu_ffn",
    )(tile_expert, n_valid, *args)


def _ret_kernel(qf_ref, kf_ref, vf_ref, qb_ref, kb_ref, vb_ref, dm_ref, qd_ref, kd_ref, cd_ref,
                s0f_ref, s0b_ref, of_ref, ob_ref, sf_ref, sb_ref, st_ref, *, n_sub):
    s = pl.program_id(1)
    c = dm_ref.shape[2]

    @pl.when(s == 0)
    def _():
        st_ref[0] = s0f_ref[0]
        st_ref[1] = s0b_ref[0]

    for d, (q_ref, k_ref, v_ref, o_ref) in enumerate(((qf_ref, kf_ref, vf_ref, of_ref),
                                                       (qb_ref, kb_ref, vb_ref, ob_ref))):
        order = range(n_sub) if d == 0 else range(n_sub - 1, -1, -1)
        for ci in order:
            r0 = ci * c
            for h in range(RET_HEADS):
                q = q_ref[r0:r0 + c, h * RET_DK:(h + 1) * RET_DK]
                k = (k_ref[r0:r0 + c, h * RET_DK:(h + 1) * RET_DK].astype(F32) * (RET_DK ** -0.5)).astype(BF16)
                v = v_ref[r0:r0 + c, h * RET_DV:(h + 1) * RET_DV]
                att = (_dot_nt(q, k) * dm_ref[d, h]).astype(BF16)
                st = st_ref[d, h]
                o = _dot(att, v) + qd_ref[d, h] * _dot(q, st.astype(BF16))
                o_ref[r0:r0 + c, h * RET_DV:(h + 1) * RET_DV] = o.astype(o_ref.dtype)
                vk = (v.astype(F32) * kd_ref[d, h]).astype(BF16)
                st_ref[d, h] = cd_ref[d, h] * st + _dot_tn(k, vk)

    @pl.when(s == pl.num_programs(1) - 1)
    def _():
        sf_ref[0] = st_ref[0]
        sb_ref[0] = st_ref[1]


def _ret(main, tables, s0f, s0b, *, batch, t):
    chunk = tables[0].shape[2]
    rows = _tile(t, 256, chunk)
    ns = t // rows
    hk, hv = RET_HEADS * RET_DK, RET_HEADS * RET_DV
    fwd = lambda blk: (lambda b, s: (b * ns + s, blk))
    bwd = lambda blk: (lambda b, s: (b * ns + ns - 1 - s, blk))
    st_spec = pl.BlockSpec((1, RET_HEADS, RET_DK, RET_DV), lambda b, s: (b, 0, 0, 0))
    st_shape = jax.ShapeDtypeStruct((batch, RET_HEADS, RET_DK, RET_DV), F32)
    io = lambda m: [pl.BlockSpec((rows, hk), m(0)), pl.BlockSpec((rows, hk), m(1)), pl.BlockSpec((rows, hv), m(2))]
    full = lambda a: pl.BlockSpec(a.shape, lambda b, s: (0,) * a.ndim)
    return pl.pallas_call(
        functools.partial(_ret_kernel, n_sub=rows // chunk),
        grid=(batch, ns),
        in_specs=io(fwd) + io(bwd) + [full(a) for a in tables] + [st_spec, st_spec],
        out_specs=[pl.BlockSpec((rows, hv), fwd(0)), pl.BlockSpec((rows, hv), bwd(0)), st_spec, st_spec],
        out_shape=[jax.ShapeDtypeStruct((batch * t, hv), BF16), jax.ShapeDtypeStruct((batch * t, hv), BF16),
                   st_shape, st_shape],
        scratch_shapes=[pltpu.VMEM((2, RET_HEADS, RET_DK, RET_DV), F32)],
        compiler_params=pltpu.CompilerParams(dimension_semantics=("parallel", "arbitrary"),
                                             vmem_limit_bytes=_vmem(46 << 20)),
        name="retention_scan",
    )(main, main, main, main, main, main, *tables, s0f, s0b)


def _router_kernel(x_ref, g_ref, sc_ref, sh_ref, wr_ref, meta_ref, gate_ref, cnt_ref, run_ref):
    i = pl.program_id(0)

    @pl.when(i == 0)
    def _():
        run_ref[...] = jnp.zeros(run_ref.shape, F32)

    y = _rms(x_ref[...], g_ref[...]) * sc_ref[0] + sh_ref[0]
    tm = y.shape[0]
    lt = jnp.transpose(_dot3(y, wr_ref[...]))[:N_EXPERTS, :]
    eid = lax.broadcasted_iota(jnp.int32, lt.shape, 0).astype(F32)
    v1 = jnp.max(lt, axis=0, keepdims=True)
    e1 = jnp.min(jnp.where(lt == v1, eid, float(N_EXPERTS)), axis=0, keepdims=True)
    lt2 = jnp.where(eid == e1, -jnp.inf, lt)
    v2 = jnp.max(lt2, axis=0, keepdims=True)
    e2 = jnp.min(jnp.where(lt2 == v2, eid, float(N_EXPERTS)), axis=0, keepdims=True)
    ex = jnp.exp(v2 - v1)
    w1 = 1.0 / (1.0 + ex)
    w2 = ex / (1.0 + ex)
    oh1 = jnp.where(eid == e1, 1.0, 0.0)
    oh2 = jnp.where(eid == e2, 1.0, 0.0)
    mem = oh1 + oh2
    r = lax.broadcasted_iota(jnp.int32, (tm, tm), 0)
    cidx = lax.broadcasted_iota(jnp.int32, (tm, tm), 1)
    before = jnp.where(r < cidx, 1.0, 0.0).astype(BF16)
    excl = _dot(mem.astype(BF16), before) + run_ref[...][:, :1]
    rk1 = jnp.sum(oh1 * excl, axis=0, keepdims=True)
    rk2 = jnp.sum(oh2 * excl, axis=0, keepdims=True)
    run_ref[...] = run_ref[...] + jnp.sum(mem, axis=1, keepdims=True)
    zero = jnp.zeros_like(e1)
    meta_ref[...] = jnp.concatenate([e1, e2, rk1, rk2, w1, w2, zero, zero], axis=0)
    wpad = jnp.concatenate([w1, w2, jnp.zeros((6, tm), F32)], axis=0)
    gate_ref[...] = jnp.transpose(jnp.concatenate([wpad] * 16, axis=0))
    cnt_ref[...] = run_ref[...]


def _router(x, g, sc, sh, wr, *, rows_per_group):
    m, d = x.shape
    tm = _tile(rows_per_group, ROUTER_ROWS)
    gpt = rows_per_group // tm
    grp = lambda i: (i // gpt, 0, 0)
    return pl.pallas_call(
        _router_kernel,
        grid=(m // tm,),
        in_specs=[pl.BlockSpec((tm, d), lambda i: (i, 0)), pl.BlockSpec((1, d), lambda i: (0, 0)),
                  pl.BlockSpec((1, 1, d), grp), pl.BlockSpec((1, 1, d), grp),
                  pl.BlockSpec((d, 128), lambda i: (0, 0))],
        out_specs=[pl.BlockSpec((8, tm), lambda i: (0, i)),
                   pl.BlockSpec((tm, 128), lambda i: (i, 0)), pl.BlockSpec((N_EXPERTS, 128), lambda i: (0, 0))],
        out_shape=[jax.ShapeDtypeStruct((8, m), F32),
                   jax.ShapeDtypeStruct((m, 128), F32), jax.ShapeDtypeStruct((N_EXPERTS, 128), F32)],
        scratch_shapes=[pltpu.VMEM((N_EXPERTS, 128), F32)],
        compiler_params=pltpu.CompilerParams(dimension_semantics=("arbitrary",),
                                             vmem_limit_bytes=_vmem(6 * tm * d * 4)),
        name="moe_router",
    )(x, g.reshape(1, d), sc, sh, wr)


def _scatter_kernel(zs_ref, pos_ref, x_ref, g_ref, sc_ref, sh_ref, o_ref, hbuf, sem, *, group_rows, n_tiles):
    tm = hbuf.shape[0]

    def zero_tile(start):
        start = pl.multiple_of(start, 8)
        for c in range(group_rows // tm):
            cp = pltpu.make_async_copy(hbuf, o_ref.at[pl.ds(start + c * tm, tm)], sem.at[0])
            cp.start()
            cp.wait()

    @pl.when(pl.program_id(0) == 0)
    def _():
        hbuf[...] = jnp.zeros(hbuf.shape, hbuf.dtype)
        for e in range(N_EXPERTS):
            pl.when(zs_ref[e] >= 0)(functools.partial(zero_tile, zs_ref[e]))
            tail = zs_ref[N_EXPERTS] + e
            pl.when(tail < n_tiles)(functools.partial(zero_tile, tail * group_rows))

    hbuf[...] = _pack_rows(_rms(x_ref[...], g_ref[...]) * sc_ref[0] + sh_ref[0])

    def copies(t):
        return (pltpu.make_async_copy(hbuf.at[t], o_ref.at[pos_ref[0, 0, t]], sem.at[0]),
                pltpu.make_async_copy(hbuf.at[t], o_ref.at[pos_ref[0, 1, t]], sem.at[1]))

    def issue(t, carry):
        for cp in copies(t):
            cp.start()
        return carry

    lax.fori_loop(0, tm, issue, 0, unroll=ROW_DMA_UNROLL)

    def drain(t, carry):
        for cp in copies(t):
            cp.wait()
        return carry

    lax.fori_loop(0, tm, drain, 0, unroll=ROW_DMA_UNROLL)


def _scatter_rows(x, g, sc, sh, pos, zero_plan, *, n_tiles, group_rows, rows_per_group):
    m, d = x.shape
    nt, _, tm = pos.shape
    assert group_rows % tm == 0
    gpt = rows_per_group // tm
    grp = lambda i, zs: (i // gpt, 0, 0)
    pshape, pdtype = _packed_shape(tm, d)
    return pl.pallas_call(
        functools.partial(_scatter_kernel, group_rows=group_rows, n_tiles=n_tiles),
        grid_spec=pltpu.PrefetchScalarGridSpec(
            num_scalar_prefetch=1,
            grid=(nt,),
            in_specs=[pl.BlockSpec((1, 2, tm), lambda i, zs: (i, 0, 0), memory_space=pltpu.SMEM),
                      pl.BlockSpec((tm, d), lambda i, zs: (i, 0)), pl.BlockSpec((1, d), lambda i, zs: (0, 0)),
                      pl.BlockSpec((1, 1, d), grp), pl.BlockSpec((1, 1, d), grp)],
            out_specs=pl.BlockSpec(memory_space=pl.ANY),
            scratch_shapes=[pltpu.VMEM(pshape, pdtype), pltpu.SemaphoreType.DMA((2,))]),
        out_shape=jax.ShapeDtypeStruct((n_tiles * group_rows, pshape[1]), pdtype),
        compiler_params=pltpu.CompilerParams(dimension_semantics=("arbitrary",),
                                             vmem_limit_bytes=_vmem(6 * tm * d * 4)),
        name="moe_scatter",
    )(zero_plan, pos, x, g.reshape(1, d), sc, sh)


def _combine_kernel(pos_ref, y_ref, x_ref, gate_ref, gf_ref, gn_ref, o_ref, buf_ref, sem):
    tm = pos_ref.shape[2]

    def copies(t):
        return (pltpu.make_async_copy(y_ref.at[pos_ref[0, 0, t]], buf_ref.at[0, t], sem.at[0]),
                pltpu.make_async_copy(y_ref.at[pos_ref[0, 1, t]], buf_ref.at[1, t], sem.at[1]))

    def issue(t, carry):
        for cp in copies(t):
            cp.start()
        return carry

    lax.fori_loop(0, tm, issue, 0, unroll=ROW_DMA_UNROLL)

    def drain(t, carry):
        for cp in copies(t):
            cp.wait()
        return carry

    lax.fori_loop(0, tm, drain, 0, unroll=ROW_DMA_UNROLL)
    w = gate_ref[...]
    moe = w[:, 0:1] * _unpack_rows(buf_ref[0]) + w[:, 1:2] * _unpack_rows(buf_ref[1])
    o_ref[...] = _rms(x_ref[...] + gf_ref[0] * moe, gn_ref[...])


def _combine(y, pos, x, gate, gf, gn, *, rows_per_group):
    m, d = x.shape
    nt, _, tm = pos.shape
    gpt = rows_per_group // tm
    return pl.pallas_call(
        _combine_kernel,
        grid=(nt,),
        in_specs=[pl.BlockSpec((1, 2, tm), lambda i: (i, 0, 0), memory_space=pltpu.SMEM),
                  pl.BlockSpec(memory_space=pl.ANY),
                  pl.BlockSpec((tm, d), lambda i: (i, 0)), pl.BlockSpec((tm, 128), lambda i: (i, 0)),
                  pl.BlockSpec((1, 1, d), lambda i: (i // gpt, 0, 0)), pl.BlockSpec((1, d), lambda i: (0, 0))],
        out_specs=pl.BlockSpec((tm, d), lambda i: (i, 0)),
        out_shape=jax.ShapeDtypeStruct((m, d), F32),
        scratch_shapes=[pltpu.VMEM((2, tm, y.shape[1]), y.dtype), pltpu.SemaphoreType.DMA((2,))],
        compiler_params=pltpu.CompilerParams(dimension_semantics=("arbitrary",),
                                             vmem_limit_bytes=_vmem(8 * tm * d * 4)),
        name="moe_combine",
    )(pos, y, x, gate, gf, gn.reshape(1, d))


def _axial_angles(rows, dim):
    row = jnp.repeat(jnp.arange(rows, dtype=F32), GRID_W)
    col = jnp.tile(jnp.arange(GRID_W, dtype=F32), rows)
    half = dim // 2
    inv = 1.0 / (ROPE_BASE ** (jnp.arange(0, half, 2, dtype=F32) / half))
    return jnp.concatenate([row[:, None] * inv, col[:, None] * inv], axis=-1)


def _rope_tables(rows, dim, t_ctx):
    ang = _axial_angles(rows, dim)
    cs = jnp.concatenate([jnp.cos(ang), jnp.cos(ang)], axis=1)
    sn = jnp.concatenate([-jnp.sin(ang), jnp.sin(ang)], axis=1)
    return (cs, sn), (jnp.ones((t_ctx, dim), F32), jnp.zeros((t_ctx, dim), F32))


def _evens_odds(w):
    pairs = w.reshape(w.shape[:-1] + (w.shape[-1] // 2, 2))
    return pairs[..., 0], pairs[..., 1]


def _even_weights(w_in, w_g2, b_g2, w_uq, w_ukv):
    d = w_in.shape[0]
    sizes = (512, 512, 1024, 1024, 16, 16, MLA_Q_RANK, MLA_KV_RANK, MLA_ROPE)
    offs = np.cumsum((0,) + sizes)
    gq, gk, gv, gr, gaf, gab, mq, mkv, mkr = [w_in[:, offs[i]:offs[i + 1]] for i in range(9)]
    kr_e, kr_o = _evens_odds(mkr)
    w_main = jnp.concatenate([gv, gr, gq, gk, mq, kr_e, kr_o, kr_o, kr_e, jnp.zeros((d, 128), F32), mkv],
                             axis=1).astype(BF16)
    w_small = jnp.concatenate([gaf, gab, jnp.zeros((d, EV_SMALL - 2 * GLA_RANK), F32)], axis=1)
    wg = jnp.zeros((2, EV_SMALL, GLA_HEADS * GLA_DK), F32)
    wg = wg.at[0, :GLA_RANK].set(w_g2[0]).at[1, GLA_RANK:2 * GLA_RANK].set(w_g2[1])
    bg = b_g2.reshape(2, 1, -1)
    wq = w_uq.reshape(MLA_Q_RANK, MLA_HEADS, MLA_DQK)
    qr_e, qr_o = _evens_odds(wq[:, :, MLA_NOPE:])
    wq = jnp.concatenate([wq[:, :, :MLA_NOPE], qr_e, qr_o, qr_o, qr_e], axis=2)
    wq = jnp.transpose(wq, (1, 0, 2)).astype(BF16)
    wkv = jnp.transpose(w_ukv.reshape(MLA_KV_RANK, MLA_HEADS, MLA_NOPE + MLA_DV), (1, 0, 2)).astype(BF16)
    return w_main, w_small, wg, bg, wq, wkv


def _perm_kernel(w_ref, p_ref, o_ref):
    o_ref[...] = _dot(w_ref[...].astype(BF16), p_ref[...]).astype(o_ref.dtype)


def _odd_weights(w_in):
    d, n = w_in.shape
    hd = RET_DK
    n_perm = 2 * RET_HEADS
    src = np.concatenate([np.arange(0, hd, 2), np.arange(1, hd, 2)])
    perm = np.zeros((2, hd, hd), np.float32)
    perm[0, src, np.arange(hd)] = 1.0
    perm[1] = np.eye(hd, dtype=np.float32)
    return pl.pallas_call(
        _perm_kernel,
        grid=(n // hd,),
        in_specs=[pl.BlockSpec((d, hd), lambda j: (0, j)),
                  pl.BlockSpec((None, hd, hd), lambda j: (jnp.where(j < n_perm, 0, 1), 0, 0))],
        out_specs=pl.BlockSpec((d, hd), lambda j: (0, j)),
        out_shape=jax.ShapeDtypeStruct((d, n), BF16),
        compiler_params=pltpu.CompilerParams(dimension_semantics=("parallel",)),
        name="weight_reorder",
    )(w_in, jnp.asarray(perm, BF16))


def _ret_tables(log_decay, c):
    lg = -jnp.exp(log_decay.astype(F32))
    pos = jnp.arange(c, dtype=F32)
    diff = pos[:, None] - pos[None, :]
    lgd = lg[:, :, None, None]
    fmask = jnp.where(diff >= 0, jnp.exp(lgd * jnp.maximum(diff, 0.0)), 0.0)
    bmask = jnp.where(diff <= 0, jnp.exp(lgd * jnp.maximum(-diff, 0.0)), 0.0)
    dm = jnp.stack([fmask[0], bmask[1]])
    lgc = lg[:, :, None, None]
    qd = jnp.stack([jnp.exp(lgc[0] * (pos + 1.0)[None, :, None]), jnp.exp(lgc[1] * (c - pos)[None, :, None])])
    kd = jnp.stack([jnp.exp(lgc[0] * (c - 1.0 - pos)[None, :, None]), jnp.exp(lgc[1] * pos[None, :, None])])
    cd = jnp.exp(lgc * c)
    return dm, qd, kd, cd


def _dispatch_plan(meta, counts, tm, n_tiles):
    cnt = counts[:, 0].astype(jnp.int32)
    tiles = (cnt + tm - 1) // tm
    tile_end = jnp.cumsum(tiles)
    start = (tile_end - tiles) * tm
    e = meta[0:2].astype(jnp.int32)
    start_e = sum(jnp.where(e == k, start[k], 0) for k in range(N_EXPERTS))
    pos = start_e + meta[2:4].astype(jnp.int32)
    tile_ids = jnp.arange(n_tiles, dtype=jnp.int32)
    n_valid = tile_end[-1:].astype(jnp.int32)
    last_valid = jnp.minimum(tile_ids, n_valid[0] - 1)
    te = jnp.sum((last_valid[:, None] >= tile_end[None, :]).astype(jnp.int32), axis=1)
    te = jnp.minimum(te, N_EXPERTS - 1).astype(jnp.int32)
    zero_start = jnp.where(tiles > 0, (tile_end - 1) * tm, -1).astype(jnp.int32)
    return pos, te, n_valid, jnp.concatenate([zero_start, n_valid])


def kernel(x, c, ctx, c_ctx, ada_w, ada_b, norm_mix, norm_ffn, norm_final, ev_w_in, gla_w_gate2, gla_b_gate2, gla_norm, mla_q_norm, mla_w_uq, mla_kv_norm, mla_w_ukv, ev_w_out, ffn_w_gate, ffn_w_up, ffn_w_down, od_w_in, ret_log_decay, ret_norm, od_w_out, moe_router, moe_w_gate, moe_w_up, moe_w_down):
    B, T, D = x.shape
    TC = ctx.shape[1]
    depth = ada_w.shape[0]
    assert depth == 2 and T % GRID_W == 0 and B <= 7
    rows = T // GRID_W
    xl = x.reshape(B * T, D)
    xc = ctx.reshape(B * TC, D)

    c_rows = jnp.concatenate([c, c_ctx[None, :], jnp.zeros((8 - B - 1, D), F32)], axis=0)
    m_all = _ada(c_rows, ada_w, ada_b).reshape(depth, 8, 6, 1, D)
    mods = [[(m_all[i, :B, k], m_all[i, B:B + 1, k]) for k in range(6)] for i in range(depth)]

    (sh_a, sc_a, g_a, sh_f, sc_f, g_f) = mods[0]
    w_main, w_small, wg2, bg2, wq, wkv = _even_weights(ev_w_in[0], gla_w_gate2[0], gla_b_gate2[0],
                                                       mla_w_uq[0], mla_w_ukv[0])
    tn_ev = _tile(EV_MAIN, 768)
    main_l, small_l = _nmm(xl, norm_mix[0], 1.0 + sc_a[0], sh_a[0], w_main, rows_per_group=T, tn=tn_ev,
                           w_small=w_small)
    main_c, small_c = _nmm(xc, norm_mix[0], 1.0 + sc_a[1], sh_a[1], w_main, rows_per_group=B * TC, tn=tn_ev,
                           w_small=w_small)

    s0 = jnp.zeros((B, GLA_HEADS, GLA_DV, GLA_DK), F32)
    ocf, ocb, s_f, s_b = _gla(main_c, small_c, wg2, bg2, s0, s0, batch=B, t=TC)
    olf, olb, _, _ = _gla(main_l, small_l, wg2, bg2, s_f, s_b, batch=B, t=T)

    (cs_m, sn_m), (cs_1, sn_0) = _rope_tables(rows, MLA_ROPE, TC)
    q_l, k_l, v_l = _mla_proj(main_l, mla_q_norm[0], mla_kv_norm[0], wq, wkv, cs_m, sn_m, t=T)
    q_c, k_c, v_c = _mla_proj(main_c, mla_q_norm[0], mla_kv_norm[0], wq, wkv, cs_1, sn_0, t=TC)
    mla_l = _attn(q_l, k_l, v_l, k_c, v_c)
    mla_c = _attn(q_c, k_c, v_c)

    w_out = ev_w_out[0].astype(BF16)
    gr_blk = EV_GR // (GLA_HEADS * GLA_DV)
    x1l = _mixout(olf, olb, main_l, gr_blk, gla_norm[0], mla_l, w_out, xl, g_a[0], rows_per_group=T,
                  n_heads=GLA_HEADS, dv=GLA_DV, center=False)
    x1c = _mixout(ocf, ocb, main_c, gr_blk, gla_norm[0], mla_c, w_out, xc, g_a[1], rows_per_group=B * TC,
                  n_heads=GLA_HEADS, dv=GLA_DV, center=False)

    fwg, fwu, fwd_ = ffn_w_gate.astype(BF16), ffn_w_up.astype(BF16), ffn_w_down.astype(BF16)
    tm_l = _tile(T, FFN_ROWS)
    x2l = _ffn(x1l, fwg, fwu, fwd_, jnp.zeros((B * T // tm_l,), jnp.int32), jnp.full((1,), B * T // tm_l, jnp.int32),
               tm=tm_l, norm=(norm_ffn[0], 1.0 + sc_f[0], sh_f[0], g_f[0], T))
    tm_c = _tile(B * TC, FFN_ROWS)
    x2c = _ffn(x1c, fwg, fwu, fwd_, jnp.zeros((B * TC // tm_c,), jnp.int32),
               jnp.full((1,), B * TC // tm_c, jnp.int32), tm=tm_c,
               norm=(norm_ffn[0], 1.0 + sc_f[1], sh_f[1], g_f[1], B * TC))

    (sh_a, sc_a, g_a, sh_f, sc_f, g_f) = mods[1]
    w_odd = _odd_weights(od_w_in[0])
    hk = RET_HEADS * RET_DK
    tn_od = _tile(4 * hk, 1024)
    rope_l, rope_c = _rope_tables(rows, RET_DK, B * TC)
    m2l = _nmm(x2l, norm_mix[1], 1.0 + sc_a[0], sh_a[0], w_odd, rows_per_group=T, tn=tn_od, rope=rope_l,
               rope_cols=2 * hk)
    m2c = _nmm(x2c, norm_mix[1], 1.0 + sc_a[1], sh_a[1], w_odd, rows_per_group=B * TC, tn=tn_od, rope=rope_c,
               rope_cols=2 * hk)
    s0 = jnp.zeros((B, RET_HEADS, RET_DK, RET_DV), F32)
    _, _, s_f, s_b = _ret(m2c, _ret_tables(ret_log_decay[0], min(RET_CHUNK, TC)), s0, s0, batch=B, t=TC)
    orf, orb, _, _ = _ret(m2l, _ret_tables(ret_log_decay[0], min(RET_CHUNK, T)), s_f, s_b, batch=B, t=T)
    x3 = _mixout(orf, orb, m2l, 3, ret_norm[0], None, od_w_out[0].astype(BF16), x2l, g_a[0], rows_per_group=T,
                 n_heads=RET_HEADS, dv=RET_DV, center=True)

    wr = jnp.concatenate([moe_router[0], jnp.zeros((D, 128 - N_EXPERTS), F32)], axis=1)
    meta, gate, counts = _router(x3, norm_ffn[1], 1.0 + sc_f[0], sh_f[0], wr, rows_per_group=T)
    tm_e = _tile(T, MOE_ROWS)
    n_tiles = (2 * B * T) // tm_e + N_EXPERTS
    pos, te, n_valid, zero_plan = _dispatch_plan(meta, counts, tm_e, n_tiles)
    tm_r = _tile(T, ROUTER_ROWS)
    pos_t = jnp.transpose(pos.reshape(2, -1, tm_r), (1, 0, 2))
    hs = _scatter_rows(x3, norm_ffn[1], 1.0 + sc_f[0], sh_f[0], pos_t, zero_plan, n_tiles=n_tiles,
                       group_rows=tm_e, rows_per_group=T)
    ys = _ffn(hs, moe_w_gate[0].astype(BF16), moe_w_up[0].astype(BF16), moe_w_down[0].astype(BF16), te, n_valid,
              tm=tm_e)
    out = _combine(ys, pos_t, x3, gate, g_f[0], norm_final, rows_per_group=T)
    return out.reshape(B, T, D)
```

```python
import functools

import jax
import jax.numpy as jnp
import numpy as np
from jax import lax
from jax.experimental import pallas as pl
from jax.experimental.pallas import tpu as pltpu

F32 = jnp.float32
BF16 = jnp.bfloat16
EPS = 1e-6
ROPE_BASE = 10000.0
GRID_W = 64

GLA_HEADS, GLA_DK, GLA_DV, GLA_RANK, GLA_GATE_NORM, GLA_CHUNK = 4, 128, 256, 16, 16.0, 64
MLA_HEADS, MLA_Q_RANK, MLA_KV_RANK, MLA_NOPE, MLA_ROPE, MLA_DV = 8, 768, 512, 128, 64, 128
MLA_DQK = MLA_NOPE + MLA_ROPE
MLA_SCALE = MLA_DQK ** -0.5
MLA_Q_SCALE = MLA_SCALE * 1.4426950408889634
RET_HEADS, RET_DK, RET_DV = 8, 256, 256
RET_CHUNK = 256
N_EXPERTS = 8

EV_GV, EV_GR, EV_GQ, EV_GK, EV_MQ, EV_KR, EV_MKV, EV_MAIN = 0, 1024, 2048, 2560, 3072, 3840, 4096, 4608
EV_SMALL = 128

V7X_VMEM_BYTES = 64 << 20
VMEM_HEADROOM_BYTES = 6 << 20
FFN_ROWS = 512
MOE_ROWS = 1024
ROUTER_ROWS = 512
ROW_DMA_UNROLL = 8


def _vmem(nbytes):
    return int(min(V7X_VMEM_BYTES - VMEM_HEADROOM_BYTES, max(32 << 20, nbytes + (8 << 20))))


def _tile(n, pref, align=128):
    if n <= pref:
        return n
    t = (pref // align) * align
    while t >= align:
        if n % t == 0:
            return t
        t -= align
    return n


def _split_bf16(a):
    hi = a.astype(BF16)
    lo = (a - hi.astype(F32)).astype(BF16)
    return hi, lo


def _dot(a, b):
    return jnp.dot(a, b, preferred_element_type=F32)


def _dot3(a, b):
    ah, al = _split_bf16(a)
    bh, bl = _split_bf16(b)
    return _dot(ah, bh) + _dot(al, bh) + _dot(ah, bl)


def _dot_nt(a, b):
    return lax.dot_general(a, b, (((1,), (1,)), ((), ())), preferred_element_type=F32)


def _dot_tn(a, b):
    return lax.dot_general(a, b, (((0,), (0,)), ((), ())), preferred_element_type=F32)


def _silu(x):
    return x * (1.0 / (1.0 + jnp.exp(-x)))


def _log_sigmoid(z):
    return -(jnp.maximum(-z, 0.0) + jnp.log(1.0 + jnp.exp(-jnp.abs(z))))


def _rms(x, g):
    return x * lax.rsqrt(jnp.mean(x * x, axis=-1, keepdims=True) + EPS) * g


def _pack_rows(y):
    half = y.shape[1] // 2
    lo = lax.bitcast_convert_type(y[:, :half].astype(BF16).astype(F32), jnp.uint32)
    hi = lax.bitcast_convert_type(y[:, half:].astype(BF16).astype(F32), jnp.uint32)
    return lax.shift_right_logical(lo, jnp.uint32(16)) | hi


def _packed_shape(m, d):
    return (m, d // 2), jnp.uint32


def _unpack_rows(w):
    lo = lax.bitcast_convert_type(lax.shift_left(w, jnp.uint32(16)), F32)
    hi = lax.bitcast_convert_type(w & jnp.uint32(0xFFFF0000), F32)
    return jnp.concatenate([lo, hi], axis=1)


def _ada_kernel(c_ref, w_ref, b_ref, o_ref):
    o_ref[0] = _dot3(_silu(c_ref[...]), w_ref[0]) + b_ref[0]


def _ada(c_rows, w, b):
    nl, d, n = w.shape
    tn = _tile(n, 1536)
    return pl.pallas_call(
        _ada_kernel,
        grid=(nl, n // tn),
        in_specs=[pl.BlockSpec((8, d), lambda l, j: (0, 0)),
                  pl.BlockSpec((1, d, tn), lambda l, j: (l, 0, j)),
                  pl.BlockSpec((1, 1, tn), lambda l, j: (l, 0, j))],
        out_specs=pl.BlockSpec((1, 8, tn), lambda l, j: (l, 0, j)),
        out_shape=jax.ShapeDtypeStruct((nl, 8, n), F32),
        compiler_params=pltpu.CompilerParams(dimension_semantics=("parallel", "parallel"),
                                             vmem_limit_bytes=_vmem(2 * d * tn * 4 * 3)),
        name="ada_mod",
    )(c_rows, w, b.reshape(nl, 1, n))


def _nmm_kernel(*refs, has_small, rope_tiles, heads_per_tile):
    x_ref, g_ref, sc_ref, sh_ref, w_ref = refs[:5]
    k = 5
    if has_small:
        w2_ref = refs[k]; k += 1
    if rope_tiles:
        cs_ref, sn_ref = refs[k], refs[k + 1]; k += 2
    o_ref = refs[k]; k += 1
    if has_small:
        o2_ref = refs[k]; k += 1
    hn_ref = refs[k]
    j = pl.program_id(1)

    @pl.when(j == 0)
    def _():
        y = _rms(x_ref[...], g_ref[...]) * sc_ref[0] + sh_ref[0]
        hn_ref[...] = y.astype(BF16)
        if has_small:
            o2_ref[...] = _dot3(y, w2_ref[...])

    acc = _dot(hn_ref[...], w_ref[...])
    if rope_tiles:
        @pl.when(j < rope_tiles)
        def _():
            cs = cs_ref[...]
            sn = sn_ref[...]
            half = cs.shape[1] // 2
            outs = []
            for h in range(heads_per_tile):
                r = acc[:, h * 2 * half:(h + 1) * 2 * half]
                rs = jnp.concatenate([r[:, half:], r[:, :half]], axis=1)
                outs.append(r * cs + rs * sn)
            o_ref[...] = jnp.concatenate(outs, axis=1).astype(o_ref.dtype)

        @pl.when(j >= rope_tiles)
        def _():
            o_ref[...] = acc.astype(o_ref.dtype)
    else:
        o_ref[...] = acc.astype(o_ref.dtype)


def _nmm(x, g, sc, sh, w, *, rows_per_group, tn, w_small=None, rope=None, rope_cols=0):
    m, d = x.shape
    n = w.shape[1]
    tm = _tile(rows_per_group, 1024)
    gpt = rows_per_group // tm
    grp = lambda i, j: (i // gpt, 0, 0)
    in_specs = [pl.BlockSpec((tm, d), lambda i, j: (i, 0)),
                pl.BlockSpec((1, d), lambda i, j: (0, 0)),
                pl.BlockSpec((1, 1, d), grp),
                pl.BlockSpec((1, 1, d), grp),
                pl.BlockSpec((d, tn), lambda i, j: (0, j))]
    args = [x, g.reshape(1, d), sc, sh, w]
    out_specs = [pl.BlockSpec((tm, tn), lambda i, j: (i, j))]
    out_shape = [jax.ShapeDtypeStruct((m, n), BF16)]
    if w_small is not None:
        in_specs.append(pl.BlockSpec((d, EV_SMALL), lambda i, j: (0, 0)))
        args.append(w_small)
        out_specs.append(pl.BlockSpec((tm, EV_SMALL), lambda i, j: (i, 0)))
        out_shape.append(jax.ShapeDtypeStruct((m, EV_SMALL), F32))
    rope_tiles = heads_per_tile = 0
    if rope is not None:
        cs, sn = rope
        t_rows, hd = cs.shape
        rope_tiles, heads_per_tile = rope_cols // tn, tn // hd
        tpb = t_rows // tm
        in_specs += [pl.BlockSpec((tm, hd), lambda i, j: (i % tpb, 0)),
                     pl.BlockSpec((tm, hd), lambda i, j: (i % tpb, 0))]
        args += [cs, sn]
    kern = functools.partial(_nmm_kernel, has_small=w_small is not None, rope_tiles=rope_tiles,
                             heads_per_tile=heads_per_tile)
    res = pl.pallas_call(
        kern,
        grid=(m // tm, n // tn),
        in_specs=in_specs,
        out_specs=out_specs,
        out_shape=out_shape,
        scratch_shapes=[pltpu.VMEM((tm, d), BF16)],
        compiler_params=pltpu.CompilerParams(
            dimension_semantics=("parallel", "arbitrary"),
            vmem_limit_bytes=_vmem(2 * tm * d * 4 + tm * d * 2 + 2 * d * tn * 2 + 5 * tm * tn * 4)),
        name="norm_proj",
    )(*args)
    return res if w_small is not None else res[0]


def _gla_kernel(mf_q, mf_k, mf_v, sm_f, mb_q, mb_k, mb_v, sm_b, wg_ref, bg_ref, s0f_ref, s0b_ref,
                of_ref, ob_ref, sf_ref, sb_ref, st_ref, *, n_sub):
    s = pl.program_id(1)
    c = GLA_CHUNK

    @pl.when(s == 0)
    def _():
        st_ref[0] = s0f_ref[0]
        st_ref[1] = s0b_ref[0]

    rows = n_sub * c
    row = lax.broadcasted_iota(jnp.int32, (c, c), 0)
    col = lax.broadcasted_iota(jnp.int32, (c, c), 1)
    brow = lax.broadcasted_iota(jnp.int32, (rows, rows), 0)
    bcol = lax.broadcasted_iota(jnp.int32, (rows, rows), 1)
    same_chunk = (brow // c) == (bcol // c)
    for d, (q_ref, k_ref, v_ref, sm_ref, o_ref) in enumerate(
            ((mf_q, mf_k, mf_v, sm_f, of_ref), (mb_q, mb_k, mb_v, sm_b, ob_ref))):
        keep = (col <= row) if d == 0 else (col >= row)
        bkeep = jnp.logical_and(same_chunk, (bcol <= brow) if d == 0 else (bcol >= brow))
        tri = jnp.where(bkeep, 1.0, 0.0).astype(BF16)
        z = _dot3(sm_ref[...], wg_ref[d]) + bg_ref[d]
        lh, ll = _split_bf16(_log_sigmoid(z) * (1.0 / GLA_GATE_NORM))
        b_all = _dot(tri, lh) + _dot(tri, ll)
        order = range(n_sub) if d == 0 else range(n_sub - 1, -1, -1)
        for ci in order:
            r0 = ci * c
            for h in range(GLA_HEADS):
                b = b_all[r0:r0 + c, h * GLA_DK:(h + 1) * GLA_DK]
                b_last = b[c - 1:c, :] if d == 0 else b[0:1, :]
                q = q_ref[r0:r0 + c, h * GLA_DK:(h + 1) * GLA_DK].astype(F32) * (GLA_DK ** -0.5)
                k = k_ref[r0:r0 + c, h * GLA_DK:(h + 1) * GLA_DK].astype(F32)
                v = v_ref[r0:r0 + c, h * GLA_DV:(h + 1) * GLA_DV]
                qe = (q * jnp.exp(b)).astype(BF16)
                kd = (k * jnp.exp(-b)).astype(BF16)
                kl = (k * jnp.exp(b_last - b)).astype(BF16)
                att = jnp.where(keep, _dot_nt(qe, kd), 0.0).astype(BF16)
                st = st_ref[d, h]
                o = _dot(att, v) + _dot_nt(qe, st.astype(BF16))
                o_ref[r0:r0 + c, h * GLA_DV:(h + 1) * GLA_DV] = o.astype(o_ref.dtype)
                st_ref[d, h] = jnp.exp(b_last) * st + _dot_tn(v, kl)

    @pl.when(s == pl.num_programs(1) - 1)
    def _():
        sf_ref[0] = st_ref[0]
        sb_ref[0] = st_ref[1]


def _gla(main, small, wg, bg, s0f, s0b, *, batch, t):
    rows = _tile(t, 256, GLA_CHUNK)
    ns = t // rows
    hv, hk = GLA_HEADS * GLA_DV, GLA_HEADS * GLA_DK
    fwd = lambda blk: (lambda b, s: (b * ns + s, blk))
    bwd = lambda blk: (lambda b, s: (b * ns + ns - 1 - s, blk))
    st_spec = pl.BlockSpec((1, GLA_HEADS, GLA_DV, GLA_DK), lambda b, s: (b, 0, 0, 0))
    st_shape = jax.ShapeDtypeStruct((batch, GLA_HEADS, GLA_DV, GLA_DK), F32)

    def io(m):
        return [pl.BlockSpec((rows, hk), m(EV_GQ // hk)), pl.BlockSpec((rows, hk), m(EV_GK // hk)),
                pl.BlockSpec((rows, hv), m(EV_GV // hv)), pl.BlockSpec((rows, EV_SMALL), m(0))]

    return pl.pallas_call(
        functools.partial(_gla_kernel, n_sub=rows // GLA_CHUNK),
        grid=(batch, ns),
        in_specs=io(fwd) + io(bwd) + [
            pl.BlockSpec((2, EV_SMALL, hk), lambda b, s: (0, 0, 0)),
            pl.BlockSpec((2, 1, hk), lambda b, s: (0, 0, 0)),
            st_spec, st_spec],
        out_specs=[pl.BlockSpec((rows, hv), fwd(0)), pl.BlockSpec((rows, hv), bwd(0)), st_spec, st_spec],
        out_shape=[jax.ShapeDtypeStruct((batch * t, hv), BF16), jax.ShapeDtypeStruct((batch * t, hv), BF16),
                   st_shape, st_shape],
        scratch_shapes=[pltpu.VMEM((2, GLA_HEADS, GLA_DV, GLA_DK), F32)],
        compiler_params=pltpu.CompilerParams(dimension_semantics=("parallel", "arbitrary"),
                                             vmem_limit_bytes=_vmem(16 << 20)),
        name="gla_scan",
    )(main, main, main, small, main, main, main, small, wg, bg, s0f, s0b)


def _mla_proj_kernel(xq_ref, xkv_ref, kr_ref, gq_ref, gkv_ref, wq_ref, wkv_ref, cs_ref, sn_ref,
                     q_ref, k_ref, v_ref):
    cs, sn = cs_ref[...], sn_ref[...]
    hq = _rms(xq_ref[...].astype(F32), gq_ref[...]).astype(BF16)
    hkv = _rms(xkv_ref[...].astype(F32), gkv_ref[...]).astype(BF16)
    kr = kr_ref[...].astype(F32)
    krr = kr[:, :MLA_ROPE] * cs + kr[:, MLA_ROPE:2 * MLA_ROPE] * sn
    for h in range(MLA_HEADS):
        r = _dot(hq, wq_ref[h])
        qr = r[:, MLA_NOPE:MLA_NOPE + MLA_ROPE] * cs + r[:, MLA_NOPE + MLA_ROPE:] * sn
        q_ref[0, h] = (jnp.concatenate([r[:, :MLA_NOPE], qr], axis=1) * MLA_Q_SCALE).astype(q_ref.dtype)
        r = _dot(hkv, wkv_ref[h])
        k_ref[0, h] = jnp.concatenate([r[:, :MLA_NOPE], krr], axis=1).astype(k_ref.dtype)
        v_ref[0, h] = r[:, MLA_NOPE:].astype(v_ref.dtype)


def _mla_proj(main, gq, gkv, wq, wkv, cs, sn, *, t):
    m = main.shape[0]
    tm = _tile(t, 512)
    tpb = t // tm
    out = lambda i: (i // tpb, 0, i % tpb, 0)
    rope = lambda i: (i % tpb, 0)
    full = lambda a: pl.BlockSpec(a.shape, lambda i: (0,) * a.ndim)
    gq, gkv = gq.reshape(1, -1), gkv.reshape(1, -1)
    return pl.pallas_call(
        _mla_proj_kernel,
        grid=(m // tm,),
        in_specs=[pl.BlockSpec((tm, MLA_Q_RANK), lambda i: (i, EV_MQ // MLA_Q_RANK)),
                  pl.BlockSpec((tm, MLA_KV_RANK), lambda i: (i, EV_MKV // MLA_KV_RANK)),
                  pl.BlockSpec((tm, 256), lambda i: (i, EV_KR // 256)),
                  full(gq), full(gkv), full(wq), full(wkv),
                  pl.BlockSpec((tm, MLA_ROPE), rope), pl.BlockSpec((tm, MLA_ROPE), rope)],
        out_specs=[pl.BlockSpec((1, MLA_HEADS, tm, MLA_DQK), out),
                   pl.BlockSpec((1, MLA_HEADS, tm, MLA_DQK), out),
                   pl.BlockSpec((1, MLA_HEADS, tm, MLA_DV), out)],
        out_shape=[jax.ShapeDtypeStruct((m // t, MLA_HEADS, t, MLA_DQK), BF16),
                   jax.ShapeDtypeStruct((m // t, MLA_HEADS, t, MLA_DQK), BF16),
                   jax.ShapeDtypeStruct((m // t, MLA_HEADS, t, MLA_DV), BF16)],
        compiler_params=pltpu.CompilerParams(dimension_semantics=("parallel",),
                                             vmem_limit_bytes=_vmem(40 * MLA_HEADS * tm * 256)),
        name="mla_proj",
    )(main, main, main, gq, gkv, wq, wkv, cs, sn)


ATTN_ROWS = 256


def _attn_kernel(*refs, n_main, has_extra):
    if has_extra:
        q_ref, k_ref, v_ref, kx_ref, vx_ref, o_ref, m_ref, acc_ref = refs
    else:
        q_ref, k_ref, v_ref, o_ref, m_ref, acc_ref = refs
    kv = pl.program_id(3)

    @pl.when(kv == 0)
    def _():
        m_ref[...] = jnp.full(m_ref.shape, -jnp.inf, F32)
        acc_ref[...] = jnp.zeros(acc_ref.shape, F32)

    tq = q_ref.shape[2]
    rc = min(ATTN_ROWS, tq)

    key_values = [(k_ref[0, 0], v_ref[0, 0])]
    if has_extra:
        key_values.append((kx_ref[0, 0], vx_ref[0, 0]))
    extra_on = kv == n_main - 1
    ks = [k for k, _ in key_values]
    v_ones = [jnp.concatenate([v, jnp.ones(v.shape, v.dtype)], axis=1) for _, v in key_values]
    for r in range(tq // rc):
        rows = slice(r * rc, (r + 1) * rc)
        q = q_ref[0, 0, rows, :]
        ss = [_dot_nt(q, k) for k in ks]
        if has_extra and n_main > 1:
            ss[1] = jnp.where(extra_on, ss[1], -jnp.inf)
        m_old = m_ref[rows, :]
        m_new = m_old
        for s in ss:
            m_new = jnp.maximum(m_new, jnp.max(s, axis=-1, keepdims=True))
        pv = sum(_dot(jnp.exp2(s - m_new).astype(BF16), vo) for s, vo in zip(ss, v_ones))
        acc_ref[rows, :] = jnp.exp2(m_old - m_new) * acc_ref[rows, :] + pv
        m_ref[rows, :] = m_new

    @pl.when(kv == pl.num_programs(3) - 1)
    def _():
        acc = acc_ref[...]
        o_ref[...] = (acc[:, :MLA_DV] / acc[:, MLA_DV:]).astype(o_ref.dtype)


def _attn(q, k, v, kx=None, vx=None, *, tq_pref=2048, tk_pref=8192):
    b, h, tq_all, dqk = q.shape
    tk_all = k.shape[2]
    tq, tk = _tile(tq_all, tq_pref), _tile(tk_all, tk_pref)
    nq, n_main = tq_all // tq, tk_all // tk
    main = lambda bi, hi, i, j: (bi, hi, j, 0)
    in_specs = [pl.BlockSpec((1, 1, tq, dqk), lambda bi, hi, i, j: (bi, hi, i, 0)),
                pl.BlockSpec((1, 1, tk, dqk), main), pl.BlockSpec((1, 1, tk, MLA_DV), main)]
    args = [q, k, v]
    if kx is not None:
        tx = kx.shape[2]
        in_specs += [pl.BlockSpec((1, 1, tx, dqk), lambda bi, hi, i, j: (bi, hi, 0, 0)),
                     pl.BlockSpec((1, 1, tx, MLA_DV), lambda bi, hi, i, j: (bi, hi, 0, 0))]
        args += [kx, vx]
    return pl.pallas_call(
        functools.partial(_attn_kernel, n_main=n_main, has_extra=kx is not None),
        grid=(b, h, nq, n_main),
        in_specs=in_specs,
        out_specs=pl.BlockSpec((tq, MLA_DV), lambda bi, hi, i, j: (bi * nq + i, hi)),
        out_shape=jax.ShapeDtypeStruct((b * tq_all, h * MLA_DV), BF16),
        scratch_shapes=[pltpu.VMEM((tq, 1), F32), pltpu.VMEM((tq, 2 * MLA_DV), F32)],
        compiler_params=pltpu.CompilerParams(
            dimension_semantics=("parallel", "parallel", "parallel", "arbitrary"),
            vmem_limit_bytes=_vmem(5 * min(ATTN_ROWS, tq) * tk * 4 + 8 * tk * 256 * 2)),
        name="mla_attention",
    )(*args)


def _mixout_kernel(*refs, n_heads, dv, center, has_mla):
    of_ref, ob_ref, gate_ref, g_ref = refs[:4]
    k = 4
    if has_mla:
        mla_ref = refs[k]; k += 1
    w_ref, x_ref, ga_ref, o_ref = refs[k:k + 4]
    o = of_ref[...].astype(F32) + ob_ref[...].astype(F32)
    g = g_ref[...]
    parts = []
    for h in range(n_heads):
        oh = o[:, h * dv:(h + 1) * dv]
        if center:
            oh = oh - jnp.mean(oh, axis=-1, keepdims=True)
        parts.append(oh * lax.rsqrt(jnp.mean(oh * oh, axis=-1, keepdims=True) + EPS) * g)
    lat = jnp.concatenate(parts, axis=1) * _silu(gate_ref[...].astype(F32))
    lhs = lat.astype(BF16)
    if has_mla:
        lhs = jnp.concatenate([lhs, mla_ref[...]], axis=1)
    o_ref[...] = x_ref[...] + ga_ref[0] * _dot(lhs, w_ref[...])


def _mixout(o_f, o_b, main, gate_blk, g, mla, w, x, ga, *, rows_per_group, n_heads, dv, center):
    m, d = x.shape
    hw = n_heads * dv
    tm = _tile(rows_per_group, 512)
    gpt = rows_per_group // tm
    in_specs = [pl.BlockSpec((tm, hw), lambda i: (i, 0)), pl.BlockSpec((tm, hw), lambda i: (i, 0)),
                pl.BlockSpec((tm, hw), lambda i: (i, gate_blk)), pl.BlockSpec((1, dv), lambda i: (0, 0))]
    args = [o_f, o_b, main, g.reshape(1, dv)]
    if mla is not None:
        in_specs.append(pl.BlockSpec((tm, mla.shape[1]), lambda i: (i, 0)))
        args.append(mla)
    in_specs += [pl.BlockSpec(w.shape, lambda i: (0, 0)), pl.BlockSpec((tm, d), lambda i: (i, 0)),
                 pl.BlockSpec((1, 1, d), lambda i: (i // gpt, 0, 0))]
    args += [w, x, ga]
    return pl.pallas_call(
        functools.partial(_mixout_kernel, n_heads=n_heads, dv=dv, center=center, has_mla=mla is not None),
        grid=(m // tm,),
        in_specs=in_specs,
        out_specs=pl.BlockSpec((tm, d), lambda i: (i, 0)),
        out_shape=jax.ShapeDtypeStruct((m, d), F32),
        compiler_params=pltpu.CompilerParams(
            dimension_semantics=("parallel",),
            vmem_limit_bytes=_vmem(2 * w.size * 2 + 4 * tm * d * 4 + 8 * tm * hw * 2 + 6 * tm * hw * 4)),
        name="mixer_out",
    )(*args)


def _ffn_kernel(te_ref, nv_ref, *refs, prenorm):
    if prenorm:
        x_ref, g_ref, sc_ref, sh_ref, wg_ref, wu_ref, wd_ref, gf_ref, o_ref, hn_ref, acc_ref = refs
    else:
        x_ref, wg_ref, wu_ref, wd_ref, o_ref, hn_ref, acc_ref = refs
    i, j = pl.program_id(0), pl.program_id(1)
    valid = i < nv_ref[0]

    @pl.when(jnp.logical_and(valid, j == 0))
    def _():
        if prenorm:
            y = _rms(x_ref[...], g_ref[...]) * sc_ref[0] + sh_ref[0]
        else:
            y = _unpack_rows(x_ref[...])
        hn_ref[...] = y.astype(BF16)
        acc_ref[...] = jnp.zeros(acc_ref.shape, F32)

    @pl.when(valid)
    def _():
        hn = hn_ref[...]
        a = (_silu(_dot(hn, wg_ref[0])) * _dot(hn, wu_ref[0])).astype(BF16)
        acc_ref[...] += _dot(a, wd_ref[0])

    last = j == pl.num_programs(1) - 1

    @pl.when(jnp.logical_and(valid, last))
    def _():
        if prenorm:
            o_ref[...] = x_ref[...] + gf_ref[0] * acc_ref[...]
        else:
            o_ref[...] = _pack_rows(acc_ref[...])

    @pl.when(jnp.logical_and(jnp.logical_not(valid), last))
    def _():
        o_ref[...] = jnp.zeros(o_ref.shape, o_ref.dtype)


def _ffn(x, wg, wu, wd, tile_expert, n_valid, *, tm, norm=None):
    p = x.shape[0]
    d = wg.shape[1]
    f = wg.shape[2]
    tf = _tile(f, 512)
    nt = p // tm
    wmap_in = lambda i, j, te, nv: (te[i], 0, j)
    wmap_out = lambda i, j, te, nv: (te[i], j, 0)
    tail = (0,) * (x.ndim - 1)
    row_block = (tm,) + x.shape[1:]
    row = lambda i, j, te, nv: (i,) + tail
    in_specs = [pl.BlockSpec(row_block, lambda i, j, te, nv: (jnp.minimum(i, nv[0] - 1),) + tail)]
    args = [x]
    if norm is not None:
        g, sc, sh, gf, rpg = norm
        gpt = rpg // tm
        grp = lambda i, j, te, nv: (i // gpt, 0, 0)
        in_specs += [pl.BlockSpec((1, d), lambda i, j, te, nv: (0, 0)), pl.BlockSpec((1, 1, d), grp),
                     pl.BlockSpec((1, 1, d), grp)]
        args += [g.reshape(1, d), sc, sh]
    in_specs += [pl.BlockSpec((1, d, tf), wmap_in), pl.BlockSpec((1, d, tf), wmap_in),
                 pl.BlockSpec((1, tf, d), wmap_out)]
    args += [wg, wu, wd]
    if norm is not None:
        in_specs.append(pl.BlockSpec((1, 1, d), grp))
        args.append(gf)
    return pl.pallas_call(
        functools.partial(_ffn_kernel, prenorm=norm is not None),
        grid_spec=pltpu.PrefetchScalarGridSpec(
            num_scalar_prefetch=2,
            grid=(nt, f // tf),
            in_specs=in_specs,
            out_specs=pl.BlockSpec(row_block, row),
            scratch_shapes=[pltpu.VMEM((tm, d), BF16), pltpu.VMEM((tm, d), F32)]),
        out_shape=jax.ShapeDtypeStruct(x.shape, x.dtype),
        compiler_params=pltpu.CompilerParams(
            dimension_semantics=("parallel", "arbitrary"),
            vmem_limit_bytes=_vmem(16 * x.size // nt + tm * d * 6 + 6 * d * tf * 2 + 5 * tm * tf * 4)),
        name="swiglu_ffn",
    )(tile_expert, n_valid, *args)


def _ret_kernel(qf_ref, kf_ref, vf_ref, qb_ref, kb_ref, vb_ref, dm_ref, qd_ref, kd_ref, cd_ref,
                s0f_ref, s0b_ref, of_ref, ob_ref, sf_ref, sb_ref, st_ref, *, n_sub):
    s = pl.program_id(1)
    c = dm_ref.shape[2]

    @pl.when(s == 0)
    def _():
        st_ref[0] = s0f_ref[0]
        st_ref[1] = s0b_ref[0]

    for d, (q_ref, k_ref, v_ref, o_ref) in enumerate(((qf_ref, kf_ref, vf_ref, of_ref),
                                                       (qb_ref, kb_ref, vb_ref, ob_ref))):
        order = range(n_sub) if d == 0 else range(n_sub - 1, -1, -1)
        for ci in order:
            r0 = ci * c
            for h in range(RET_HEADS):
                q = q_ref[r0:r0 + c, h * RET_DK:(h + 1) * RET_DK]
                k = (k_ref[r0:r0 + c, h * RET_DK:(h + 1) * RET_DK].astype(F32) * (RET_DK ** -0.5)).astype(BF16)
                v = v_ref[r0:r0 + c, h * RET_DV:(h + 1) * RET_DV]
                att = (_dot_nt(q, k) * dm_ref[d, h]).astype(BF16)
                st = st_ref[d, h]
                o = _dot(att, v) + qd_ref[d, h] * _dot(q, st.astype(BF16))
                o_ref[r0:r0 + c, h * RET_DV:(h + 1) * RET_DV] = o.astype(o_ref.dtype)
                vk = (v.astype(F32) * kd_ref[d, h]).astype(BF16)
                st_ref[d, h] = cd_ref[d, h] * st + _dot_tn(k, vk)

    @pl.when(s == pl.num_programs(1) - 1)
    def _():
        sf_ref[0] = st_ref[0]
        sb_ref[0] = st_ref[1]


def _ret(main, tables, s0f, s0b, *, batch, t):
    chunk = tables[0].shape[2]
    rows = _tile(t, 256, chunk)
    ns = t // rows
    hk, hv = RET_HEADS * RET_DK, RET_HEADS * RET_DV
    fwd = lambda blk: (lambda b, s: (b * ns + s, blk))
    bwd = lambda blk: (lambda b, s: (b * ns + ns - 1 - s, blk))
    st_spec = pl.BlockSpec((1, RET_HEADS, RET_DK, RET_DV), lambda b, s: (b, 0, 0, 0))
    st_shape = jax.ShapeDtypeStruct((batch, RET_HEADS, RET_DK, RET_DV), F32)
    io = lambda m: [pl.BlockSpec((rows, hk), m(0)), pl.BlockSpec((rows, hk), m(1)), pl.BlockSpec((rows, hv), m(2))]
    full = lambda a: pl.BlockSpec(a.shape, lambda b, s: (0,) * a.ndim)
    return pl.pallas_call(
        functools.partial(_ret_kernel, n_sub=rows // chunk),
        grid=(batch, ns),
        in_specs=io(fwd) + io(bwd) + [full(a) for a in tables] + [st_spec, st_spec],
        out_specs=[pl.BlockSpec((rows, hv), fwd(0)), pl.BlockSpec((rows, hv), bwd(0)), st_spec, st_spec],
        out_shape=[jax.ShapeDtypeStruct((batch * t, hv), BF16), jax.ShapeDtypeStruct((batch * t, hv), BF16),
                   st_shape, st_shape],
        scratch_shapes=[pltpu.VMEM((2, RET_HEADS, RET_DK, RET_DV), F32)],
        compiler_params=pltpu.CompilerParams(dimension_semantics=("parallel", "arbitrary"),
                                             vmem_limit_bytes=_vmem(46 << 20)),
        name="retention_scan",
    )(main, main, main, main, main, main, *tables, s0f, s0b)


def _router_kernel(x_ref, g_ref, sc_ref, sh_ref, wr_ref, meta_ref, gate_ref, cnt_ref, run_ref):
    i = pl.program_id(0)

    @pl.when(i == 0)
    def _():
        run_ref[...] = jnp.zeros(run_ref.shape, F32)

    y = _rms(x_ref[...], g_ref[...]) * sc_ref[0] + sh_ref[0]
    tm = y.shape[0]
    lt = jnp.transpose(_dot3(y, wr_ref[...]))[:N_EXPERTS, :]
    eid = lax.broadcasted_iota(jnp.int32, lt.shape, 0).astype(F32)
    v1 = jnp.max(lt, axis=0, keepdims=True)
    e1 = jnp.min(jnp.where(lt == v1, eid, float(N_EXPERTS)), axis=0, keepdims=True)
    lt2 = jnp.where(eid == e1, -jnp.inf, lt)
    v2 = jnp.max(lt2, axis=0, keepdims=True)
    e2 = jnp.min(jnp.where(lt2 == v2, eid, float(N_EXPERTS)), axis=0, keepdims=True)
    ex = jnp.exp(v2 - v1)
    w1 = 1.0 / (1.0 + ex)
    w2 = ex / (1.0 + ex)
    oh1 = jnp.where(eid == e1, 1.0, 0.0)
    oh2 = jnp.where(eid == e2, 1.0, 0.0)
    mem = oh1 + oh2
    r = lax.broadcasted_iota(jnp.int32, (tm, tm), 0)
    cidx = lax.broadcasted_iota(jnp.int32, (tm, tm), 1)
    before = jnp.where(r < cidx, 1.0, 0.0).astype(BF16)
    excl = _dot(mem.astype(BF16), before) + run_ref[...][:, :1]
    rk1 = jnp.sum(oh1 * excl, axis=0, keepdims=True)
    rk2 = jnp.sum(oh2 * excl, axis=0, keepdims=True)
    run_ref[...] = run_ref[...] + jnp.sum(mem, axis=1, keepdims=True)
    zero = jnp.zeros_like(e1)
    meta_ref[...] = jnp.concatenate([e1, e2, rk1, rk2, w1, w2, zero, zero], axis=0)
    wpad = jnp.concatenate([w1, w2, jnp.zeros((6, tm), F32)], axis=0)
    gate_ref[...] = jnp.transpose(jnp.concatenate([wpad] * 16, axis=0))
    cnt_ref[...] = run_ref[...]


def _router(x, g, sc, sh, wr, *, rows_per_group):
    m, d = x.shape
    tm = _tile(rows_per_group, ROUTER_ROWS)
    gpt = rows_per_group // tm
    grp = lambda i: (i // gpt, 0, 0)
    return pl.pallas_call(
        _router_kernel,
        grid=(m // tm,),
        in_specs=[pl.BlockSpec((tm, d), lambda i: (i, 0)), pl.BlockSpec((1, d), lambda i: (0, 0)),
                  pl.BlockSpec((1, 1, d), grp), pl.BlockSpec((1, 1, d), grp),
                  pl.BlockSpec((d, 128), lambda i: (0, 0))],
        out_specs=[pl.BlockSpec((8, tm), lambda i: (0, i)),
                   pl.BlockSpec((tm, 128), lambda i: (i, 0)), pl.BlockSpec((N_EXPERTS, 128), lambda i: (0, 0))],
        out_shape=[jax.ShapeDtypeStruct((8, m), F32),
                   jax.ShapeDtypeStruct((m, 128), F32), jax.ShapeDtypeStruct((N_EXPERTS, 128), F32)],
        scratch_shapes=[pltpu.VMEM((N_EXPERTS, 128), F32)],
        compiler_params=pltpu.CompilerParams(dimension_semantics=("arbitrary",),
                                             vmem_limit_bytes=_vmem(6 * tm * d * 4)),
        name="moe_router",
    )(x, g.reshape(1, d), sc, sh, wr)


def _scatter_kernel(zs_ref, pos_ref, x_ref, g_ref, sc_ref, sh_ref, o_ref, hbuf, sem, *, group_rows, n_tiles):
    tm = hbuf.shape[0]

    def zero_tile(start):
        start = pl.multiple_of(start, 8)
        for c in range(group_rows // tm):
            cp = pltpu.make_async_copy(hbuf, o_ref.at[pl.ds(start + c * tm, tm)], sem.at[0])
            cp.start()
            cp.wait()

    @pl.when(pl.program_id(0) == 0)
    def _():
        hbuf[...] = jnp.zeros(hbuf.shape, hbuf.dtype)
        for e in range(N_EXPERTS):
            pl.when(zs_ref[e] >= 0)(functools.partial(zero_tile, zs_ref[e]))
            tail = zs_ref[N_EXPERTS] + e
            pl.when(tail < n_tiles)(functools.partial(zero_tile, tail * group_rows))

    hbuf[...] = _pack_rows(_rms(x_ref[...], g_ref[...]) * sc_ref[0] + sh_ref[0])

    def copies(t):
        return (pltpu.make_async_copy(hbuf.at[t], o_ref.at[pos_ref[0, 0, t]], sem.at[0]),
                pltpu.make_async_copy(hbuf.at[t], o_ref.at[pos_ref[0, 1, t]], sem.at[1]))

    def issue(t, carry):
        for cp in copies(t):
            cp.start()
        return carry

    lax.fori_loop(0, tm, issue, 0, unroll=ROW_DMA_UNROLL)

    def drain(t, carry):
        for cp in copies(t):
            cp.wait()
        return carry

    lax.fori_loop(0, tm, drain, 0, unroll=ROW_DMA_UNROLL)


def _scatter_rows(x, g, sc, sh, pos, zero_plan, *, n_tiles, group_rows, rows_per_group):
    m, d = x.shape
    nt, _, tm = pos.shape
    assert group_rows % tm == 0
    gpt = rows_per_group // tm
    grp = lambda i, zs: (i // gpt, 0, 0)
    pshape, pdtype = _packed_shape(tm, d)
    return pl.pallas_call(
        functools.partial(_scatter_kernel, group_rows=group_rows, n_tiles=n_tiles),
        grid_spec=pltpu.PrefetchScalarGridSpec(
            num_scalar_prefetch=1,
            grid=(nt,),
            in_specs=[pl.BlockSpec((1, 2, tm), lambda i, zs: (i, 0, 0), memory_space=pltpu.SMEM),
                      pl.BlockSpec((tm, d), lambda i, zs: (i, 0)), pl.BlockSpec((1, d), lambda i, zs: (0, 0)),
                      pl.BlockSpec((1, 1, d), grp), pl.BlockSpec((1, 1, d), grp)],
            out_specs=pl.BlockSpec(memory_space=pl.ANY),
            scratch_shapes=[pltpu.VMEM(pshape, pdtype), pltpu.SemaphoreType.DMA((2,))]),
        out_shape=jax.ShapeDtypeStruct((n_tiles * group_rows,) + pshape[1:], pdtype),
        compiler_params=pltpu.CompilerParams(dimension_semantics=("arbitrary",),
                                             vmem_limit_bytes=_vmem(6 * tm * d * 4)),
        name="moe_scatter",
    )(zero_plan, pos, x, g.reshape(1, d), sc, sh)


def _combine_kernel(pos_ref, y_ref, x_ref, gate_ref, gf_ref, gn_ref, o_ref, buf_ref, sem):
    tm = pos_ref.shape[2]

    def copies(t):
        return (pltpu.make_async_copy(y_ref.at[pos_ref[0, 0, t]], buf_ref.at[0, t], sem.at[0]),
                pltpu.make_async_copy(y_ref.at[pos_ref[0, 1, t]], buf_ref.at[1, t], sem.at[1]))

    def issue(t, carry):
        for cp in copies(t):
            cp.start()
        return carry

    lax.fori_loop(0, tm, issue, 0, unroll=ROW_DMA_UNROLL)

    def drain(t, carry):
        for cp in copies(t):
            cp.wait()
        return carry

    lax.fori_loop(0, tm, drain, 0, unroll=ROW_DMA_UNROLL)
    w = gate_ref[...]
    moe = w[:, 0:1] * _unpack_rows(buf_ref[0]) + w[:, 1:2] * _unpack_rows(buf_ref[1])
    o_ref[...] = _rms(x_ref[...] + gf_ref[0] * moe, gn_ref[...])


def _combine(y, pos, x, gate, gf, gn, *, rows_per_group):
    m, d = x.shape
    nt, _, tm = pos.shape
    gpt = rows_per_group // tm
    return pl.pallas_call(
        _combine_kernel,
        grid=(nt,),
        in_specs=[pl.BlockSpec((1, 2, tm), lambda i: (i, 0, 0), memory_space=pltpu.SMEM),
                  pl.BlockSpec(memory_space=pl.ANY),
                  pl.BlockSpec((tm, d), lambda i: (i, 0)), pl.BlockSpec((tm, 128), lambda i: (i, 0)),
                  pl.BlockSpec((1, 1, d), lambda i: (i // gpt, 0, 0)), pl.BlockSpec((1, d), lambda i: (0, 0))],
        out_specs=pl.BlockSpec((tm, d), lambda i: (i, 0)),
        out_shape=jax.ShapeDtypeStruct((m, d), F32),
        scratch_shapes=[pltpu.VMEM((2, tm) + y.shape[1:], y.dtype), pltpu.SemaphoreType.DMA((2,))],
        compiler_params=pltpu.CompilerParams(dimension_semantics=("arbitrary",),
                                             vmem_limit_bytes=_vmem(8 * tm * d * 4)),
        name="moe_combine",
    )(pos, y, x, gate, gf, gn.reshape(1, d))


def _axial_angles(rows, dim):
    row = jnp.repeat(jnp.arange(rows, dtype=F32), GRID_W)
    col = jnp.tile(jnp.arange(GRID_W, dtype=F32), rows)
    half = dim // 2
    inv = 1.0 / (ROPE_BASE ** (jnp.arange(0, half, 2, dtype=F32) / half))
    return jnp.concatenate([row[:, None] * inv, col[:, None] * inv], axis=-1)


def _rope_tables(rows, dim, t_ctx):
    ang = _axial_angles(rows, dim)
    cs = jnp.concatenate([jnp.cos(ang), jnp.cos(ang)], axis=1)
    sn = jnp.concatenate([-jnp.sin(ang), jnp.sin(ang)], axis=1)
    return (cs, sn), (jnp.ones((t_ctx, dim), F32), jnp.zeros((t_ctx, dim), F32))


def _evens_odds(w):
    pairs = w.reshape(w.shape[:-1] + (w.shape[-1] // 2, 2))
    return pairs[..., 0], pairs[..., 1]


def _even_weights(w_in, w_g2, b_g2, w_uq, w_ukv):
    d = w_in.shape[0]
    sizes = (512, 512, 1024, 1024, 16, 16, MLA_Q_RANK, MLA_KV_RANK, MLA_ROPE)
    offs = np.cumsum((0,) + sizes)
    gq, gk, gv, gr, gaf, gab, mq, mkv, mkr = [w_in[:, offs[i]:offs[i + 1]] for i in range(9)]
    kr_e, kr_o = _evens_odds(mkr)
    w_main = jnp.concatenate([gv, gr, gq, gk, mq, kr_e, kr_o, kr_o, kr_e, jnp.zeros((d, 128), F32), mkv],
                             axis=1).astype(BF16)
    w_small = jnp.concatenate([gaf, gab, jnp.zeros((d, EV_SMALL - 2 * GLA_RANK), F32)], axis=1)
    wg = jnp.zeros((2, EV_SMALL, GLA_HEADS * GLA_DK), F32)
    wg = wg.at[0, :GLA_RANK].set(w_g2[0]).at[1, GLA_RANK:2 * GLA_RANK].set(w_g2[1])
    bg = b_g2.reshape(2, 1, -1)
    wq = w_uq.reshape(MLA_Q_RANK, MLA_HEADS, MLA_DQK)
    qr_e, qr_o = _evens_odds(wq[:, :, MLA_NOPE:])
    wq = jnp.concatenate([wq[:, :, :MLA_NOPE], qr_e, qr_o, qr_o, qr_e], axis=2)
    wq = jnp.transpose(wq, (1, 0, 2)).astype(BF16)
    wkv = jnp.transpose(w_ukv.reshape(MLA_KV_RANK, MLA_HEADS, MLA_NOPE + MLA_DV), (1, 0, 2)).astype(BF16)
    return w_main, w_small, wg, bg, wq, wkv


def _perm_kernel(w_ref, p_ref, o_ref):
    o_ref[...] = _dot(w_ref[...].astype(BF16), p_ref[...]).astype(o_ref.dtype)


def _odd_weights(w_in):
    d, n = w_in.shape
    hd = RET_DK
    n_perm = 2 * RET_HEADS
    src = np.concatenate([np.arange(0, hd, 2), np.arange(1, hd, 2)])
    perm = np.zeros((2, hd, hd), np.float32)
    perm[0, src, np.arange(hd)] = 1.0
    perm[1] = np.eye(hd, dtype=np.float32)
    return pl.pallas_call(
        _perm_kernel,
        grid=(n // hd,),
        in_specs=[pl.BlockSpec((d, hd), lambda j: (0, j)),
                  pl.BlockSpec((None, hd, hd), lambda j: (jnp.where(j < n_perm, 0, 1), 0, 0))],
        out_specs=pl.BlockSpec((d, hd), lambda j: (0, j)),
        out_shape=jax.ShapeDtypeStruct((d, n), BF16),
        compiler_params=pltpu.CompilerParams(dimension_semantics=("parallel",)),
        name="weight_reorder",
    )(w_in, jnp.asarray(perm, BF16))


def _ret_tables(log_decay, c):
    lg = -jnp.exp(log_decay.astype(F32))
    pos = jnp.arange(c, dtype=F32)
    diff = pos[:, None] - pos[None, :]
    lgd = lg[:, :, None, None]
    fmask = jnp.where(diff >= 0, jnp.exp(lgd * jnp.maximum(diff, 0.0)), 0.0)
    bmask = jnp.where(diff <= 0, jnp.exp(lgd * jnp.maximum(-diff, 0.0)), 0.0)
    dm = jnp.stack([fmask[0], bmask[1]])
    lgc = lg[:, :, None, None]
    qd = jnp.stack([jnp.exp(lgc[0] * (pos + 1.0)[None, :, None]), jnp.exp(lgc[1] * (c - pos)[None, :, None])])
    kd = jnp.stack([jnp.exp(lgc[0] * (c - 1.0 - pos)[None, :, None]), jnp.exp(lgc[1] * pos[None, :, None])])
    cd = jnp.exp(lgc * c)
    return dm, qd, kd, cd


def _dispatch_plan(meta, counts, tm, n_tiles):
    cnt = counts[:, 0].astype(jnp.int32)
    tiles = (cnt + tm - 1) // tm
    tile_end = jnp.cumsum(tiles)
    start = (tile_end - tiles) * tm
    e = meta[0:2].astype(jnp.int32)
    start_e = sum(jnp.where(e == k, start[k], 0) for k in range(N_EXPERTS))
    pos = start_e + meta[2:4].astype(jnp.int32)
    tile_ids = jnp.arange(n_tiles, dtype=jnp.int32)
    n_valid = tile_end[-1:].astype(jnp.int32)
    last_valid = jnp.minimum(tile_ids, n_valid[0] - 1)
    te = jnp.sum((last_valid[:, None] >= tile_end[None, :]).astype(jnp.int32), axis=1)
    te = jnp.minimum(te, N_EXPERTS - 1).astype(jnp.int32)
    zero_start = jnp.where(tiles > 0, (tile_end - 1) * tm, -1).astype(jnp.int32)
    return pos, te, n_valid, jnp.concatenate([zero_start, n_valid])


def kernel(x, c, ctx, c_ctx, ada_w, ada_b, norm_mix, norm_ffn, norm_final, ev_w_in, gla_w_gate2, gla_b_gate2, gla_norm, mla_q_norm, mla_w_uq, mla_kv_norm, mla_w_ukv, ev_w_out, ffn_w_gate, ffn_w_up, ffn_w_down, od_w_in, ret_log_decay, ret_norm, od_w_out, moe_router, moe_w_gate, moe_w_up, moe_w_down):
    B, T, D = x.shape
    TC = ctx.shape[1]
    depth = ada_w.shape[0]
    assert depth == 2 and T % GRID_W == 0 and B <= 7
    rows = T // GRID_W
    xl = x.reshape(B * T, D)
    xc = ctx.reshape(B * TC, D)

    c_rows = jnp.concatenate([c, c_ctx[None, :], jnp.zeros((8 - B - 1, D), F32)], axis=0)
    m_all = _ada(c_rows, ada_w, ada_b).reshape(depth, 8, 6, 1, D)
    mods = [[(m_all[i, :B, k], m_all[i, B:B + 1, k]) for k in range(6)] for i in range(depth)]

    (sh_a, sc_a, g_a, sh_f, sc_f, g_f) = mods[0]
    w_main, w_small, wg2, bg2, wq, wkv = _even_weights(ev_w_in[0], gla_w_gate2[0], gla_b_gate2[0],
                                                       mla_w_uq[0], mla_w_ukv[0])
    tn_ev = _tile(EV_MAIN, 768)
    main_l, small_l = _nmm(xl, norm_mix[0], 1.0 + sc_a[0], sh_a[0], w_main, rows_per_group=T, tn=tn_ev,
                           w_small=w_small)
    main_c, small_c = _nmm(xc, norm_mix[0], 1.0 + sc_a[1], sh_a[1], w_main, rows_per_group=B * TC, tn=tn_ev,
                           w_small=w_small)

    s0 = jnp.zeros((B, GLA_HEADS, GLA_DV, GLA_DK), F32)
    ocf, ocb, s_f, s_b = _gla(main_c, small_c, wg2, bg2, s0, s0, batch=B, t=TC)
    olf, olb, _, _ = _gla(main_l, small_l, wg2, bg2, s_f, s_b, batch=B, t=T)

    (cs_m, sn_m), (cs_1, sn_0) = _rope_tables(rows, MLA_ROPE, TC)
    q_l, k_l, v_l = _mla_proj(main_l, mla_q_norm[0], mla_kv_norm[0], wq, wkv, cs_m, sn_m, t=T)
    q_c, k_c, v_c = _mla_proj(main_c, mla_q_norm[0], mla_kv_norm[0], wq, wkv, cs_1, sn_0, t=TC)
    mla_l = _attn(q_l, k_l, v_l, k_c, v_c)
    mla_c = _attn(q_c, k_c, v_c)

    w_out = ev_w_out[0].astype(BF16)
    gr_blk = EV_GR // (GLA_HEADS * GLA_DV)
    x1l = _mixout(olf, olb, main_l, gr_blk, gla_norm[0], mla_l, w_out, xl, g_a[0], rows_per_group=T,
                  n_heads=GLA_HEADS, dv=GLA_DV, center=False)
    x1c = _mixout(ocf, ocb, main_c, gr_blk, gla_norm[0], mla_c, w_out, xc, g_a[1], rows_per_group=B * TC,
                  n_heads=GLA_HEADS, dv=GLA_DV, center=False)

    fwg, fwu, fwd_ = ffn_w_gate.astype(BF16), ffn_w_up.astype(BF16), ffn_w_down.astype(BF16)
    tm_l = _tile(T, FFN_ROWS)
    x2l = _ffn(x1l, fwg, fwu, fwd_, jnp.zeros((B * T // tm_l,), jnp.int32), jnp.full((1,), B * T // tm_l, jnp.int32),
               tm=tm_l, norm=(norm_ffn[0], 1.0 + sc_f[0], sh_f[0], g_f[0], T))
    tm_c = _tile(B * TC, FFN_ROWS)
    x2c = _ffn(x1c, fwg, fwu, fwd_, jnp.zeros((B * TC // tm_c,), jnp.int32),
               jnp.full((1,), B * TC // tm_c, jnp.int32), tm=tm_c,
               norm=(norm_ffn[0], 1.0 + sc_f[1], sh_f[1], g_f[1], B * TC))

    (sh_a, sc_a, g_a, sh_f, sc_f, g_f) = mods[1]
    w_odd = _odd_weights(od_w_in[0])
    hk = RET_HEADS * RET_DK
    tn_od = _tile(4 * hk, 1024)
    rope_l, rope_c = _rope_tables(rows, RET_DK, B * TC)
    m2l = _nmm(x2l, norm_mix[1], 1.0 + sc_a[0], sh_a[0], w_odd, rows_per_group=T, tn=tn_od, rope=rope_l,
               rope_cols=2 * hk)
    m2c = _nmm(x2c, norm_mix[1], 1.0 + sc_a[1], sh_a[1], w_odd, rows_per_group=B * TC, tn=tn_od, rope=rope_c,
               rope_cols=2 * hk)
    s0 = jnp.zeros((B, RET_HEADS, RET_DK, RET_DV), F32)
    _, _, s_f, s_b = _ret(m2c, _ret_tables(ret_log_decay[0], min(RET_CHUNK, TC)), s0, s0, batch=B, t=TC)
    orf, orb, _, _ = _ret(m2l, _ret_tables(ret_log_decay[0], min(RET_CHUNK, T)), s_f, s_b, batch=B, t=T)
    x3 = _mixout(orf, orb, m2l, 3, ret_norm[0], None, od_w_out[0].astype(BF16), x2l, g_a[0], rows_per_group=T,
                 n_heads=RET_HEADS, dv=RET_DV, center=True)

    wr = jnp.concatenate([moe_router[0], jnp.zeros((D, 128 - N_EXPERTS), F32)], axis=1)
    meta, gate, counts = _router(x3, norm_ffn[1], 1.0 + sc_f[0], sh_f[0], wr, rows_per_group=T)
    tm_e = _tile(T, MOE_ROWS)
    n_tiles = (2 * B * T) // tm_e + N_EXPERTS
    pos, te, n_valid, zero_plan = _dispatch_plan(meta, counts, tm_e, n_tiles)
    tm_r = _tile(T, ROUTER_ROWS)
    pos_t = jnp.transpose(pos.reshape(2, -1, tm_r), (1, 0, 2))
    hs = _scatter_rows(x3, norm_ffn[1], 1.0 + sc_f[0], sh_f[0], pos_t, zero_plan, n_tiles=n_tiles,
                       group_rows=tm_e, rows_per_group=T)
    ys = _ffn(hs, moe_w_gate[0].astype(BF16), moe_w_up[0].astype(BF16), moe_w_down[0].astype(BF16), te, n_valid,
              tm=tm_e)
    out = _combine(ys, pos_t, x3, gate, g_f[0], norm_final, rows_per_group=T)
    return out.reshape(B, T, D)
```

```python
import functools

import jax
import jax.numpy as jnp
import numpy as np
from jax import lax
from jax.experimental import pallas as pl
from jax.experimental.pallas import tpu as pltpu

F32 = jnp.float32
BF16 = jnp.bfloat16
EPS = 1e-6
ROPE_BASE = 10000.0
GRID_W = 64

GLA_HEADS, GLA_DK, GLA_DV, GLA_RANK, GLA_GATE_NORM, GLA_CHUNK = 4, 128, 256, 16, 16.0, 64
MLA_HEADS, MLA_Q_RANK, MLA_KV_RANK, MLA_NOPE, MLA_ROPE, MLA_DV = 8, 768, 512, 128, 64, 128
MLA_DQK = MLA_NOPE + MLA_ROPE
MLA_SCALE = MLA_DQK ** -0.5
MLA_Q_SCALE = MLA_SCALE * 1.4426950408889634
RET_HEADS, RET_DK, RET_DV = 8, 256, 256
RET_CHUNK = 256
N_EXPERTS = 8

EV_GV, EV_GR, EV_GQ, EV_GK, EV_MQ, EV_KR, EV_MKV, EV_MAIN = 0, 1024, 2048, 2560, 3072, 3840, 4096, 4608
EV_SMALL = 128

V7X_VMEM_BYTES = 64 << 20
VMEM_HEADROOM_BYTES = 6 << 20
FFN_ROWS = 512
MOE_ROWS = 1024
FFN_SMALL_ROWS = 256
ROUTER_ROWS = 512
ROW_DMA_UNROLL = 8


def _vmem(nbytes):
    return int(min(V7X_VMEM_BYTES - VMEM_HEADROOM_BYTES, max(32 << 20, nbytes + (8 << 20))))


def _tile(n, pref, align=128):
    if n <= pref:
        return n
    t = (pref // align) * align
    while t >= align:
        if n % t == 0:
            return t
        t -= align
    return n


def _split_bf16(a):
    hi = a.astype(BF16)
    lo = (a - hi.astype(F32)).astype(BF16)
    return hi, lo


def _dot(a, b):
    return jnp.dot(a, b, preferred_element_type=F32)


def _dot3(a, b):
    ah, al = _split_bf16(a)
    bh, bl = _split_bf16(b)
    return _dot(ah, bh) + _dot(al, bh) + _dot(ah, bl)


def _dot_nt(a, b):
    return lax.dot_general(a, b, (((1,), (1,)), ((), ())), preferred_element_type=F32)


def _dot_tn(a, b):
    return lax.dot_general(a, b, (((0,), (0,)), ((), ())), preferred_element_type=F32)


def _silu(x):
    return x * (1.0 / (1.0 + jnp.exp(-x)))


def _log_sigmoid(z):
    return -(jnp.maximum(-z, 0.0) + jnp.log(1.0 + jnp.exp(-jnp.abs(z))))


def _rms(x, g):
    return x * lax.rsqrt(jnp.mean(x * x, axis=-1, keepdims=True) + EPS) * g


def _pack_rows(y):
    half = y.shape[1] // 2
    lo = lax.bitcast_convert_type(y[:, :half].astype(BF16).astype(F32), jnp.uint32)
    hi = lax.bitcast_convert_type(y[:, half:].astype(BF16).astype(F32), jnp.uint32)
    return lax.shift_right_logical(lo, jnp.uint32(16)) | hi


def _packed_shape(m, d):
    return (m, d // 2), jnp.uint32


def _unpack_rows(w):
    lo = lax.bitcast_convert_type(lax.shift_left(w, jnp.uint32(16)), F32)
    hi = lax.bitcast_convert_type(w & jnp.uint32(0xFFFF0000), F32)
    return jnp.concatenate([lo, hi], axis=1)


def _ada_kernel(c_ref, w_ref, b_ref, o_ref):
    o_ref[0] = _dot3(_silu(c_ref[...]), w_ref[0]) + b_ref[0]


def _ada(c_rows, w, b):
    nl, d, n = w.shape
    tn = _tile(n, 1536)
    return pl.pallas_call(
        _ada_kernel,
        grid=(nl, n // tn),
        in_specs=[pl.BlockSpec((8, d), lambda l, j: (0, 0)),
                  pl.BlockSpec((1, d, tn), lambda l, j: (l, 0, j)),
                  pl.BlockSpec((1, 1, tn), lambda l, j: (l, 0, j))],
        out_specs=pl.BlockSpec((1, 8, tn), lambda l, j: (l, 0, j)),
        out_shape=jax.ShapeDtypeStruct((nl, 8, n), F32),
        compiler_params=pltpu.CompilerParams(dimension_semantics=("parallel", "parallel"),
                                             vmem_limit_bytes=_vmem(2 * d * tn * 4 * 3)),
        name="ada_mod",
    )(c_rows, w, b.reshape(nl, 1, n))


def _nmm_kernel(*refs, has_small, rope_tiles, heads_per_tile):
    x_ref, g_ref, sc_ref, sh_ref, w_ref = refs[:5]
    k = 5
    if has_small:
        w2_ref = refs[k]; k += 1
    if rope_tiles:
        cs_ref, sn_ref = refs[k], refs[k + 1]; k += 2
    o_ref = refs[k]; k += 1
    if has_small:
        o2_ref = refs[k]; k += 1
    hn_ref = refs[k]
    j = pl.program_id(1)

    @pl.when(j == 0)
    def _():
        y = _rms(x_ref[...], g_ref[...]) * sc_ref[0] + sh_ref[0]
        hn_ref[...] = y.astype(BF16)
        if has_small:
            o2_ref[...] = _dot3(y, w2_ref[...])

    acc = _dot(hn_ref[...], w_ref[...])
    if rope_tiles:
        @pl.when(j < rope_tiles)
        def _():
            cs = cs_ref[...]
            sn = sn_ref[...]
            half = cs.shape[1] // 2
            outs = []
            for h in range(heads_per_tile):
                r = acc[:, h * 2 * half:(h + 1) * 2 * half]
                rs = jnp.concatenate([r[:, half:], r[:, :half]], axis=1)
                outs.append(r * cs + rs * sn)
            o_ref[...] = jnp.concatenate(outs, axis=1).astype(o_ref.dtype)

        @pl.when(j >= rope_tiles)
        def _():
            o_ref[...] = acc.astype(o_ref.dtype)
    else:
        o_ref[...] = acc.astype(o_ref.dtype)


def _nmm(x, g, sc, sh, w, *, rows_per_group, tn, w_small=None, rope=None, rope_cols=0):
    m, d = x.shape
    n = w.shape[1]
    tm = _tile(rows_per_group, 1024)
    gpt = rows_per_group // tm
    grp = lambda i, j: (i // gpt, 0, 0)
    in_specs = [pl.BlockSpec((tm, d), lambda i, j: (i, 0)),
                pl.BlockSpec((1, d), lambda i, j: (0, 0)),
                pl.BlockSpec((1, 1, d), grp),
                pl.BlockSpec((1, 1, d), grp),
                pl.BlockSpec((d, tn), lambda i, j: (0, j))]
    args = [x, g.reshape(1, d), sc, sh, w]
    out_specs = [pl.BlockSpec((tm, tn), lambda i, j: (i, j))]
    out_shape = [jax.ShapeDtypeStruct((m, n), BF16)]
    if w_small is not None:
        in_specs.append(pl.BlockSpec((d, EV_SMALL), lambda i, j: (0, 0)))
        args.append(w_small)
        out_specs.append(pl.BlockSpec((tm, EV_SMALL), lambda i, j: (i, 0)))
        out_shape.append(jax.ShapeDtypeStruct((m, EV_SMALL), F32))
    rope_tiles = heads_per_tile = 0
    if rope is not None:
        cs, sn = rope
        t_rows, hd = cs.shape
        rope_tiles, heads_per_tile = rope_cols // tn, tn // hd
        tpb = t_rows // tm
        in_specs += [pl.BlockSpec((tm, hd), lambda i, j: (i % tpb, 0)),
                     pl.BlockSpec((tm, hd), lambda i, j: (i % tpb, 0))]
        args += [cs, sn]
    kern = functools.partial(_nmm_kernel, has_small=w_small is not None, rope_tiles=rope_tiles,
                             heads_per_tile=heads_per_tile)
    res = pl.pallas_call(
        kern,
        grid=(m // tm, n // tn),
        in_specs=in_specs,
        out_specs=out_specs,
        out_shape=out_shape,
        scratch_shapes=[pltpu.VMEM((tm, d), BF16)],
        compiler_params=pltpu.CompilerParams(
            dimension_semantics=("parallel", "arbitrary"),
            vmem_limit_bytes=_vmem(2 * tm * d * 4 + tm * d * 2 + 2 * d * tn * 2 + 5 * tm * tn * 4)),
        name="norm_proj",
    )(*args)
    return res if w_small is not None else res[0]


def _gla_kernel(mf_q, mf_k, mf_v, sm_f, mb_q, mb_k, mb_v, sm_b, wg_ref, bg_ref, s0f_ref, s0b_ref,
                of_ref, ob_ref, sf_ref, sb_ref, st_ref, *, n_sub):
    s = pl.program_id(1)
    c = GLA_CHUNK

    @pl.when(s == 0)
    def _():
        st_ref[0] = s0f_ref[0]
        st_ref[1] = s0b_ref[0]

    rows = n_sub * c
    row = lax.broadcasted_iota(jnp.int32, (c, c), 0)
    col = lax.broadcasted_iota(jnp.int32, (c, c), 1)
    brow = lax.broadcasted_iota(jnp.int32, (rows, rows), 0)
    bcol = lax.broadcasted_iota(jnp.int32, (rows, rows), 1)
    same_chunk = (brow // c) == (bcol // c)
    for d, (q_ref, k_ref, v_ref, sm_ref, o_ref) in enumerate(
            ((mf_q, mf_k, mf_v, sm_f, of_ref), (mb_q, mb_k, mb_v, sm_b, ob_ref))):
        keep = (col <= row) if d == 0 else (col >= row)
        bkeep = jnp.logical_and(same_chunk, (bcol <= brow) if d == 0 else (bcol >= brow))
        tri = jnp.where(bkeep, 1.0, 0.0).astype(BF16)
        z = _dot3(sm_ref[...], wg_ref[d]) + bg_ref[d]
        lh, ll = _split_bf16(_log_sigmoid(z) * (1.0 / GLA_GATE_NORM))
        b_all = _dot(tri, lh) + _dot(tri, ll)
        order = range(n_sub) if d == 0 else range(n_sub - 1, -1, -1)
        for ci in order:
            r0 = ci * c
            for h in range(GLA_HEADS):
                b = b_all[r0:r0 + c, h * GLA_DK:(h + 1) * GLA_DK]
                b_last = b[c - 1:c, :] if d == 0 else b[0:1, :]
                q = q_ref[r0:r0 + c, h * GLA_DK:(h + 1) * GLA_DK].astype(F32) * (GLA_DK ** -0.5)
                k = k_ref[r0:r0 + c, h * GLA_DK:(h + 1) * GLA_DK].astype(F32)
                v = v_ref[r0:r0 + c, h * GLA_DV:(h + 1) * GLA_DV]
                qe = (q * jnp.exp(b)).astype(BF16)
                kd = (k * jnp.exp(-b)).astype(BF16)
                kl = (k * jnp.exp(b_last - b)).astype(BF16)
                att = jnp.where(keep, _dot_nt(qe, kd), 0.0).astype(BF16)
                st = st_ref[d, h]
                o = _dot(att, v) + _dot_nt(qe, st.astype(BF16))
                o_ref[r0:r0 + c, h * GLA_DV:(h + 1) * GLA_DV] = o.astype(o_ref.dtype)
                st_ref[d, h] = jnp.exp(b_last) * st + _dot_tn(v, kl)

    @pl.when(s == pl.num_programs(1) - 1)
    def _():
        sf_ref[0] = st_ref[0]
        sb_ref[0] = st_ref[1]


def _gla(main, small, wg, bg, s0f, s0b, *, batch, t):
    rows = _tile(t, 256, GLA_CHUNK)
    ns = t // rows
    hv, hk = GLA_HEADS * GLA_DV, GLA_HEADS * GLA_DK
    fwd = lambda blk: (lambda b, s: (b * ns + s, blk))
    bwd = lambda blk: (lambda b, s: (b * ns + ns - 1 - s, blk))
    st_spec = pl.BlockSpec((1, GLA_HEADS, GLA_DV, GLA_DK), lambda b, s: (b, 0, 0, 0))
    st_shape = jax.ShapeDtypeStruct((batch, GLA_HEADS, GLA_DV, GLA_DK), F32)

    def io(m):
        return [pl.BlockSpec((rows, hk), m(EV_GQ // hk)), pl.BlockSpec((rows, hk), m(EV_GK // hk)),
                pl.BlockSpec((rows, hv), m(EV_GV // hv)), pl.BlockSpec((rows, EV_SMALL), m(0))]

    return pl.pallas_call(
        functools.partial(_gla_kernel, n_sub=rows // GLA_CHUNK),
        grid=(batch, ns),
        in_specs=io(fwd) + io(bwd) + [
            pl.BlockSpec((2, EV_SMALL, hk), lambda b, s: (0, 0, 0)),
            pl.BlockSpec((2, 1, hk), lambda b, s: (0, 0, 0)),
            st_spec, st_spec],
        out_specs=[pl.BlockSpec((rows, hv), fwd(0)), pl.BlockSpec((rows, hv), bwd(0)), st_spec, st_spec],
        out_shape=[jax.ShapeDtypeStruct((batch * t, hv), BF16), jax.ShapeDtypeStruct((batch * t, hv), BF16),
                   st_shape, st_shape],
        scratch_shapes=[pltpu.VMEM((2, GLA_HEADS, GLA_DV, GLA_DK), F32)],
        compiler_params=pltpu.CompilerParams(dimension_semantics=("parallel", "arbitrary"),
                                             vmem_limit_bytes=_vmem(16 << 20)),
        name="gla_scan",
    )(main, main, main, small, main, main, main, small, wg, bg, s0f, s0b)


def _mla_proj_kernel(xq_ref, xkv_ref, kr_ref, gq_ref, gkv_ref, wq_ref, wkv_ref, cs_ref, sn_ref,
                     q_ref, k_ref, v_ref):
    cs, sn = cs_ref[...], sn_ref[...]
    hq = _rms(xq_ref[...].astype(F32), gq_ref[...]).astype(BF16)
    hkv = _rms(xkv_ref[...].astype(F32), gkv_ref[...]).astype(BF16)
    kr = kr_ref[...].astype(F32)
    krr = kr[:, :MLA_ROPE] * cs + kr[:, MLA_ROPE:2 * MLA_ROPE] * sn
    for h in range(MLA_HEADS):
        r = _dot(hq, wq_ref[h])
        qr = r[:, MLA_NOPE:MLA_NOPE + MLA_ROPE] * cs + r[:, MLA_NOPE + MLA_ROPE:] * sn
        q_ref[0, h] = (jnp.concatenate([r[:, :MLA_NOPE], qr], axis=1) * MLA_Q_SCALE).astype(q_ref.dtype)
        r = _dot(hkv, wkv_ref[h])
        k_ref[0, h] = jnp.concatenate([r[:, :MLA_NOPE], krr], axis=1).astype(k_ref.dtype)
        v_ref[0, h] = r[:, MLA_NOPE:].astype(v_ref.dtype)


def _mla_proj(main, gq, gkv, wq, wkv, cs, sn, *, t):
    m = main.shape[0]
    tm = _tile(t, 512)
    tpb = t // tm
    out = lambda i: (i // tpb, 0, i % tpb, 0)
    rope = lambda i: (i % tpb, 0)
    full = lambda a: pl.BlockSpec(a.shape, lambda i: (0,) * a.ndim)
    gq, gkv = gq.reshape(1, -1), gkv.reshape(1, -1)
    return pl.pallas_call(
        _mla_proj_kernel,
        grid=(m // tm,),
        in_specs=[pl.BlockSpec((tm, MLA_Q_RANK), lambda i: (i, EV_MQ // MLA_Q_RANK)),
                  pl.BlockSpec((tm, MLA_KV_RANK), lambda i: (i, EV_MKV // MLA_KV_RANK)),
                  pl.BlockSpec((tm, 256), lambda i: (i, EV_KR // 256)),
                  full(gq), full(gkv), full(wq), full(wkv),
                  pl.BlockSpec((tm, MLA_ROPE), rope), pl.BlockSpec((tm, MLA_ROPE), rope)],
        out_specs=[pl.BlockSpec((1, MLA_HEADS, tm, MLA_DQK), out),
                   pl.BlockSpec((1, MLA_HEADS, tm, MLA_DQK), out),
                   pl.BlockSpec((1, MLA_HEADS, tm, MLA_DV), out)],
        out_shape=[jax.ShapeDtypeStruct((m // t, MLA_HEADS, t, MLA_DQK), BF16),
                   jax.ShapeDtypeStruct((m // t, MLA_HEADS, t, MLA_DQK), BF16),
                   jax.ShapeDtypeStruct((m // t, MLA_HEADS, t, MLA_DV), BF16)],
        compiler_params=pltpu.CompilerParams(dimension_semantics=("parallel",),
                                             vmem_limit_bytes=_vmem(40 * MLA_HEADS * tm * 256)),
        name="mla_proj",
    )(main, main, main, gq, gkv, wq, wkv, cs, sn)


ATTN_ROWS = 256


def _attn_kernel(*refs, n_main, has_extra):
    if has_extra:
        q_ref, k_ref, v_ref, kx_ref, vx_ref, o_ref, m_ref, acc_ref = refs
    else:
        q_ref, k_ref, v_ref, o_ref, m_ref, acc_ref = refs
    kv = pl.program_id(3)

    @pl.when(kv == 0)
    def _():
        m_ref[...] = jnp.full(m_ref.shape, -jnp.inf, F32)
        acc_ref[...] = jnp.zeros(acc_ref.shape, F32)

    tq = q_ref.shape[2]
    rc = min(ATTN_ROWS, tq)

    key_values = [(k_ref[0, 0], v_ref[0, 0])]
    if has_extra:
        key_values.append((kx_ref[0, 0], vx_ref[0, 0]))
    extra_on = kv == n_main - 1
    ks = [k for k, _ in key_values]
    v_ones = [jnp.concatenate([v, jnp.ones(v.shape, v.dtype)], axis=1) for _, v in key_values]
    for r in range(tq // rc):
        rows = slice(r * rc, (r + 1) * rc)
        q = q_ref[0, 0, rows, :]
        ss = [_dot_nt(q, k) for k in ks]
        if has_extra and n_main > 1:
            ss[1] = jnp.where(extra_on, ss[1], -jnp.inf)
        m_old = m_ref[rows, :]
        m_new = m_old
        for s in ss:
            m_new = jnp.maximum(m_new, jnp.max(s, axis=-1, keepdims=True))
        pv = sum(_dot(jnp.exp2(s - m_new).astype(BF16), vo) for s, vo in zip(ss, v_ones))
        acc_ref[rows, :] = jnp.exp2(m_old - m_new) * acc_ref[rows, :] + pv
        m_ref[rows, :] = m_new

    @pl.when(kv == pl.num_programs(3) - 1)
    def _():
        acc = acc_ref[...]
        o_ref[...] = (acc[:, :MLA_DV] / acc[:, MLA_DV:]).astype(o_ref.dtype)


def _attn(q, k, v, kx=None, vx=None, *, tq_pref=2048, tk_pref=8192):
    b, h, tq_all, dqk = q.shape
    tk_all = k.shape[2]
    tq, tk = _tile(tq_all, tq_pref), _tile(tk_all, tk_pref)
    nq, n_main = tq_all // tq, tk_all // tk
    main = lambda bi, hi, i, j: (bi, hi, j, 0)
    in_specs = [pl.BlockSpec((1, 1, tq, dqk), lambda bi, hi, i, j: (bi, hi, i, 0)),
                pl.BlockSpec((1, 1, tk, dqk), main), pl.BlockSpec((1, 1, tk, MLA_DV), main)]
    args = [q, k, v]
    if kx is not None:
        tx = kx.shape[2]
        in_specs += [pl.BlockSpec((1, 1, tx, dqk), lambda bi, hi, i, j: (bi, hi, 0, 0)),
                     pl.BlockSpec((1, 1, tx, MLA_DV), lambda bi, hi, i, j: (bi, hi, 0, 0))]
        args += [kx, vx]
    return pl.pallas_call(
        functools.partial(_attn_kernel, n_main=n_main, has_extra=kx is not None),
        grid=(b, h, nq, n_main),
        in_specs=in_specs,
        out_specs=pl.BlockSpec((tq, MLA_DV), lambda bi, hi, i, j: (bi * nq + i, hi)),
        out_shape=jax.ShapeDtypeStruct((b * tq_all, h * MLA_DV), BF16),
        scratch_shapes=[pltpu.VMEM((tq, 1), F32), pltpu.VMEM((tq, 2 * MLA_DV), F32)],
        compiler_params=pltpu.CompilerParams(
            dimension_semantics=("parallel", "parallel", "parallel", "arbitrary"),
            vmem_limit_bytes=_vmem(5 * min(ATTN_ROWS, tq) * tk * 4 + 8 * tk * 256 * 2)),
        name="mla_attention",
    )(*args)


def _mixout_kernel(*refs, n_heads, dv, center, has_mla):
    of_ref, ob_ref, gate_ref, g_ref = refs[:4]
    k = 4
    if has_mla:
        mla_ref = refs[k]; k += 1
    w_ref, x_ref, ga_ref, o_ref = refs[k:k + 4]
    o = of_ref[...].astype(F32) + ob_ref[...].astype(F32)
    g = g_ref[...]
    parts = []
    for h in range(n_heads):
        oh = o[:, h * dv:(h + 1) * dv]
        if center:
            oh = oh - jnp.mean(oh, axis=-1, keepdims=True)
        parts.append(oh * lax.rsqrt(jnp.mean(oh * oh, axis=-1, keepdims=True) + EPS) * g)
    lat = jnp.concatenate(parts, axis=1) * _silu(gate_ref[...].astype(F32))
    lhs = lat.astype(BF16)
    if has_mla:
        lhs = jnp.concatenate([lhs, mla_ref[...]], axis=1)
    o_ref[...] = x_ref[...] + ga_ref[0] * _dot(lhs, w_ref[...])


def _mixout(o_f, o_b, main, gate_blk, g, mla, w, x, ga, *, rows_per_group, n_heads, dv, center):
    m, d = x.shape
    hw = n_heads * dv
    tm = _tile(rows_per_group, 512)
    gpt = rows_per_group // tm
    in_specs = [pl.BlockSpec((tm, hw), lambda i: (i, 0)), pl.BlockSpec((tm, hw), lambda i: (i, 0)),
                pl.BlockSpec((tm, hw), lambda i: (i, gate_blk)), pl.BlockSpec((1, dv), lambda i: (0, 0))]
    args = [o_f, o_b, main, g.reshape(1, dv)]
    if mla is not None:
        in_specs.append(pl.BlockSpec((tm, mla.shape[1]), lambda i: (i, 0)))
        args.append(mla)
    in_specs += [pl.BlockSpec(w.shape, lambda i: (0, 0)), pl.BlockSpec((tm, d), lambda i: (i, 0)),
                 pl.BlockSpec((1, 1, d), lambda i: (i // gpt, 0, 0))]
    args += [w, x, ga]
    return pl.pallas_call(
        functools.partial(_mixout_kernel, n_heads=n_heads, dv=dv, center=center, has_mla=mla is not None),
        grid=(m // tm,),
        in_specs=in_specs,
        out_specs=pl.BlockSpec((tm, d), lambda i: (i, 0)),
        out_shape=jax.ShapeDtypeStruct((m, d), F32),
        compiler_params=pltpu.CompilerParams(
            dimension_semantics=("parallel",),
            vmem_limit_bytes=_vmem(2 * w.size * 2 + 4 * tm * d * 4 + 8 * tm * hw * 2 + 6 * tm * hw * 4)),
        name="mixer_out",
    )(*args)


def _ffn_kernel(te_ref, nv_ref, *refs, prenorm):
    if prenorm:
        x_ref, g_ref, sc_ref, sh_ref, wg_ref, wu_ref, wd_ref, gf_ref, o_ref, hn_ref, acc_ref = refs
    else:
        x_ref, wg_ref, wu_ref, wd_ref, o_ref, hn_ref, acc_ref = refs
    i, j = pl.program_id(0), pl.program_id(1)
    valid = i < nv_ref[0]

    @pl.when(jnp.logical_and(valid, j == 0))
    def _():
        if prenorm:
            y = _rms(x_ref[...], g_ref[...]) * sc_ref[0] + sh_ref[0]
        else:
            y = _unpack_rows(x_ref[...])
        hn_ref[...] = y.astype(BF16)
        acc_ref[...] = jnp.zeros(acc_ref.shape, F32)

    def accumulate(rows):
        hn = hn_ref[rows, :]
        a = (_silu(_dot(hn, wg_ref[0])) * _dot(hn, wu_ref[0])).astype(BF16)
        acc_ref[rows, :] += _dot(a, wd_ref[0])

    small_rows = min(FFN_SMALL_ROWS, hn_ref.shape[0])
    if prenorm or small_rows == hn_ref.shape[0]:
        pl.when(valid)(lambda: accumulate(slice(None)))
    else:
        few = nv_ref[1 + i] <= small_rows
        pl.when(jnp.logical_and(valid, jnp.logical_not(few)))(lambda: accumulate(slice(None)))
        pl.when(jnp.logical_and(valid, few))(lambda: accumulate(slice(0, small_rows)))

    last = j == pl.num_programs(1) - 1

    @pl.when(jnp.logical_and(valid, last))
    def _():
        if prenorm:
            o_ref[...] = x_ref[...] + gf_ref[0] * acc_ref[...]
        else:
            o_ref[...] = _pack_rows(acc_ref[...])

    @pl.when(jnp.logical_and(jnp.logical_not(valid), last))
    def _():
        o_ref[...] = jnp.zeros(o_ref.shape, o_ref.dtype)


def _ffn(x, wg, wu, wd, tile_expert, n_valid, *, tm, norm=None):
    p = x.shape[0]
    d = wg.shape[1]
    f = wg.shape[2]
    tf = _tile(f, 512)
    nt = p // tm
    wmap_in = lambda i, j, te, nv: (te[i], 0, j)
    wmap_out = lambda i, j, te, nv: (te[i], j, 0)
    tail = (0,) * (x.ndim - 1)
    row_block = (tm,) + x.shape[1:]
    row = lambda i, j, te, nv: (i,) + tail
    in_specs = [pl.BlockSpec(row_block, lambda i, j, te, nv: (jnp.minimum(i, nv[0] - 1),) + tail)]
    args = [x]
    if norm is not None:
        g, sc, sh, gf, rpg = norm
        gpt = rpg // tm
        grp = lambda i, j, te, nv: (i // gpt, 0, 0)
        in_specs += [pl.BlockSpec((1, d), lambda i, j, te, nv: (0, 0)), pl.BlockSpec((1, 1, d), grp),
                     pl.BlockSpec((1, 1, d), grp)]
        args += [g.reshape(1, d), sc, sh]
    in_specs += [pl.BlockSpec((1, d, tf), wmap_in), pl.BlockSpec((1, d, tf), wmap_in),
                 pl.BlockSpec((1, tf, d), wmap_out)]
    args += [wg, wu, wd]
    if norm is not None:
        in_specs.append(pl.BlockSpec((1, 1, d), grp))
        args.append(gf)
    return pl.pallas_call(
        functools.partial(_ffn_kernel, prenorm=norm is not None),
        grid_spec=pltpu.PrefetchScalarGridSpec(
            num_scalar_prefetch=2,
            grid=(nt, f // tf),
            in_specs=in_specs,
            out_specs=pl.BlockSpec(row_block, row),
            scratch_shapes=[pltpu.VMEM((tm, d), BF16), pltpu.VMEM((tm, d), F32)]),
        out_shape=jax.ShapeDtypeStruct(x.shape, x.dtype),
        compiler_params=pltpu.CompilerParams(
            dimension_semantics=("parallel", "arbitrary"),
            vmem_limit_bytes=_vmem(16 * x.size // nt + tm * d * 6 + 6 * d * tf * 2 + 5 * tm * tf * 4)),
        name="swiglu_ffn",
    )(tile_expert, n_valid, *args)


def _ret_kernel(qf_ref, kf_ref, vf_ref, qb_ref, kb_ref, vb_ref, dm_ref, qd_ref, kd_ref, cd_ref,
                s0f_ref, s0b_ref, of_ref, ob_ref, sf_ref, sb_ref, st_ref, *, n_sub):
    s = pl.program_id(1)
    c = dm_ref.shape[2]

    @pl.when(s == 0)
    def _():
        st_ref[0] = s0f_ref[0]
        st_ref[1] = s0b_ref[0]

    for d, (q_ref, k_ref, v_ref, o_ref) in enumerate(((qf_ref, kf_ref, vf_ref, of_ref),
                                                       (qb_ref, kb_ref, vb_ref, ob_ref))):
        order = range(n_sub) if d == 0 else range(n_sub - 1, -1, -1)
        for ci in order:
            r0 = ci * c
            for h in range(RET_HEADS):
                q = q_ref[r0:r0 + c, h * RET_DK:(h + 1) * RET_DK]
                k = (k_ref[r0:r0 + c, h * RET_DK:(h + 1) * RET_DK].astype(F32) * (RET_DK ** -0.5)).astype(BF16)
                v = v_ref[r0:r0 + c, h * RET_DV:(h + 1) * RET_DV]
                att = (_dot_nt(q, k) * dm_ref[d, h]).astype(BF16)
                st = st_ref[d, h]
                o = _dot(att, v) + qd_ref[d, h] * _dot(q, st.astype(BF16))
                o_ref[r0:r0 + c, h * RET_DV:(h + 1) * RET_DV] = o.astype(o_ref.dtype)
                vk = (v.astype(F32) * kd_ref[d, h]).astype(BF16)
                st_ref[d, h] = cd_ref[d, h] * st + _dot_tn(k, vk)

    @pl.when(s == pl.num_programs(1) - 1)
    def _():
        sf_ref[0] = st_ref[0]
        sb_ref[0] = st_ref[1]


def _ret(main, tables, s0f, s0b, *, batch, t):
    chunk = tables[0].shape[2]
    rows = _tile(t, 256, chunk)
    ns = t // rows
    hk, hv = RET_HEADS * RET_DK, RET_HEADS * RET_DV
    fwd = lambda blk: (lambda b, s: (b * ns + s, blk))
    bwd = lambda blk: (lambda b, s: (b * ns + ns - 1 - s, blk))
    st_spec = pl.BlockSpec((1, RET_HEADS, RET_DK, RET_DV), lambda b, s: (b, 0, 0, 0))
    st_shape = jax.ShapeDtypeStruct((batch, RET_HEADS, RET_DK, RET_DV), F32)
    io = lambda m: [pl.BlockSpec((rows, hk), m(0)), pl.BlockSpec((rows, hk), m(1)), pl.BlockSpec((rows, hv), m(2))]
    full = lambda a: pl.BlockSpec(a.shape, lambda b, s: (0,) * a.ndim)
    return pl.pallas_call(
        functools.partial(_ret_kernel, n_sub=rows // chunk),
        grid=(batch, ns),
        in_specs=io(fwd) + io(bwd) + [full(a) for a in tables] + [st_spec, st_spec],
        out_specs=[pl.BlockSpec((rows, hv), fwd(0)), pl.BlockSpec((rows, hv), bwd(0)), st_spec, st_spec],
        out_shape=[jax.ShapeDtypeStruct((batch * t, hv), BF16), jax.ShapeDtypeStruct((batch * t, hv), BF16),
                   st_shape, st_shape],
        scratch_shapes=[pltpu.VMEM((2, RET_HEADS, RET_DK, RET_DV), F32)],
        compiler_params=pltpu.CompilerParams(dimension_semantics=("parallel", "arbitrary"),
                                             vmem_limit_bytes=_vmem(46 << 20)),
        name="retention_scan",
    )(main, main, main, main, main, main, *tables, s0f, s0b)


def _router_kernel(x_ref, g_ref, sc_ref, sh_ref, wr_ref, meta_ref, gate_ref, cnt_ref, run_ref):
    i = pl.program_id(0)

    @pl.when(i == 0)
    def _():
        run_ref[...] = jnp.zeros(run_ref.shape, F32)

    y = _rms(x_ref[...], g_ref[...]) * sc_ref[0] + sh_ref[0]
    tm = y.shape[0]
    lt = jnp.transpose(_dot3(y, wr_ref[...]))[:N_EXPERTS, :]
    eid = lax.broadcasted_iota(jnp.int32, lt.shape, 0).astype(F32)
    v1 = jnp.max(lt, axis=0, keepdims=True)
    e1 = jnp.min(jnp.where(lt == v1, eid, float(N_EXPERTS)), axis=0, keepdims=True)
    lt2 = jnp.where(eid == e1, -jnp.inf, lt)
    v2 = jnp.max(lt2, axis=0, keepdims=True)
    e2 = jnp.min(jnp.where(lt2 == v2, eid, float(N_EXPERTS)), axis=0, keepdims=True)
    ex = jnp.exp(v2 - v1)
    w1 = 1.0 / (1.0 + ex)
    w2 = ex / (1.0 + ex)
    oh1 = jnp.where(eid == e1, 1.0, 0.0)
    oh2 = jnp.where(eid == e2, 1.0, 0.0)
    mem = oh1 + oh2
    r = lax.broadcasted_iota(jnp.int32, (tm, tm), 0)
    cidx = lax.broadcasted_iota(jnp.int32, (tm, tm), 1)
    before = jnp.where(r < cidx, 1.0, 0.0).astype(BF16)
    excl = _dot(mem.astype(BF16), before) + run_ref[...][:, :1]
    rk1 = jnp.sum(oh1 * excl, axis=0, keepdims=True)
    rk2 = jnp.sum(oh2 * excl, axis=0, keepdims=True)
    run_ref[...] = run_ref[...] + jnp.sum(mem, axis=1, keepdims=True)
    zero = jnp.zeros_like(e1)
    meta_ref[...] = jnp.concatenate([e1, e2, rk1, rk2, w1, w2, zero, zero], axis=0)
    wpad = jnp.concatenate([w1, w2, jnp.zeros((6, tm), F32)], axis=0)
    gate_ref[...] = jnp.transpose(jnp.concatenate([wpad] * 16, axis=0))
    cnt_ref[...] = run_ref[...]


def _router(x, g, sc, sh, wr, *, rows_per_group):
    m, d = x.shape
    tm = _tile(rows_per_group, ROUTER_ROWS)
    gpt = rows_per_group // tm
    grp = lambda i: (i // gpt, 0, 0)
    return pl.pallas_call(
        _router_kernel,
        grid=(m // tm,),
        in_specs=[pl.BlockSpec((tm, d), lambda i: (i, 0)), pl.BlockSpec((1, d), lambda i: (0, 0)),
                  pl.BlockSpec((1, 1, d), grp), pl.BlockSpec((1, 1, d), grp),
                  pl.BlockSpec((d, 128), lambda i: (0, 0))],
        out_specs=[pl.BlockSpec((8, tm), lambda i: (0, i)),
                   pl.BlockSpec((tm, 128), lambda i: (i, 0)), pl.BlockSpec((N_EXPERTS, 128), lambda i: (0, 0))],
        out_shape=[jax.ShapeDtypeStruct((8, m), F32),
                   jax.ShapeDtypeStruct((m, 128), F32), jax.ShapeDtypeStruct((N_EXPERTS, 128), F32)],
        scratch_shapes=[pltpu.VMEM((N_EXPERTS, 128), F32)],
        compiler_params=pltpu.CompilerParams(dimension_semantics=("arbitrary",),
                                             vmem_limit_bytes=_vmem(6 * tm * d * 4)),
        name="moe_router",
    )(x, g.reshape(1, d), sc, sh, wr)


def _scatter_kernel(zs_ref, pos_ref, x_ref, g_ref, sc_ref, sh_ref, o_ref, hbuf, sem, *, group_rows, n_tiles):
    tm = hbuf.shape[0]

    def zero_tile(start):
        start = pl.multiple_of(start, 8)
        for c in range(group_rows // tm):
            cp = pltpu.make_async_copy(hbuf, o_ref.at[pl.ds(start + c * tm, tm)], sem.at[0])
            cp.start()
            cp.wait()

    @pl.when(pl.program_id(0) == 0)
    def _():
        hbuf[...] = jnp.zeros(hbuf.shape, hbuf.dtype)
        for e in range(N_EXPERTS):
            pl.when(zs_ref[e] >= 0)(functools.partial(zero_tile, zs_ref[e]))
            tail = zs_ref[N_EXPERTS] + e
            pl.when(tail < n_tiles)(functools.partial(zero_tile, tail * group_rows))

    hbuf[...] = _pack_rows(_rms(x_ref[...], g_ref[...]) * sc_ref[0] + sh_ref[0])

    def copies(t):
        return (pltpu.make_async_copy(hbuf.at[t], o_ref.at[pos_ref[0, 0, t]], sem.at[0]),
                pltpu.make_async_copy(hbuf.at[t], o_ref.at[pos_ref[0, 1, t]], sem.at[1]))

    def issue(t, carry):
        for cp in copies(t):
            cp.start()
        return carry

    lax.fori_loop(0, tm, issue, 0, unroll=ROW_DMA_UNROLL)

    def drain(t, carry):
        for cp in copies(t):
            cp.wait()
        return carry

    lax.fori_loop(0, tm, drain, 0, unroll=ROW_DMA_UNROLL)


def _scatter_rows(x, g, sc, sh, pos, zero_plan, *, n_tiles, group_rows, rows_per_group):
    m, d = x.shape
    nt, _, tm = pos.shape
    assert group_rows % tm == 0
    gpt = rows_per_group // tm
    grp = lambda i, zs: (i // gpt, 0, 0)
    pshape, pdtype = _packed_shape(tm, d)
    return pl.pallas_call(
        functools.partial(_scatter_kernel, group_rows=group_rows, n_tiles=n_tiles),
        grid_spec=pltpu.PrefetchScalarGridSpec(
            num_scalar_prefetch=1,
            grid=(nt,),
            in_specs=[pl.BlockSpec((1, 2, tm), lambda i, zs: (i, 0, 0), memory_space=pltpu.SMEM),
                      pl.BlockSpec((tm, d), lambda i, zs: (i, 0)), pl.BlockSpec((1, d), lambda i, zs: (0, 0)),
                      pl.BlockSpec((1, 1, d), grp), pl.BlockSpec((1, 1, d), grp)],
            out_specs=pl.BlockSpec(memory_space=pl.ANY),
            scratch_shapes=[pltpu.VMEM(pshape, pdtype), pltpu.SemaphoreType.DMA((2,))]),
        out_shape=jax.ShapeDtypeStruct((n_tiles * group_rows,) + pshape[1:], pdtype),
        compiler_params=pltpu.CompilerParams(dimension_semantics=("arbitrary",),
                                             vmem_limit_bytes=_vmem(6 * tm * d * 4)),
        name="moe_scatter",
    )(zero_plan, pos, x, g.reshape(1, d), sc, sh)


def _combine_kernel(pos_ref, y_ref, x_ref, gate_ref, gf_ref, gn_ref, o_ref, buf_ref, sem):
    tm = pos_ref.shape[2]

    def copies(t):
        return (pltpu.make_async_copy(y_ref.at[pos_ref[0, 0, t]], buf_ref.at[0, t], sem.at[0]),
                pltpu.make_async_copy(y_ref.at[pos_ref[0, 1, t]], buf_ref.at[1, t], sem.at[1]))

    def issue(t, carry):
        for cp in copies(t):
            cp.start()
        return carry

    lax.fori_loop(0, tm, issue, 0, unroll=ROW_DMA_UNROLL)

    def drain(t, carry):
        for cp in copies(t):
            cp.wait()
        return carry

    lax.fori_loop(0, tm, drain, 0, unroll=ROW_DMA_UNROLL)
    w = gate_ref[...]
    moe = w[:, 0:1] * _unpack_rows(buf_ref[0]) + w[:, 1:2] * _unpack_rows(buf_ref[1])
    o_ref[...] = _rms(x_ref[...] + gf_ref[0] * moe, gn_ref[...])


def _combine(y, pos, x, gate, gf, gn, *, rows_per_group):
    m, d = x.shape
    nt, _, tm = pos.shape
    gpt = rows_per_group // tm
    return pl.pallas_call(
        _combine_kernel,
        grid=(nt,),
        in_specs=[pl.BlockSpec((1, 2, tm), lambda i: (i, 0, 0), memory_space=pltpu.SMEM),
                  pl.BlockSpec(memory_space=pl.ANY),
                  pl.BlockSpec((tm, d), lambda i: (i, 0)), pl.BlockSpec((tm, 128), lambda i: (i, 0)),
                  pl.BlockSpec((1, 1, d), lambda i: (i // gpt, 0, 0)), pl.BlockSpec((1, d), lambda i: (0, 0))],
        out_specs=pl.BlockSpec((tm, d), lambda i: (i, 0)),
        out_shape=jax.ShapeDtypeStruct((m, d), F32),
        scratch_shapes=[pltpu.VMEM((2, tm) + y.shape[1:], y.dtype), pltpu.SemaphoreType.DMA((2,))],
        compiler_params=pltpu.CompilerParams(dimension_semantics=("arbitrary",),
                                             vmem_limit_bytes=_vmem(8 * tm * d * 4)),
        name="moe_combine",
    )(pos, y, x, gate, gf, gn.reshape(1, d))


def _axial_angles(rows, dim):
    row = jnp.repeat(jnp.arange(rows, dtype=F32), GRID_W)
    col = jnp.tile(jnp.arange(GRID_W, dtype=F32), rows)
    half = dim // 2
    inv = 1.0 / (ROPE_BASE ** (jnp.arange(0, half, 2, dtype=F32) / half))
    return jnp.concatenate([row[:, None] * inv, col[:, None] * inv], axis=-1)


def _rope_tables(rows, dim, t_ctx):
    ang = _axial_angles(rows, dim)
    cs = jnp.concatenate([jnp.cos(ang), jnp.cos(ang)], axis=1)
    sn = jnp.concatenate([-jnp.sin(ang), jnp.sin(ang)], axis=1)
    return (cs, sn), (jnp.ones((t_ctx, dim), F32), jnp.zeros((t_ctx, dim), F32))


def _evens_odds(w):
    pairs = w.reshape(w.shape[:-1] + (w.shape[-1] // 2, 2))
    return pairs[..., 0], pairs[..., 1]


def _even_weights(w_in, w_g2, b_g2, w_uq, w_ukv):
    d = w_in.shape[0]
    sizes = (512, 512, 1024, 1024, 16, 16, MLA_Q_RANK, MLA_KV_RANK, MLA_ROPE)
    offs = np.cumsum((0,) + sizes)
    gq, gk, gv, gr, gaf, gab, mq, mkv, mkr = [w_in[:, offs[i]:offs[i + 1]] for i in range(9)]
    kr_e, kr_o = _evens_odds(mkr)
    w_main = jnp.concatenate([gv, gr, gq, gk, mq, kr_e, kr_o, kr_o, kr_e, jnp.zeros((d, 128), F32), mkv],
                             axis=1).astype(BF16)
    w_small = jnp.concatenate([gaf, gab, jnp.zeros((d, EV_SMALL - 2 * GLA_RANK), F32)], axis=1)
    wg = jnp.zeros((2, EV_SMALL, GLA_HEADS * GLA_DK), F32)
    wg = wg.at[0, :GLA_RANK].set(w_g2[0]).at[1, GLA_RANK:2 * GLA_RANK].set(w_g2[1])
    bg = b_g2.reshape(2, 1, -1)
    wq = w_uq.reshape(MLA_Q_RANK, MLA_HEADS, MLA_DQK)
    qr_e, qr_o = _evens_odds(wq[:, :, MLA_NOPE:])
    wq = jnp.concatenate([wq[:, :, :MLA_NOPE], qr_e, qr_o, qr_o, qr_e], axis=2)
    wq = jnp.transpose(wq, (1, 0, 2)).astype(BF16)
    wkv = jnp.transpose(w_ukv.reshape(MLA_KV_RANK, MLA_HEADS, MLA_NOPE + MLA_DV), (1, 0, 2)).astype(BF16)
    return w_main, w_small, wg, bg, wq, wkv


def _perm_kernel(w_ref, p_ref, o_ref):
    o_ref[...] = _dot(w_ref[...].astype(BF16), p_ref[...]).astype(o_ref.dtype)


def _odd_weights(w_in):
    d, n = w_in.shape
    hd = RET_DK
    n_perm = 2 * RET_HEADS
    src = np.concatenate([np.arange(0, hd, 2), np.arange(1, hd, 2)])
    perm = np.zeros((2, hd, hd), np.float32)
    perm[0, src, np.arange(hd)] = 1.0
    perm[1] = np.eye(hd, dtype=np.float32)
    return pl.pallas_call(
        _perm_kernel,
        grid=(n // hd,),
        in_specs=[pl.BlockSpec((d, hd), lambda j: (0, j)),
                  pl.BlockSpec((None, hd, hd), lambda j: (jnp.where(j < n_perm, 0, 1), 0, 0))],
        out_specs=pl.BlockSpec((d, hd), lambda j: (0, j)),
        out_shape=jax.ShapeDtypeStruct((d, n), BF16),
        compiler_params=pltpu.CompilerParams(dimension_semantics=("parallel",)),
        name="weight_reorder",
    )(w_in, jnp.asarray(perm, BF16))


def _ret_tables(log_decay, c):
    lg = -jnp.exp(log_decay.astype(F32))
    pos = jnp.arange(c, dtype=F32)
    diff = pos[:, None] - pos[None, :]
    lgd = lg[:, :, None, None]
    fmask = jnp.where(diff >= 0, jnp.exp(lgd * jnp.maximum(diff, 0.0)), 0.0)
    bmask = jnp.where(diff <= 0, jnp.exp(lgd * jnp.maximum(-diff, 0.0)), 0.0)
    dm = jnp.stack([fmask[0], bmask[1]])
    lgc = lg[:, :, None, None]
    qd = jnp.stack([jnp.exp(lgc[0] * (pos + 1.0)[None, :, None]), jnp.exp(lgc[1] * (c - pos)[None, :, None])])
    kd = jnp.stack([jnp.exp(lgc[0] * (c - 1.0 - pos)[None, :, None]), jnp.exp(lgc[1] * pos[None, :, None])])
    cd = jnp.exp(lgc * c)
    return dm, qd, kd, cd


def _dispatch_plan(meta, counts, tm, n_tiles):
    cnt = counts[:, 0].astype(jnp.int32)
    tiles = (cnt + tm - 1) // tm
    tile_end = jnp.cumsum(tiles)
    start = (tile_end - tiles) * tm
    e = meta[0:2].astype(jnp.int32)
    start_e = sum(jnp.where(e == k, start[k], 0) for k in range(N_EXPERTS))
    pos = start_e + meta[2:4].astype(jnp.int32)
    tile_ids = jnp.arange(n_tiles, dtype=jnp.int32)
    n_valid = tile_end[-1:].astype(jnp.int32)
    last_valid = jnp.minimum(tile_ids, n_valid[0] - 1)
    te = jnp.sum((last_valid[:, None] >= tile_end[None, :]).astype(jnp.int32), axis=1)
    te = jnp.minimum(te, N_EXPERTS - 1).astype(jnp.int32)
    zero_start = jnp.where(tiles > 0, (tile_end - 1) * tm, -1).astype(jnp.int32)
    first_tile = sum(jnp.where(te == k, (tile_end - tiles)[k], 0) for k in range(N_EXPERTS))
    cnt_te = sum(jnp.where(te == k, cnt[k], 0) for k in range(N_EXPERTS))
    tile_rows = jnp.clip(cnt_te - (tile_ids - first_tile) * tm, 0, tm).astype(jnp.int32)
    return pos, te, jnp.concatenate([n_valid, tile_rows]), jnp.concatenate([zero_start, n_valid])


def kernel(x, c, ctx, c_ctx, ada_w, ada_b, norm_mix, norm_ffn, norm_final, ev_w_in, gla_w_gate2, gla_b_gate2, gla_norm, mla_q_norm, mla_w_uq, mla_kv_norm, mla_w_ukv, ev_w_out, ffn_w_gate, ffn_w_up, ffn_w_down, od_w_in, ret_log_decay, ret_norm, od_w_out, moe_router, moe_w_gate, moe_w_up, moe_w_down):
    B, T, D = x.shape
    TC = ctx.shape[1]
    depth = ada_w.shape[0]
    assert depth == 2 and T % GRID_W == 0 and B <= 7
    rows = T // GRID_W
    xl = x.reshape(B * T, D)
    xc = ctx.reshape(B * TC, D)

    c_rows = jnp.concatenate([c, c_ctx[None, :], jnp.zeros((8 - B - 1, D), F32)], axis=0)
    m_all = _ada(c_rows, ada_w, ada_b).reshape(depth, 8, 6, 1, D)
    mods = [[(m_all[i, :B, k], m_all[i, B:B + 1, k]) for k in range(6)] for i in range(depth)]

    (sh_a, sc_a, g_a, sh_f, sc_f, g_f) = mods[0]
    w_main, w_small, wg2, bg2, wq, wkv = _even_weights(ev_w_in[0], gla_w_gate2[0], gla_b_gate2[0],
                                                       mla_w_uq[0], mla_w_ukv[0])
    tn_ev = _tile(EV_MAIN, 768)
    main_l, small_l = _nmm(xl, norm_mix[0], 1.0 + sc_a[0], sh_a[0], w_main, rows_per_group=T, tn=tn_ev,
                           w_small=w_small)
    main_c, small_c = _nmm(xc, norm_mix[0], 1.0 + sc_a[1], sh_a[1], w_main, rows_per_group=B * TC, tn=tn_ev,
                           w_small=w_small)

    s0 = jnp.zeros((B, GLA_HEADS, GLA_DV, GLA_DK), F32)
    ocf, ocb, s_f, s_b = _gla(main_c, small_c, wg2, bg2, s0, s0, batch=B, t=TC)
    olf, olb, _, _ = _gla(main_l, small_l, wg2, bg2, s_f, s_b, batch=B, t=T)

    (cs_m, sn_m), (cs_1, sn_0) = _rope_tables(rows, MLA_ROPE, TC)
    q_l, k_l, v_l = _mla_proj(main_l, mla_q_norm[0], mla_kv_norm[0], wq, wkv, cs_m, sn_m, t=T)
    q_c, k_c, v_c = _mla_proj(main_c, mla_q_norm[0], mla_kv_norm[0], wq, wkv, cs_1, sn_0, t=TC)
    mla_l = _attn(q_l, k_l, v_l, k_c, v_c)
    mla_c = _attn(q_c, k_c, v_c)

    w_out = ev_w_out[0].astype(BF16)
    gr_blk = EV_GR // (GLA_HEADS * GLA_DV)
    x1l = _mixout(olf, olb, main_l, gr_blk, gla_norm[0], mla_l, w_out, xl, g_a[0], rows_per_group=T,
                  n_heads=GLA_HEADS, dv=GLA_DV, center=False)
    x1c = _mixout(ocf, ocb, main_c, gr_blk, gla_norm[0], mla_c, w_out, xc, g_a[1], rows_per_group=B * TC,
                  n_heads=GLA_HEADS, dv=GLA_DV, center=False)

    fwg, fwu, fwd_ = ffn_w_gate.astype(BF16), ffn_w_up.astype(BF16), ffn_w_down.astype(BF16)
    tm_l = _tile(T, FFN_ROWS)
    x2l = _ffn(x1l, fwg, fwu, fwd_, jnp.zeros((B * T // tm_l,), jnp.int32), jnp.full((1,), B * T // tm_l, jnp.int32),
               tm=tm_l, norm=(norm_ffn[0], 1.0 + sc_f[0], sh_f[0], g_f[0], T))
    tm_c = _tile(B * TC, FFN_ROWS)
    x2c = _ffn(x1c, fwg, fwu, fwd_, jnp.zeros((B * TC // tm_c,), jnp.int32),
               jnp.full((1,), B * TC // tm_c, jnp.int32), tm=tm_c,
               norm=(norm_ffn[0], 1.0 + sc_f[1], sh_f[1], g_f[1], B * TC))

    (sh_a, sc_a, g_a, sh_f, sc_f, g_f) = mods[1]
    w_odd = _odd_weights(od_w_in[0])
    hk = RET_HEADS * RET_DK
    tn_od = _tile(4 * hk, 1024)
    rope_l, rope_c = _rope_tables(rows, RET_DK, B * TC)
    m2l = _nmm(x2l, norm_mix[1], 1.0 + sc_a[0], sh_a[0], w_odd, rows_per_group=T, tn=tn_od, rope=rope_l,
               rope_cols=2 * hk)
    m2c = _nmm(x2c, norm_mix[1], 1.0 + sc_a[1], sh_a[1], w_odd, rows_per_group=B * TC, tn=tn_od, rope=rope_c,
               rope_cols=2 * hk)
    s0 = jnp.zeros((B, RET_HEADS, RET_DK, RET_DV), F32)
    _, _, s_f, s_b = _ret(m2c, _ret_tables(ret_log_decay[0], min(RET_CHUNK, TC)), s0, s0, batch=B, t=TC)
    orf, orb, _, _ = _ret(m2l, _ret_tables(ret_log_decay[0], min(RET_CHUNK, T)), s_f, s_b, batch=B, t=T)
    x3 = _mixout(orf, orb, m2l, 3, ret_norm[0], None, od_w_out[0].astype(BF16), x2l, g_a[0], rows_per_group=T,
                 n_heads=RET_HEADS, dv=RET_DV, center=True)

    wr = jnp.concatenate([moe_router[0], jnp.zeros((D, 128 - N_EXPERTS), F32)], axis=1)
    meta, gate, counts = _router(x3, norm_ffn[1], 1.0 + sc_f[0], sh_f[0], wr, rows_per_group=T)
    tm_e = _tile(T, MOE_ROWS)
    n_tiles = (2 * B * T) // tm_e + N_EXPERTS
    pos, te, n_valid, zero_plan = _dispatch_plan(meta, counts, tm_e, n_tiles)
    tm_r = _tile(T, ROUTER_ROWS)
    pos_t = jnp.transpose(pos.reshape(2, -1, tm_r), (1, 0, 2))
    hs = _scatter_rows(x3, norm_ffn[1], 1.0 + sc_f[0], sh_f[0], pos_t, zero_plan, n_tiles=n_tiles,
                       group_rows=tm_e, rows_per_group=T)
    ys = _ffn(hs, moe_w_gate[0].astype(BF16), moe_w_up[0].astype(BF16), moe_w_down[0].astype(BF16), te, n_valid,
              tm=tm_e)
    out = _combine(ys, pos_t, x3, gate, g_f[0], norm_final, rows_per_group=T)
    return out.reshape(B, T, D)
```

```python
import functools

import jax
import jax.numpy as jnp
import numpy as np
from jax import lax
from jax.experimental import pallas as pl
from jax.experimental.pallas import tpu as pltpu

F32 = jnp.float32
BF16 = jnp.bfloat16
EPS = 1e-6
ROPE_BASE = 10000.0
GRID_W = 64

GLA_HEADS, GLA_DK, GLA_DV, GLA_RANK, GLA_GATE_NORM, GLA_CHUNK = 4, 128, 256, 16, 16.0, 64
MLA_HEADS, MLA_Q_RANK, MLA_KV_RANK, MLA_NOPE, MLA_ROPE, MLA_DV = 8, 768, 512, 128, 64, 128
MLA_DQK = MLA_NOPE + MLA_ROPE
MLA_SCALE = MLA_DQK ** -0.5
MLA_Q_SCALE = MLA_SCALE * 1.4426950408889634
RET_HEADS, RET_DK, RET_DV = 8, 256, 256
RET_CHUNK = 256
N_EXPERTS = 8

EV_GV, EV_GR, EV_GQ, EV_GK, EV_MQ, EV_KR, EV_MKV, EV_MAIN = 0, 1024, 2048, 2560, 3072, 3840, 4096, 4608
EV_SMALL = 128

V7X_VMEM_BYTES = 64 << 20
VMEM_HEADROOM_BYTES = 6 << 20
FFN_ROWS = 512
MOE_ROWS = 1024
FFN_SMALL_ROWS = 256
ROUTER_ROWS = 512
ROW_DMA_UNROLL = 8


def _vmem(nbytes):
    return int(min(V7X_VMEM_BYTES - VMEM_HEADROOM_BYTES, max(32 << 20, nbytes + (8 << 20))))


def _tile(n, pref, align=128):
    if n <= pref:
        return n
    t = (pref // align) * align
    while t >= align:
        if n % t == 0:
            return t
        t -= align
    return n


def _split_bf16(a):
    hi = a.astype(BF16)
    lo = (a - hi.astype(F32)).astype(BF16)
    return hi, lo


def _dot(a, b):
    return jnp.dot(a, b, preferred_element_type=F32)


def _dot3(a, b):
    ah, al = _split_bf16(a)
    bh, bl = _split_bf16(b)
    return _dot(ah, bh) + _dot(al, bh) + _dot(ah, bl)


def _dot_nt(a, b):
    return lax.dot_general(a, b, (((1,), (1,)), ((), ())), preferred_element_type=F32)


def _dot_tn(a, b):
    return lax.dot_general(a, b, (((0,), (0,)), ((), ())), preferred_element_type=F32)


def _silu(x):
    return x * (1.0 / (1.0 + jnp.exp(-x)))


def _log_sigmoid(z):
    return -(jnp.maximum(-z, 0.0) + jnp.log(1.0 + jnp.exp(-jnp.abs(z))))


def _rms(x, g):
    return x * lax.rsqrt(jnp.mean(x * x, axis=-1, keepdims=True) + EPS) * g


def _pack_rows(y):
    half = y.shape[1] // 2
    lo = lax.bitcast_convert_type(y[:, :half].astype(BF16).astype(F32), jnp.uint32)
    hi = lax.bitcast_convert_type(y[:, half:].astype(BF16).astype(F32), jnp.uint32)
    return lax.shift_right_logical(lo, jnp.uint32(16)) | hi


def _packed_shape(m, d):
    return (m, d // 2), jnp.uint32


def _unpack_rows(w):
    lo = lax.bitcast_convert_type(lax.shift_left(w, jnp.uint32(16)), F32)
    hi = lax.bitcast_convert_type(w & jnp.uint32(0xFFFF0000), F32)
    return jnp.concatenate([lo, hi], axis=1)


def _ada_kernel(c_ref, w_ref, b_ref, o_ref):
    o_ref[0] = _dot3(_silu(c_ref[...]), w_ref[0]) + b_ref[0]


def _ada(c_rows, w, b):
    nl, d, n = w.shape
    tn = _tile(n, 1536)
    return pl.pallas_call(
        _ada_kernel,
        grid=(nl, n // tn),
        in_specs=[pl.BlockSpec((8, d), lambda l, j: (0, 0)),
                  pl.BlockSpec((1, d, tn), lambda l, j: (l, 0, j)),
                  pl.BlockSpec((1, 1, tn), lambda l, j: (l, 0, j))],
        out_specs=pl.BlockSpec((1, 8, tn), lambda l, j: (l, 0, j)),
        out_shape=jax.ShapeDtypeStruct((nl, 8, n), F32),
        compiler_params=pltpu.CompilerParams(dimension_semantics=("parallel", "parallel"),
                                             vmem_limit_bytes=_vmem(2 * d * tn * 4 * 3)),
        name="ada_mod",
    )(c_rows, w, b.reshape(nl, 1, n))


def _nmm_kernel(*refs, has_small, rope_tiles, heads_per_tile):
    x_ref, g_ref, sc_ref, sh_ref, w_ref = refs[:5]
    k = 5
    if has_small:
        w2_ref = refs[k]; k += 1
    if rope_tiles:
        cs_ref, sn_ref = refs[k], refs[k + 1]; k += 2
    o_ref = refs[k]; k += 1
    if has_small:
        o2_ref = refs[k]; k += 1
    hn_ref = refs[k]
    j = pl.program_id(1)

    @pl.when(j == 0)
    def _():
        y = _rms(x_ref[...], g_ref[...]) * sc_ref[0] + sh_ref[0]
        hn_ref[...] = y.astype(BF16)
        if has_small:
            o2_ref[...] = _dot3(y, w2_ref[...])

    acc = _dot(hn_ref[...], w_ref[...])
    if rope_tiles:
        @pl.when(j < rope_tiles)
        def _():
            cs = cs_ref[...]
            sn = sn_ref[...]
            half = cs.shape[1] // 2
            outs = []
            for h in range(heads_per_tile):
                r = acc[:, h * 2 * half:(h + 1) * 2 * half]
                rs = jnp.concatenate([r[:, half:], r[:, :half]], axis=1)
                outs.append(r * cs + rs * sn)
            o_ref[...] = jnp.concatenate(outs, axis=1).astype(o_ref.dtype)

        @pl.when(j >= rope_tiles)
        def _():
            o_ref[...] = acc.astype(o_ref.dtype)
    else:
        o_ref[...] = acc.astype(o_ref.dtype)


def _nmm(x, g, sc, sh, w, *, rows_per_group, tn, w_small=None, rope=None, rope_cols=0):
    m, d = x.shape
    n = w.shape[1]
    tm = _tile(rows_per_group, 1024)
    gpt = rows_per_group // tm
    grp = lambda i, j: (i // gpt, 0, 0)
    in_specs = [pl.BlockSpec((tm, d), lambda i, j: (i, 0)),
                pl.BlockSpec((1, d), lambda i, j: (0, 0)),
                pl.BlockSpec((1, 1, d), grp),
                pl.BlockSpec((1, 1, d), grp),
                pl.BlockSpec((d, tn), lambda i, j: (0, j))]
    args = [x, g.reshape(1, d), sc, sh, w]
    out_specs = [pl.BlockSpec((tm, tn), lambda i, j: (i, j))]
    out_shape = [jax.ShapeDtypeStruct((m, n), BF16)]
    if w_small is not None:
        in_specs.append(pl.BlockSpec((d, EV_SMALL), lambda i, j: (0, 0)))
        args.append(w_small)
        out_specs.append(pl.BlockSpec((tm, EV_SMALL), lambda i, j: (i, 0)))
        out_shape.append(jax.ShapeDtypeStruct((m, EV_SMALL), F32))
    rope_tiles = heads_per_tile = 0
    if rope is not None:
        cs, sn = rope
        t_rows, hd = cs.shape
        rope_tiles, heads_per_tile = rope_cols // tn, tn // hd
        tpb = t_rows // tm
        in_specs += [pl.BlockSpec((tm, hd), lambda i, j: (i % tpb, 0)),
                     pl.BlockSpec((tm, hd), lambda i, j: (i % tpb, 0))]
        args += [cs, sn]
    kern = functools.partial(_nmm_kernel, has_small=w_small is not None, rope_tiles=rope_tiles,
                             heads_per_tile=heads_per_tile)
    res = pl.pallas_call(
        kern,
        grid=(m // tm, n // tn),
        in_specs=in_specs,
        out_specs=out_specs,
        out_shape=out_shape,
        scratch_shapes=[pltpu.VMEM((tm, d), BF16)],
        compiler_params=pltpu.CompilerParams(
            dimension_semantics=("parallel", "arbitrary"),
            vmem_limit_bytes=_vmem(2 * tm * d * 4 + tm * d * 2 + 2 * d * tn * 2 + 5 * tm * tn * 4)),
        name="norm_proj",
    )(*args)
    return res if w_small is not None else res[0]


def _gla_kernel(mf_q, mf_k, mf_v, sm_f, mb_q, mb_k, mb_v, sm_b, wg_ref, bg_ref, s0f_ref, s0b_ref,
                of_ref, ob_ref, sf_ref, sb_ref, st_ref, *, n_sub):
    s = pl.program_id(1)
    c = GLA_CHUNK

    @pl.when(s == 0)
    def _():
        st_ref[0] = s0f_ref[0]
        st_ref[1] = s0b_ref[0]

    rows = n_sub * c
    row = lax.broadcasted_iota(jnp.int32, (c, c), 0)
    col = lax.broadcasted_iota(jnp.int32, (c, c), 1)
    brow = lax.broadcasted_iota(jnp.int32, (rows, rows), 0)
    bcol = lax.broadcasted_iota(jnp.int32, (rows, rows), 1)
    same_chunk = (brow // c) == (bcol // c)
    for d, (q_ref, k_ref, v_ref, sm_ref, o_ref) in enumerate(
            ((mf_q, mf_k, mf_v, sm_f, of_ref), (mb_q, mb_k, mb_v, sm_b, ob_ref))):
        keep = (col <= row) if d == 0 else (col >= row)
        bkeep = jnp.logical_and(same_chunk, (bcol <= brow) if d == 0 else (bcol >= brow))
        tri = jnp.where(bkeep, 1.0, 0.0).astype(BF16)
        z = _dot3(sm_ref[...], wg_ref[d]) + bg_ref[d]
        lh, ll = _split_bf16(_log_sigmoid(z) * (1.0 / GLA_GATE_NORM))
        b_all = _dot(tri, lh) + _dot(tri, ll)
        order = range(n_sub) if d == 0 else range(n_sub - 1, -1, -1)
        for ci in order:
            r0 = ci * c
            for h in range(GLA_HEADS):
                b = b_all[r0:r0 + c, h * GLA_DK:(h + 1) * GLA_DK]
                b_last = b[c - 1:c, :] if d == 0 else b[0:1, :]
                q = q_ref[r0:r0 + c, h * GLA_DK:(h + 1) * GLA_DK].astype(F32) * (GLA_DK ** -0.5)
                k = k_ref[r0:r0 + c, h * GLA_DK:(h + 1) * GLA_DK].astype(F32)
                v = v_ref[r0:r0 + c, h * GLA_DV:(h + 1) * GLA_DV]
                qe = (q * jnp.exp(b)).astype(BF16)
                kd = (k * jnp.exp(-b)).astype(BF16)
                kl = (k * jnp.exp(b_last - b)).astype(BF16)
                att = jnp.where(keep, _dot_nt(qe, kd), 0.0).astype(BF16)
                st = st_ref[d, h]
                o = _dot(att, v) + _dot_nt(qe, st.astype(BF16))
                o_ref[r0:r0 + c, h * GLA_DV:(h + 1) * GLA_DV] = o.astype(o_ref.dtype)
                st_ref[d, h] = jnp.exp(b_last) * st + _dot_tn(v, kl)

    @pl.when(s == pl.num_programs(1) - 1)
    def _():
        sf_ref[0] = st_ref[0]
        sb_ref[0] = st_ref[1]


def _gla(main, small, wg, bg, s0f, s0b, *, batch, t):
    rows = _tile(t, 256, GLA_CHUNK)
    ns = t // rows
    hv, hk = GLA_HEADS * GLA_DV, GLA_HEADS * GLA_DK
    fwd = lambda blk: (lambda b, s: (b * ns + s, blk))
    bwd = lambda blk: (lambda b, s: (b * ns + ns - 1 - s, blk))
    st_spec = pl.BlockSpec((1, GLA_HEADS, GLA_DV, GLA_DK), lambda b, s: (b, 0, 0, 0))
    st_shape = jax.ShapeDtypeStruct((batch, GLA_HEADS, GLA_DV, GLA_DK), F32)

    def io(m):
        return [pl.BlockSpec((rows, hk), m(EV_GQ // hk)), pl.BlockSpec((rows, hk), m(EV_GK // hk)),
                pl.BlockSpec((rows, hv), m(EV_GV // hv)), pl.BlockSpec((rows, EV_SMALL), m(0))]

    return pl.pallas_call(
        functools.partial(_gla_kernel, n_sub=rows // GLA_CHUNK),
        grid=(batch, ns),
        in_specs=io(fwd) + io(bwd) + [
            pl.BlockSpec((2, EV_SMALL, hk), lambda b, s: (0, 0, 0)),
            pl.BlockSpec((2, 1, hk), lambda b, s: (0, 0, 0)),
            st_spec, st_spec],
        out_specs=[pl.BlockSpec((rows, hv), fwd(0)), pl.BlockSpec((rows, hv), bwd(0)), st_spec, st_spec],
        out_shape=[jax.ShapeDtypeStruct((batch * t, hv), BF16), jax.ShapeDtypeStruct((batch * t, hv), BF16),
                   st_shape, st_shape],
        scratch_shapes=[pltpu.VMEM((2, GLA_HEADS, GLA_DV, GLA_DK), F32)],
        compiler_params=pltpu.CompilerParams(dimension_semantics=("parallel", "arbitrary"),
                                             vmem_limit_bytes=_vmem(16 << 20)),
        name="gla_scan",
    )(main, main, main, small, main, main, main, small, wg, bg, s0f, s0b)


def _mla_proj_kernel(xq_ref, xkv_ref, kr_ref, gq_ref, gkv_ref, wq_ref, wkv_ref, cs_ref, sn_ref,
                     q_ref, k_ref, v_ref):
    cs, sn = cs_ref[...], sn_ref[...]
    hq = _rms(xq_ref[...].astype(F32), gq_ref[...]).astype(BF16)
    hkv = _rms(xkv_ref[...].astype(F32), gkv_ref[...]).astype(BF16)
    kr = kr_ref[...].astype(F32)
    krr = kr[:, :MLA_ROPE] * cs + kr[:, MLA_ROPE:2 * MLA_ROPE] * sn
    for h in range(MLA_HEADS):
        r = _dot(hq, wq_ref[h])
        qr = r[:, MLA_NOPE:MLA_NOPE + MLA_ROPE] * cs + r[:, MLA_NOPE + MLA_ROPE:] * sn
        q_ref[0, h] = (jnp.concatenate([r[:, :MLA_NOPE], qr], axis=1) * MLA_Q_SCALE).astype(q_ref.dtype)
        r = _dot(hkv, wkv_ref[h])
        k_ref[0, h] = jnp.concatenate([r[:, :MLA_NOPE], krr], axis=1).astype(k_ref.dtype)
        v_ref[0, h] = r[:, MLA_NOPE:].astype(v_ref.dtype)


def _mla_proj(main, gq, gkv, wq, wkv, cs, sn, *, t):
    m = main.shape[0]
    tm = _tile(t, 512)
    tpb = t // tm
    out = lambda i: (i // tpb, 0, i % tpb, 0)
    rope = lambda i: (i % tpb, 0)
    full = lambda a: pl.BlockSpec(a.shape, lambda i: (0,) * a.ndim)
    gq, gkv = gq.reshape(1, -1), gkv.reshape(1, -1)
    return pl.pallas_call(
        _mla_proj_kernel,
        grid=(m // tm,),
        in_specs=[pl.BlockSpec((tm, MLA_Q_RANK), lambda i: (i, EV_MQ // MLA_Q_RANK)),
                  pl.BlockSpec((tm, MLA_KV_RANK), lambda i: (i, EV_MKV // MLA_KV_RANK)),
                  pl.BlockSpec((tm, 256), lambda i: (i, EV_KR // 256)),
                  full(gq), full(gkv), full(wq), full(wkv),
                  pl.BlockSpec((tm, MLA_ROPE), rope), pl.BlockSpec((tm, MLA_ROPE), rope)],
        out_specs=[pl.BlockSpec((1, MLA_HEADS, tm, MLA_DQK), out),
                   pl.BlockSpec((1, MLA_HEADS, tm, MLA_DQK), out),
                   pl.BlockSpec((1, MLA_HEADS, tm, MLA_DV), out)],
        out_shape=[jax.ShapeDtypeStruct((m // t, MLA_HEADS, t, MLA_DQK), BF16),
                   jax.ShapeDtypeStruct((m // t, MLA_HEADS, t, MLA_DQK), BF16),
                   jax.ShapeDtypeStruct((m // t, MLA_HEADS, t, MLA_DV), BF16)],
        compiler_params=pltpu.CompilerParams(dimension_semantics=("parallel",),
                                             vmem_limit_bytes=_vmem(40 * MLA_HEADS * tm * 256)),
        name="mla_proj",
    )(main, main, main, gq, gkv, wq, wkv, cs, sn)


ATTN_ROWS = 256


def _attn_kernel(*refs, n_main, has_extra):
    if has_extra:
        q_ref, k_ref, v_ref, kx_ref, vx_ref, o_ref, m_ref, acc_ref = refs
    else:
        q_ref, k_ref, v_ref, o_ref, m_ref, acc_ref = refs
    kv = pl.program_id(3)

    @pl.when(kv == 0)
    def _():
        m_ref[...] = jnp.full(m_ref.shape, -jnp.inf, F32)
        acc_ref[...] = jnp.zeros(acc_ref.shape, F32)

    tq = q_ref.shape[2]
    rc = min(ATTN_ROWS, tq)

    key_values = [(k_ref[0, 0], v_ref[0, 0])]
    if has_extra:
        key_values.append((kx_ref[0, 0], vx_ref[0, 0]))
    extra_on = kv == n_main - 1
    ks = [k for k, _ in key_values]
    v_ones = [jnp.concatenate([v, jnp.ones(v.shape, v.dtype)], axis=1) for _, v in key_values]
    for r in range(tq // rc):
        rows = slice(r * rc, (r + 1) * rc)
        q = q_ref[0, 0, rows, :]
        ss = [_dot_nt(q, k) for k in ks]
        if has_extra and n_main > 1:
            ss[1] = jnp.where(extra_on, ss[1], -jnp.inf)
        m_old = m_ref[rows, :]
        m_new = m_old
        for s in ss:
            m_new = jnp.maximum(m_new, jnp.max(s, axis=-1, keepdims=True))
        pv = sum(_dot(jnp.exp2(s - m_new).astype(BF16), vo) for s, vo in zip(ss, v_ones))
        acc_ref[rows, :] = jnp.exp2(m_old - m_new) * acc_ref[rows, :] + pv
        m_ref[rows, :] = m_new

    @pl.when(kv == pl.num_programs(3) - 1)
    def _():
        acc = acc_ref[...]
        o_ref[...] = (acc[:, :MLA_DV] / acc[:, MLA_DV:]).astype(o_ref.dtype)


def _attn(q, k, v, kx=None, vx=None, *, tq_pref=2048, tk_pref=8192):
    b, h, tq_all, dqk = q.shape
    tk_all = k.shape[2]
    tq, tk = _tile(tq_all, tq_pref), _tile(tk_all, tk_pref)
    nq, n_main = tq_all // tq, tk_all // tk
    main = lambda bi, hi, i, j: (bi, hi, j, 0)
    in_specs = [pl.BlockSpec((1, 1, tq, dqk), lambda bi, hi, i, j: (bi, hi, i, 0)),
                pl.BlockSpec((1, 1, tk, dqk), main), pl.BlockSpec((1, 1, tk, MLA_DV), main)]
    args = [q, k, v]
    if kx is not None:
        tx = kx.shape[2]
        in_specs += [pl.BlockSpec((1, 1, tx, dqk), lambda bi, hi, i, j: (bi, hi, 0, 0)),
                     pl.BlockSpec((1, 1, tx, MLA_DV), lambda bi, hi, i, j: (bi, hi, 0, 0))]
        args += [kx, vx]
    return pl.pallas_call(
        functools.partial(_attn_kernel, n_main=n_main, has_extra=kx is not None),
        grid=(b, h, nq, n_main),
        in_specs=in_specs,
        out_specs=pl.BlockSpec((tq, MLA_DV), lambda bi, hi, i, j: (bi * nq + i, hi)),
        out_shape=jax.ShapeDtypeStruct((b * tq_all, h * MLA_DV), BF16),
        scratch_shapes=[pltpu.VMEM((tq, 1), F32), pltpu.VMEM((tq, 2 * MLA_DV), F32)],
        compiler_params=pltpu.CompilerParams(
            dimension_semantics=("parallel", "parallel", "parallel", "arbitrary"),
            vmem_limit_bytes=_vmem(5 * min(ATTN_ROWS, tq) * tk * 4 + 8 * tk * 256 * 2)),
        name="mla_attention",
    )(*args)


def _mixout_kernel(*refs, n_heads, dv, center, has_mla):
    of_ref, ob_ref, gate_ref, g_ref = refs[:4]
    k = 4
    if has_mla:
        mla_ref = refs[k]; k += 1
    w_ref, x_ref, ga_ref, o_ref = refs[k:k + 4]
    o = of_ref[...].astype(F32) + ob_ref[...].astype(F32)
    g = g_ref[...]
    parts = []
    for h in range(n_heads):
        oh = o[:, h * dv:(h + 1) * dv]
        if center:
            oh = oh - jnp.mean(oh, axis=-1, keepdims=True)
        parts.append(oh * lax.rsqrt(jnp.mean(oh * oh, axis=-1, keepdims=True) + EPS) * g)
    lat = jnp.concatenate(parts, axis=1) * _silu(gate_ref[...].astype(F32))
    lhs = lat.astype(BF16)
    if has_mla:
        lhs = jnp.concatenate([lhs, mla_ref[...]], axis=1)
    o_ref[...] = x_ref[...] + ga_ref[0] * _dot(lhs, w_ref[...])


def _mixout(o_f, o_b, main, gate_blk, g, mla, w, x, ga, *, rows_per_group, n_heads, dv, center):
    m, d = x.shape
    hw = n_heads * dv
    tm = _tile(rows_per_group, 512)
    gpt = rows_per_group // tm
    in_specs = [pl.BlockSpec((tm, hw), lambda i: (i, 0)), pl.BlockSpec((tm, hw), lambda i: (i, 0)),
                pl.BlockSpec((tm, hw), lambda i: (i, gate_blk)), pl.BlockSpec((1, dv), lambda i: (0, 0))]
    args = [o_f, o_b, main, g.reshape(1, dv)]
    if mla is not None:
        in_specs.append(pl.BlockSpec((tm, mla.shape[1]), lambda i: (i, 0)))
        args.append(mla)
    in_specs += [pl.BlockSpec(w.shape, lambda i: (0, 0)), pl.BlockSpec((tm, d), lambda i: (i, 0)),
                 pl.BlockSpec((1, 1, d), lambda i: (i // gpt, 0, 0))]
    args += [w, x, ga]
    return pl.pallas_call(
        functools.partial(_mixout_kernel, n_heads=n_heads, dv=dv, center=center, has_mla=mla is not None),
        grid=(m // tm,),
        in_specs=in_specs,
        out_specs=pl.BlockSpec((tm, d), lambda i: (i, 0)),
        out_shape=jax.ShapeDtypeStruct((m, d), F32),
        compiler_params=pltpu.CompilerParams(
            dimension_semantics=("parallel",),
            vmem_limit_bytes=_vmem(2 * w.size * 2 + 4 * tm * d * 4 + 8 * tm * hw * 2 + 6 * tm * hw * 4)),
        name="mixer_out",
    )(*args)


def _ffn_kernel(te_ref, nv_ref, *refs, prenorm):
    if prenorm:
        x_ref, g_ref, sc_ref, sh_ref, wg_ref, wu_ref, wd_ref, gf_ref, o_ref, hn_ref, acc_ref = refs
    else:
        x_ref, wg_ref, wu_ref, wd_ref, o_ref, hn_ref, acc_ref = refs
    i, j = pl.program_id(0), pl.program_id(1)
    valid = i < nv_ref[0]

    @pl.when(jnp.logical_and(valid, j == 0))
    def _():
        if prenorm:
            y = _rms(x_ref[...], g_ref[...]) * sc_ref[0] + sh_ref[0]
        else:
            y = _unpack_rows(x_ref[...])
        hn_ref[...] = y.astype(BF16)

    def accumulate(rows, first):
        hn = hn_ref[rows, :]
        a = (_silu(_dot(hn, wg_ref[0])) * _dot(hn, wu_ref[0])).astype(BF16)
        if first:
            acc_ref[rows, :] = _dot(a, wd_ref[0])
        else:
            acc_ref[rows, :] += _dot(a, wd_ref[0])

    def accumulate_all(cond):
        pl.when(jnp.logical_and(cond, j == 0))(lambda: accumulate(slice(None), True))
        pl.when(jnp.logical_and(cond, j > 0))(lambda: accumulate(slice(None), False))

    small_rows = min(FFN_SMALL_ROWS, hn_ref.shape[0])
    if prenorm or small_rows == hn_ref.shape[0]:
        accumulate_all(valid)
    else:
        few = jnp.logical_and(valid, nv_ref[1 + i] <= small_rows)
        accumulate_all(jnp.logical_and(valid, jnp.logical_not(few)))

        @pl.when(few)
        def _():
            @pl.when(j == 0)
            def _():
                acc_ref[...] = jnp.zeros(acc_ref.shape, F32)
            accumulate(slice(0, small_rows), False)

    last = j == pl.num_programs(1) - 1

    @pl.when(jnp.logical_and(valid, last))
    def _():
        if prenorm:
            o_ref[...] = x_ref[...] + gf_ref[0] * acc_ref[...]
        else:
            o_ref[...] = _pack_rows(acc_ref[...])

    @pl.when(jnp.logical_and(jnp.logical_not(valid), last))
    def _():
        o_ref[...] = jnp.zeros(o_ref.shape, o_ref.dtype)


def _ffn(x, wg, wu, wd, tile_expert, n_valid, *, tm, norm=None):
    p = x.shape[0]
    d = wg.shape[1]
    f = wg.shape[2]
    tf = _tile(f, 512)
    nt = p // tm
    wmap_in = lambda i, j, te, nv: (te[i], 0, j)
    wmap_out = lambda i, j, te, nv: (te[i], j, 0)
    tail = (0,) * (x.ndim - 1)
    row_block = (tm,) + x.shape[1:]
    row = lambda i, j, te, nv: (i,) + tail
    in_specs = [pl.BlockSpec(row_block, lambda i, j, te, nv: (jnp.minimum(i, nv[0] - 1),) + tail)]
    args = [x]
    if norm is not None:
        g, sc, sh, gf, rpg = norm
        gpt = rpg // tm
        grp = lambda i, j, te, nv: (i // gpt, 0, 0)
        in_specs += [pl.BlockSpec((1, d), lambda i, j, te, nv: (0, 0)), pl.BlockSpec((1, 1, d), grp),
                     pl.BlockSpec((1, 1, d), grp)]
        args += [g.reshape(1, d), sc, sh]
    in_specs += [pl.BlockSpec((1, d, tf), wmap_in), pl.BlockSpec((1, d, tf), wmap_in),
                 pl.BlockSpec((1, tf, d), wmap_out)]
    args += [wg, wu, wd]
    if norm is not None:
        in_specs.append(pl.BlockSpec((1, 1, d), grp))
        args.append(gf)
    return pl.pallas_call(
        functools.partial(_ffn_kernel, prenorm=norm is not None),
        grid_spec=pltpu.PrefetchScalarGridSpec(
            num_scalar_prefetch=2,
            grid=(nt, f // tf),
            in_specs=in_specs,
            out_specs=pl.BlockSpec(row_block, row),
            scratch_shapes=[pltpu.VMEM((tm, d), BF16), pltpu.VMEM((tm, d), F32)]),
        out_shape=jax.ShapeDtypeStruct(x.shape, x.dtype),
        compiler_params=pltpu.CompilerParams(
            dimension_semantics=("parallel", "arbitrary"),
            vmem_limit_bytes=_vmem(16 * x.size // nt + tm * d * 6 + 6 * d * tf * 2 + 5 * tm * tf * 4)),
        name="swiglu_ffn",
    )(tile_expert, n_valid, *args)


def _ret_kernel(qf_ref, kf_ref, vf_ref, qb_ref, kb_ref, vb_ref, dm_ref, qd_ref, kd_ref, cd_ref,
                s0f_ref, s0b_ref, of_ref, ob_ref, sf_ref, sb_ref, st_ref, *, n_sub):
    s = pl.program_id(1)
    c = dm_ref.shape[2]

    @pl.when(s == 0)
    def _():
        st_ref[0] = s0f_ref[0]
        st_ref[1] = s0b_ref[0]

    for d, (q_ref, k_ref, v_ref, o_ref) in enumerate(((qf_ref, kf_ref, vf_ref, of_ref),
                                                       (qb_ref, kb_ref, vb_ref, ob_ref))):
        order = range(n_sub) if d == 0 else range(n_sub - 1, -1, -1)
        for ci in order:
            r0 = ci * c
            for h in range(RET_HEADS):
                q = q_ref[r0:r0 + c, h * RET_DK:(h + 1) * RET_DK]
                k = (k_ref[r0:r0 + c, h * RET_DK:(h + 1) * RET_DK].astype(F32) * (RET_DK ** -0.5)).astype(BF16)
                v = v_ref[r0:r0 + c, h * RET_DV:(h + 1) * RET_DV]
                att = (_dot_nt(q, k) * dm_ref[d, h]).astype(BF16)
                st = st_ref[d, h]
                o = _dot(att, v) + qd_ref[d, h] * _dot(q, st.astype(BF16))
                o_ref[r0:r0 + c, h * RET_DV:(h + 1) * RET_DV] = o.astype(o_ref.dtype)
                vk = (v.astype(F32) * kd_ref[d, h]).astype(BF16)
                st_ref[d, h] = cd_ref[d, h] * st + _dot_tn(k, vk)

    @pl.when(s == pl.num_programs(1) - 1)
    def _():
        sf_ref[0] = st_ref[0]
        sb_ref[0] = st_ref[1]


def _ret(main, tables, s0f, s0b, *, batch, t):
    chunk = tables[0].shape[2]
    rows = _tile(t, 256, chunk)
    ns = t // rows
    hk, hv = RET_HEADS * RET_DK, RET_HEADS * RET_DV
    fwd = lambda blk: (lambda b, s: (b * ns + s, blk))
    bwd = lambda blk: (lambda b, s: (b * ns + ns - 1 - s, blk))
    st_spec = pl.BlockSpec((1, RET_HEADS, RET_DK, RET_DV), lambda b, s: (b, 0, 0, 0))
    st_shape = jax.ShapeDtypeStruct((batch, RET_HEADS, RET_DK, RET_DV), F32)
    io = lambda m: [pl.BlockSpec((rows, hk), m(0)), pl.BlockSpec((rows, hk), m(1)), pl.BlockSpec((rows, hv), m(2))]
    full = lambda a: pl.BlockSpec(a.shape, lambda b, s: (0,) * a.ndim)
    return pl.pallas_call(
        functools.partial(_ret_kernel, n_sub=rows // chunk),
        grid=(batch, ns),
        in_specs=io(fwd) + io(bwd) + [full(a) for a in tables] + [st_spec, st_spec],
        out_specs=[pl.BlockSpec((rows, hv), fwd(0)), pl.BlockSpec((rows, hv), bwd(0)), st_spec, st_spec],
        out_shape=[jax.ShapeDtypeStruct((batch * t, hv), BF16), jax.ShapeDtypeStruct((batch * t, hv), BF16),
                   st_shape, st_shape],
        scratch_shapes=[pltpu.VMEM((2, RET_HEADS, RET_DK, RET_DV), F32)],
        compiler_params=pltpu.CompilerParams(dimension_semantics=("parallel", "arbitrary"),
                                             vmem_limit_bytes=_vmem(46 << 20)),
        name="retention_scan",
    )(main, main, main, main, main, main, *tables, s0f, s0b)


def _router_kernel(x_ref, g_ref, sc_ref, sh_ref, wr_ref, meta_ref, gate_ref, cnt_ref, run_ref):
    i = pl.program_id(0)

    @pl.when(i == 0)
    def _():
        run_ref[...] = jnp.zeros(run_ref.shape, F32)

    y = _rms(x_ref[...], g_ref[...]) * sc_ref[0] + sh_ref[0]
    tm = y.shape[0]
    lt = jnp.transpose(_dot3(y, wr_ref[...]))[:N_EXPERTS, :]
    eid = lax.broadcasted_iota(jnp.int32, lt.shape, 0).astype(F32)
    v1 = jnp.max(lt, axis=0, keepdims=True)
    e1 = jnp.min(jnp.where(lt == v1, eid, float(N_EXPERTS)), axis=0, keepdims=True)
    lt2 = jnp.where(eid == e1, -jnp.inf, lt)
    v2 = jnp.max(lt2, axis=0, keepdims=True)
    e2 = jnp.min(jnp.where(lt2 == v2, eid, float(N_EXPERTS)), axis=0, keepdims=True)
    ex = jnp.exp(v2 - v1)
    w1 = 1.0 / (1.0 + ex)
    w2 = ex / (1.0 + ex)
    oh1 = jnp.where(eid == e1, 1.0, 0.0)
    oh2 = jnp.where(eid == e2, 1.0, 0.0)
    mem = oh1 + oh2
    r = lax.broadcasted_iota(jnp.int32, (tm, tm), 0)
    cidx = lax.broadcasted_iota(jnp.int32, (tm, tm), 1)
    before = jnp.where(r < cidx, 1.0, 0.0).astype(BF16)
    excl = _dot(mem.astype(BF16), before) + run_ref[...][:, :1]
    rk1 = jnp.sum(oh1 * excl, axis=0, keepdims=True)
    rk2 = jnp.sum(oh2 * excl, axis=0, keepdims=True)
    run_ref[...] = run_ref[...] + jnp.sum(mem, axis=1, keepdims=True)
    zero = jnp.zeros_like(e1)
    meta_ref[...] = jnp.concatenate([e1, e2, rk1, rk2, w1, w2, zero, zero], axis=0)
    wpad = jnp.concatenate([w1, w2, jnp.zeros((6, tm), F32)], axis=0)
    gate_ref[...] = jnp.transpose(jnp.concatenate([wpad] * 16, axis=0))
    cnt_ref[...] = run_ref[...]


def _router(x, g, sc, sh, wr, *, rows_per_group):
    m, d = x.shape
    tm = _tile(rows_per_group, ROUTER_ROWS)
    gpt = rows_per_group // tm
    grp = lambda i: (i // gpt, 0, 0)
    return pl.pallas_call(
        _router_kernel,
        grid=(m // tm,),
        in_specs=[pl.BlockSpec((tm, d), lambda i: (i, 0)), pl.BlockSpec((1, d), lambda i: (0, 0)),
                  pl.BlockSpec((1, 1, d), grp), pl.BlockSpec((1, 1, d), grp),
                  pl.BlockSpec((d, 128), lambda i: (0, 0))],
        out_specs=[pl.BlockSpec((8, tm), lambda i: (0, i)),
                   pl.BlockSpec((tm, 128), lambda i: (i, 0)), pl.BlockSpec((N_EXPERTS, 128), lambda i: (0, 0))],
        out_shape=[jax.ShapeDtypeStruct((8, m), F32),
                   jax.ShapeDtypeStruct((m, 128), F32), jax.ShapeDtypeStruct((N_EXPERTS, 128), F32)],
        scratch_shapes=[pltpu.VMEM((N_EXPERTS, 128), F32)],
        compiler_params=pltpu.CompilerParams(dimension_semantics=("arbitrary",),
                                             vmem_limit_bytes=_vmem(6 * tm * d * 4)),
        name="moe_router",
    )(x, g.reshape(1, d), sc, sh, wr)


def _scatter_kernel(zs_ref, pos_ref, x_ref, g_ref, sc_ref, sh_ref, o_ref, hbuf, sem, *, group_rows, n_tiles):
    tm = hbuf.shape[0]

    def zero_tile(start):
        start = pl.multiple_of(start, 8)
        for c in range(group_rows // tm):
            cp = pltpu.make_async_copy(hbuf, o_ref.at[pl.ds(start + c * tm, tm)], sem.at[0])
            cp.start()
            cp.wait()

    @pl.when(pl.program_id(0) == 0)
    def _():
        hbuf[...] = jnp.zeros(hbuf.shape, hbuf.dtype)
        for e in range(N_EXPERTS):
            pl.when(zs_ref[e] >= 0)(functools.partial(zero_tile, zs_ref[e]))
            tail = zs_ref[N_EXPERTS] + e
            pl.when(tail < n_tiles)(functools.partial(zero_tile, tail * group_rows))

    hbuf[...] = _pack_rows(_rms(x_ref[...], g_ref[...]) * sc_ref[0] + sh_ref[0])

    def copies(t):
        return (pltpu.make_async_copy(hbuf.at[t], o_ref.at[pos_ref[0, 0, t]], sem.at[0]),
                pltpu.make_async_copy(hbuf.at[t], o_ref.at[pos_ref[0, 1, t]], sem.at[1]))

    def issue(t, carry):
        for cp in copies(t):
            cp.start()
        return carry

    lax.fori_loop(0, tm, issue, 0, unroll=ROW_DMA_UNROLL)

    def drain(t, carry):
        for cp in copies(t):
            cp.wait()
        return carry

    lax.fori_loop(0, tm, drain, 0, unroll=ROW_DMA_UNROLL)


def _scatter_rows(x, g, sc, sh, pos, zero_plan, *, n_tiles, group_rows, rows_per_group):
    m, d = x.shape
    nt, _, tm = pos.shape
    assert group_rows % tm == 0
    gpt = rows_per_group // tm
    grp = lambda i, zs: (i // gpt, 0, 0)
    pshape, pdtype = _packed_shape(tm, d)
    return pl.pallas_call(
        functools.partial(_scatter_kernel, group_rows=group_rows, n_tiles=n_tiles),
        grid_spec=pltpu.PrefetchScalarGridSpec(
            num_scalar_prefetch=1,
            grid=(nt,),
            in_specs=[pl.BlockSpec((1, 2, tm), lambda i, zs: (i, 0, 0), memory_space=pltpu.SMEM),
                      pl.BlockSpec((tm, d), lambda i, zs: (i, 0)), pl.BlockSpec((1, d), lambda i, zs: (0, 0)),
                      pl.BlockSpec((1, 1, d), grp), pl.BlockSpec((1, 1, d), grp)],
            out_specs=pl.BlockSpec(memory_space=pl.ANY),
            scratch_shapes=[pltpu.VMEM(pshape, pdtype), pltpu.SemaphoreType.DMA((2,))]),
        out_shape=jax.ShapeDtypeStruct((n_tiles * group_rows,) + pshape[1:], pdtype),
        compiler_params=pltpu.CompilerParams(dimension_semantics=("arbitrary",),
                                             vmem_limit_bytes=_vmem(6 * tm * d * 4)),
        name="moe_scatter",
    )(zero_plan, pos, x, g.reshape(1, d), sc, sh)


def _combine_kernel(pos_ref, y_ref, x_ref, gate_ref, gf_ref, gn_ref, o_ref, buf_ref, sem):
    tm = pos_ref.shape[2]

    def copies(t):
        return (pltpu.make_async_copy(y_ref.at[pos_ref[0, 0, t]], buf_ref.at[0, t], sem.at[0]),
                pltpu.make_async_copy(y_ref.at[pos_ref[0, 1, t]], buf_ref.at[1, t], sem.at[1]))

    def issue(t, carry):
        for cp in copies(t):
            cp.start()
        return carry

    lax.fori_loop(0, tm, issue, 0, unroll=ROW_DMA_UNROLL)

    def drain(t, carry):
        for cp in copies(t):
            cp.wait()
        return carry

    lax.fori_loop(0, tm, drain, 0, unroll=ROW_DMA_UNROLL)
    w = gate_ref[...]
    moe = w[:, 0:1] * _unpack_rows(buf_ref[0]) + w[:, 1:2] * _unpack_rows(buf_ref[1])
    o_ref[...] = _rms(x_ref[...] + gf_ref[0] * moe, gn_ref[...])


def _combine(y, pos, x, gate, gf, gn, *, rows_per_group):
    m, d = x.shape
    nt, _, tm = pos.shape
    gpt = rows_per_group // tm
    return pl.pallas_call(
        _combine_kernel,
        grid=(nt,),
        in_specs=[pl.BlockSpec((1, 2, tm), lambda i: (i, 0, 0), memory_space=pltpu.SMEM),
                  pl.BlockSpec(memory_space=pl.ANY),
                  pl.BlockSpec((tm, d), lambda i: (i, 0)), pl.BlockSpec((tm, 128), lambda i: (i, 0)),
                  pl.BlockSpec((1, 1, d), lambda i: (i // gpt, 0, 0)), pl.BlockSpec((1, d), lambda i: (0, 0))],
        out_specs=pl.BlockSpec((tm, d), lambda i: (i, 0)),
        out_shape=jax.ShapeDtypeStruct((m, d), F32),
        scratch_shapes=[pltpu.VMEM((2, tm) + y.shape[1:], y.dtype), pltpu.SemaphoreType.DMA((2,))],
        compiler_params=pltpu.CompilerParams(dimension_semantics=("arbitrary",),
                                             vmem_limit_bytes=_vmem(8 * tm * d * 4)),
        name="moe_combine",
    )(pos, y, x, gate, gf, gn.reshape(1, d))


def _axial_angles(rows, dim):
    row = jnp.repeat(jnp.arange(rows, dtype=F32), GRID_W)
    col = jnp.tile(jnp.arange(GRID_W, dtype=F32), rows)
    half = dim // 2
    inv = 1.0 / (ROPE_BASE ** (jnp.arange(0, half, 2, dtype=F32) / half))
    return jnp.concatenate([row[:, None] * inv, col[:, None] * inv], axis=-1)


def _rope_tables(rows, dim, t_ctx):
    ang = _axial_angles(rows, dim)
    cs = jnp.concatenate([jnp.cos(ang), jnp.cos(ang)], axis=1)
    sn = jnp.concatenate([-jnp.sin(ang), jnp.sin(ang)], axis=1)
    return (cs, sn), (jnp.ones((t_ctx, dim), F32), jnp.zeros((t_ctx, dim), F32))


def _evens_odds(w):
    pairs = w.reshape(w.shape[:-1] + (w.shape[-1] // 2, 2))
    return pairs[..., 0], pairs[..., 1]


def _even_weights(w_in, w_g2, b_g2, w_uq, w_ukv):
    d = w_in.shape[0]
    sizes = (512, 512, 1024, 1024, 16, 16, MLA_Q_RANK, MLA_KV_RANK, MLA_ROPE)
    offs = np.cumsum((0,) + sizes)
    gq, gk, gv, gr, gaf, gab, mq, mkv, mkr = [w_in[:, offs[i]:offs[i + 1]] for i in range(9)]
    kr_e, kr_o = _evens_odds(mkr)
    w_main = jnp.concatenate([gv, gr, gq, gk, mq, kr_e, kr_o, kr_o, kr_e, jnp.zeros((d, 128), F32), mkv],
                             axis=1).astype(BF16)
    w_small = jnp.concatenate([gaf, gab, jnp.zeros((d, EV_SMALL - 2 * GLA_RANK), F32)], axis=1)
    wg = jnp.zeros((2, EV_SMALL, GLA_HEADS * GLA_DK), F32)
    wg = wg.at[0, :GLA_RANK].set(w_g2[0]).at[1, GLA_RANK:2 * GLA_RANK].set(w_g2[1])
    bg = b_g2.reshape(2, 1, -1)
    wq = w_uq.reshape(MLA_Q_RANK, MLA_HEADS, MLA_DQK)
    qr_e, qr_o = _evens_odds(wq[:, :, MLA_NOPE:])
    wq = jnp.concatenate([wq[:, :, :MLA_NOPE], qr_e, qr_o, qr_o, qr_e], axis=2)
    wq = jnp.transpose(wq, (1, 0, 2)).astype(BF16)
    wkv = jnp.transpose(w_ukv.reshape(MLA_KV_RANK, MLA_HEADS, MLA_NOPE + MLA_DV), (1, 0, 2)).astype(BF16)
    return w_main, w_small, wg, bg, wq, wkv


def _perm_kernel(w_ref, p_ref, o_ref):
    o_ref[...] = _dot(w_ref[...].astype(BF16), p_ref[...]).astype(o_ref.dtype)


def _odd_weights(w_in):
    d, n = w_in.shape
    hd = RET_DK
    n_perm = 2 * RET_HEADS
    src = np.concatenate([np.arange(0, hd, 2), np.arange(1, hd, 2)])
    perm = np.zeros((2, hd, hd), np.float32)
    perm[0, src, np.arange(hd)] = 1.0
    perm[1] = np.eye(hd, dtype=np.float32)
    return pl.pallas_call(
        _perm_kernel,
        grid=(n // hd,),
        in_specs=[pl.BlockSpec((d, hd), lambda j: (0, j)),
                  pl.BlockSpec((None, hd, hd), lambda j: (jnp.where(j < n_perm, 0, 1), 0, 0))],
        out_specs=pl.BlockSpec((d, hd), lambda j: (0, j)),
        out_shape=jax.ShapeDtypeStruct((d, n), BF16),
        compiler_params=pltpu.CompilerParams(dimension_semantics=("parallel",)),
        name="weight_reorder",
    )(w_in, jnp.asarray(perm, BF16))


def _ret_tables(log_decay, c):
    lg = -jnp.exp(log_decay.astype(F32))
    pos = jnp.arange(c, dtype=F32)
    diff = pos[:, None] - pos[None, :]
    lgd = lg[:, :, None, None]
    fmask = jnp.where(diff >= 0, jnp.exp(lgd * jnp.maximum(diff, 0.0)), 0.0)
    bmask = jnp.where(diff <= 0, jnp.exp(lgd * jnp.maximum(-diff, 0.0)), 0.0)
    dm = jnp.stack([fmask[0], bmask[1]])
    lgc = lg[:, :, None, None]
    qd = jnp.stack([jnp.exp(lgc[0] * (pos + 1.0)[None, :, None]), jnp.exp(lgc[1] * (c - pos)[None, :, None])])
    kd = jnp.stack([jnp.exp(lgc[0] * (c - 1.0 - pos)[None, :, None]), jnp.exp(lgc[1] * pos[None, :, None])])
    cd = jnp.exp(lgc * c)
    return dm, qd, kd, cd


def _dispatch_plan(meta, counts, tm, n_tiles):
    cnt = counts[:, 0].astype(jnp.int32)
    tiles = (cnt + tm - 1) // tm
    tile_end = jnp.cumsum(tiles)
    start = (tile_end - tiles) * tm
    e = meta[0:2].astype(jnp.int32)
    start_e = sum(jnp.where(e == k, start[k], 0) for k in range(N_EXPERTS))
    pos = start_e + meta[2:4].astype(jnp.int32)
    tile_ids = jnp.arange(n_tiles, dtype=jnp.int32)
    n_valid = tile_end[-1:].astype(jnp.int32)
    last_valid = jnp.minimum(tile_ids, n_valid[0] - 1)
    te = jnp.sum((last_valid[:, None] >= tile_end[None, :]).astype(jnp.int32), axis=1)
    te = jnp.minimum(te, N_EXPERTS - 1).astype(jnp.int32)
    zero_start = jnp.where(tiles > 0, (tile_end - 1) * tm, -1).astype(jnp.int32)
    first_tile = sum(jnp.where(te == k, (tile_end - tiles)[k], 0) for k in range(N_EXPERTS))
    cnt_te = sum(jnp.where(te == k, cnt[k], 0) for k in range(N_EXPERTS))
    tile_rows = jnp.clip(cnt_te - (tile_ids - first_tile) * tm, 0, tm).astype(jnp.int32)
    return pos, te, jnp.concatenate([n_valid, tile_rows]), jnp.concatenate([zero_start, n_valid])


def kernel(x, c, ctx, c_ctx, ada_w, ada_b, norm_mix, norm_ffn, norm_final, ev_w_in, gla_w_gate2, gla_b_gate2, gla_norm, mla_q_norm, mla_w_uq, mla_kv_norm, mla_w_ukv, ev_w_out, ffn_w_gate, ffn_w_up, ffn_w_down, od_w_in, ret_log_decay, ret_norm, od_w_out, moe_router, moe_w_gate, moe_w_up, moe_w_down):
    B, T, D = x.shape
    TC = ctx.shape[1]
    depth = ada_w.shape[0]
    assert depth == 2 and T % GRID_W == 0 and B <= 7
    rows = T // GRID_W
    xl = x.reshape(B * T, D)
    xc = ctx.reshape(B * TC, D)

    c_rows = jnp.concatenate([c, c_ctx[None, :], jnp.zeros((8 - B - 1, D), F32)], axis=0)
    m_all = _ada(c_rows, ada_w, ada_b).reshape(depth, 8, 6, 1, D)
    mods = [[(m_all[i, :B, k], m_all[i, B:B + 1, k]) for k in range(6)] for i in range(depth)]

    (sh_a, sc_a, g_a, sh_f, sc_f, g_f) = mods[0]
    w_main, w_small, wg2, bg2, wq, wkv = _even_weights(ev_w_in[0], gla_w_gate2[0], gla_b_gate2[0],
                                                       mla_w_uq[0], mla_w_ukv[0])
    tn_ev = _tile(EV_MAIN, 768)
    main_l, small_l = _nmm(xl, norm_mix[0], 1.0 + sc_a[0], sh_a[0], w_main, rows_per_group=T, tn=tn_ev,
                           w_small=w_small)
    main_c, small_c = _nmm(xc, norm_mix[0], 1.0 + sc_a[1], sh_a[1], w_main, rows_per_group=B * TC, tn=tn_ev,
                           w_small=w_small)

    s0 = jnp.zeros((B, GLA_HEADS, GLA_DV, GLA_DK), F32)
    ocf, ocb, s_f, s_b = _gla(main_c, small_c, wg2, bg2, s0, s0, batch=B, t=TC)
    olf, olb, _, _ = _gla(main_l, small_l, wg2, bg2, s_f, s_b, batch=B, t=T)

    (cs_m, sn_m), (cs_1, sn_0) = _rope_tables(rows, MLA_ROPE, TC)
    q_l, k_l, v_l = _mla_proj(main_l, mla_q_norm[0], mla_kv_norm[0], wq, wkv, cs_m, sn_m, t=T)
    q_c, k_c, v_c = _mla_proj(main_c, mla_q_norm[0], mla_kv_norm[0], wq, wkv, cs_1, sn_0, t=TC)
    mla_l = _attn(q_l, k_l, v_l, k_c, v_c)
    mla_c = _attn(q_c, k_c, v_c)

    w_out = ev_w_out[0].astype(BF16)
    gr_blk = EV_GR // (GLA_HEADS * GLA_DV)
    x1l = _mixout(olf, olb, main_l, gr_blk, gla_norm[0], mla_l, w_out, xl, g_a[0], rows_per_group=T,
                  n_heads=GLA_HEADS, dv=GLA_DV, center=False)
    x1c = _mixout(ocf, ocb, main_c, gr_blk, gla_norm[0], mla_c, w_out, xc, g_a[1], rows_per_group=B * TC,
                  n_heads=GLA_HEADS, dv=GLA_DV, center=False)

    fwg, fwu, fwd_ = ffn_w_gate.astype(BF16), ffn_w_up.astype(BF16), ffn_w_down.astype(BF16)
    tm_l = _tile(T, FFN_ROWS)
    x2l = _ffn(x1l, fwg, fwu, fwd_, jnp.zeros((B * T // tm_l,), jnp.int32), jnp.full((1,), B * T // tm_l, jnp.int32),
               tm=tm_l, norm=(norm_ffn[0], 1.0 + sc_f[0], sh_f[0], g_f[0], T))
    tm_c = _tile(B * TC, FFN_ROWS)
    x2c = _ffn(x1c, fwg, fwu, fwd_, jnp.zeros((B * TC // tm_c,), jnp.int32),
               jnp.full((1,), B * TC // tm_c, jnp.int32), tm=tm_c,
               norm=(norm_ffn[0], 1.0 + sc_f[1], sh_f[1], g_f[1], B * TC))

    (sh_a, sc_a, g_a, sh_f, sc_f, g_f) = mods[1]
    w_odd = _odd_weights(od_w_in[0])
    hk = RET_HEADS * RET_DK
    tn_od = _tile(4 * hk, 1024)
    rope_l, rope_c = _rope_tables(rows, RET_DK, B * TC)
    m2l = _nmm(x2l, norm_mix[1], 1.0 + sc_a[0], sh_a[0], w_odd, rows_per_group=T, tn=tn_od, rope=rope_l,
               rope_cols=2 * hk)
    m2c = _nmm(x2c, norm_mix[1], 1.0 + sc_a[1], sh_a[1], w_odd, rows_per_group=B * TC, tn=tn_od, rope=rope_c,
               rope_cols=2 * hk)
    s0 = jnp.zeros((B, RET_HEADS, RET_DK, RET_DV), F32)
    _, _, s_f, s_b = _ret(m2c, _ret_tables(ret_log_decay[0], min(RET_CHUNK, TC)), s0, s0, batch=B, t=TC)
    orf, orb, _, _ = _ret(m2l, _ret_tables(ret_log_decay[0], min(RET_CHUNK, T)), s_f, s_b, batch=B, t=T)
    x3 = _mixout(orf, orb, m2l, 3, ret_norm[0], None, od_w_out[0].astype(BF16), x2l, g_a[0], rows_per_group=T,
                 n_heads=RET_HEADS, dv=RET_DV, center=True)

    wr = jnp.concatenate([moe_router[0], jnp.zeros((D, 128 - N_EXPERTS), F32)], axis=1)
    meta, gate, counts = _router(x3, norm_ffn[1], 1.0 + sc_f[0], sh_f[0], wr, rows_per_group=T)
    tm_e = _tile(T, MOE_ROWS)
    n_tiles = (2 * B * T) // tm_e + N_EXPERTS
    pos, te, n_valid, zero_plan = _dispatch_plan(meta, counts, tm_e, n_tiles)
    tm_r = _tile(T, ROUTER_ROWS)
    pos_t = jnp.transpose(pos.reshape(2, -1, tm_r), (1, 0, 2))
    hs = _scatter_rows(x3, norm_ffn[1], 1.0 + sc_f[0], sh_f[0], pos_t, zero_plan, n_tiles=n_tiles,
                       group_rows=tm_e, rows_per_group=T)
    ys = _ffn(hs, moe_w_gate[0].astype(BF16), moe_w_up[0].astype(BF16), moe_w_down[0].astype(BF16), te, n_valid,
              tm=tm_e)
    out = _combine(ys, pos_t, x3, gate, g_f[0], norm_final, rows_per_group=T)
    return out.reshape(B, T, D)
```
